```python
import jax, jax.numpy as jnp
from jax import lax
import numpy as np

D_MODEL = 2048
BATCH = 4
SEQ = 4096
DEPTH = 2

N_A = DEPTH // 2
N_B = DEPTH - N_A
GLA_HEADS = 4
GLA_DK = D_MODEL // 2 // GLA_HEADS
GLA_DV = D_MODEL // GLA_HEADS
GLA_QK = GLA_HEADS * GLA_DK
GLA_V = GLA_HEADS * GLA_DV
GLA_GATE_RANK = 16
GLA_TAU = 16.0
GLA_CHUNK = 64
MLA_HEADS = 16
MLA_NOPE = 128
MLA_ROPE = 64
MLA_V = 128
MLA_QK = MLA_NOPE + MLA_ROPE
Q_LORA = 512
KV_LORA = 512
ROPE_THETA = 10000.0
Q_BLOCK = 128
D_FF = -(-8 * D_MODEL // (3 * 256)) * 256
EPS = 1e-6

kernel_name = 'yoco_gla_mla_adaln_hybrid'


def rms_norm(x, gain):
    x32 = x.astype(jnp.float32)
    y = x32 * lax.rsqrt(jnp.mean(x32 * x32, axis=-1, keepdims=True) + EPS)
    return (y * gain.astype(jnp.float32)).astype(x.dtype)


def modulate(x, gain, shift, scale):
    return rms_norm(x, gain) * (1 + scale[:, None, :]) + shift[:, None, :]


def rope(x, cos, sin):
    half = MLA_ROPE // 2
    x32 = x.astype(jnp.float32)
    x1, x2 = x32[..., :half], x32[..., half:]
    return jnp.concatenate([x1 * cos - x2 * sin, x2 * cos + x1 * sin], axis=-1).astype(x.dtype)


def swiglu(h, w_gu, w_down):
    g, u = jnp.split(h @ w_gu, 2, axis=-1)
    return (jax.nn.silu(g) * u) @ w_down


def gla_mixer(h, w_in, w_alpha, b_alpha, onorm, w_out):
    B, S, _ = h.shape
    nc = S // GLA_CHUNK
    proj = h @ w_in
    q, k, v, g, a_lr = jnp.split(proj, [GLA_QK, 2 * GLA_QK, 2 * GLA_QK + GLA_V, 2 * GLA_QK + 2 * GLA_V], axis=-1)
    log_alpha = jax.nn.log_sigmoid((a_lr @ w_alpha + b_alpha).astype(jnp.float32)) / GLA_TAU

    def to_chunks(t, d):
        return t.astype(jnp.float32).reshape(B, nc, GLA_CHUNK, GLA_HEADS, d).transpose(1, 0, 3, 2, 4)

    qc = to_chunks(q, GLA_DK) * (GLA_DK ** -0.5)
    kc = to_chunks(k, GLA_DK)
    vc = to_chunks(v, GLA_DV)
    bcum = jnp.cumsum(to_chunks(log_alpha, GLA_DK), axis=3)
    b_last = bcum[:, :, :, -1:, :]
    q_dec = qc * jnp.exp(bcum)
    k_intra = kc * jnp.exp(-bcum)
    k_state = kc * jnp.exp(b_last - bcum)
    causal = jnp.tril(jnp.ones((GLA_CHUNK, GLA_CHUNK), dtype=bool))
    att = jnp.einsum('nbhik,nbhjk->nbhij', q_dec, k_intra)
    att = jnp.where(causal, att, 0.0)
    o_intra = jnp.einsum('nbhij,nbhjv->nbhiv', att, vc)

    def step(state, inp):
        q_n, k_n, v_n, dec_n = inp
        o_n = jnp.einsum('bhck,bhkv->bhcv', q_n, state)
        state = dec_n[..., None] * state + jnp.einsum('bhck,bhcv->bhkv', k_n, v_n)
        return state, o_n

    s0 = jnp.zeros((B, GLA_HEADS, GLA_DK, GLA_DV), jnp.float32)
    _, o_inter = lax.scan(step, s0, (q_dec, k_state, vc, jnp.exp(b_last[:, :, :, 0, :])))
    o = (o_intra + o_inter).transpose(1, 0, 3, 2, 4).reshape(B, S, GLA_HEADS, GLA_DV)
    o = rms_norm(o, onorm).astype(h.dtype)
    o = o * jax.nn.silu(g).reshape(B, S, GLA_HEADS, GLA_DV)
    return o.reshape(B, S, GLA_V) @ w_out


def shared_kv(x_mid, c, ada_w, ada_b, norm, w_dkv, lat_norm, w_ukv, k_norm, cos, sin):
    B, S, _ = x_mid.shape
    shift, scale = jnp.split(jax.nn.silu(c) @ ada_w + ada_b, 2, axis=-1)
    h = modulate(x_mid, norm, shift, scale)
    ckv = h @ w_dkv
    c_lat = rms_norm(ckv[..., :KV_LORA], lat_norm)
    k_pe = rope(ckv[..., KV_LORA:], cos, sin)
    kv = (c_lat @ w_ukv).reshape(B, S, MLA_HEADS, MLA_NOPE + MLA_V)
    k_nope, v = kv[..., :MLA_NOPE], kv[..., MLA_NOPE:]
    k = jnp.concatenate([k_nope, jnp.broadcast_to(k_pe[:, :, None, :], (B, S, MLA_HEADS, MLA_ROPE))], axis=-1)
    k = rms_norm(k, k_norm)
    return k, v


def mla_mixer(h, k, v, w_dq, q_lat_norm, w_uq, q_norm, w_out, cos, sin):
    B, S, _ = h.shape
    cq = rms_norm(h @ w_dq, q_lat_norm)
    q = (cq @ w_uq).reshape(B, S, MLA_HEADS, MLA_QK)
    q = jnp.concatenate([q[..., :MLA_NOPE], rope(q[..., MLA_NOPE:], cos[:, :, None, :], sin[:, :, None, :])], axis=-1)
    q = rms_norm(q, q_norm)
    nb = S // Q_BLOCK
    qb = q.reshape(B, nb, Q_BLOCK, MLA_HEADS, MLA_QK).transpose(1, 0, 2, 3, 4)
    key_idx = jnp.arange(S)
    sm_scale = MLA_QK ** -0.5

    def attend(args):
        q_blk, blk = args
        s = jnp.einsum('bqhd,bkhd->bhqk', q_blk, k).astype(jnp.float32) * sm_scale
        q_idx = blk * Q_BLOCK + jnp.arange(Q_BLOCK)
        s = jnp.where(key_idx[None, :] <= q_idx[:, None], s, -jnp.inf)
        p = jax.nn.softmax(s, axis=-1).astype(v.dtype)
        return jnp.einsum('bhqk,bkhd->bqhd', p, v)

    o = lax.map(attend, (qb, jnp.arange(nb)))
    o = o.transpose(1, 0, 2, 3, 4).reshape(B, S, MLA_HEADS * MLA_V)
    return o @ w_out


def setup_inputs(seed: int = 0) -> dict:
    key = jax.random.key(seed)
    ks = jax.random.split(key, 32)
    f32 = jnp.float32
    D = D_MODEL

    def nrm(k, shape, std):
        return jax.random.normal(k, shape, f32) * std

    def gain(k, shape):
        return 1.0 + 0.02 * jax.random.normal(k, shape, f32)

    offset = jax.random.randint(ks[2], (BATCH, 1), 0, 1024, dtype=jnp.int32)
    positions = offset + jnp.arange(SEQ, dtype=jnp.int32)[None, :]
    return {
        'x': nrm(ks[0], (BATCH, SEQ, D), 1.0),
        'c': nrm(ks[1], (BATCH, D), 1.0),
        'positions': positions,
        'ada_w': nrm(ks[3], (DEPTH, D, 6 * D), 0.5 * D ** -0.5),
        'ada_b': nrm(ks[4], (DEPTH, 6 * D), 0.01),
        'norm_mix': gain(ks[5], (DEPTH, D)),
        'norm_ffn': gain(ks[6], (DEPTH, D)),
        'gla_w_in': nrm(ks[7], (N_A, D, 2 * GLA_QK + 2 * GLA_V + GLA_GATE_RANK), D ** -0.5),
        'gla_w_alpha': nrm(ks[8], (N_A, GLA_GATE_RANK, GLA_QK), GLA_GATE_RANK ** -0.5),
        'gla_b_alpha': nrm(ks[9], (N_A, GLA_QK), 0.01),
        'gla_onorm': gain(ks[10], (N_A, GLA_DV)),
        'gla_w_out': nrm(ks[11], (N_A, GLA_V, D), GLA_V ** -0.5),
        'mla_w_dq': nrm(ks[12], (N_B, D, Q_LORA), D ** -0.5),
        'mla_q_lat_norm': gain(ks[13], (N_B, Q_LORA)),
        'mla_w_uq': nrm(ks[14], (N_B, Q_LORA, MLA_HEADS * MLA_QK), Q_LORA ** -0.5),
        'mla_q_norm': gain(ks[15], (N_B, MLA_QK)),
        'mla_w_out': nrm(ks[16], (N_B, MLA_HEADS * MLA_V, D), (MLA_HEADS * MLA_V) ** -0.5),
        'kv_ada_w': nrm(ks[17], (D, 2 * D), 0.5 * D ** -0.5),
        'kv_ada_b': nrm(ks[18], (2 * D,), 0.01),
        'kv_norm': gain(ks[19], (D,)),
        'kv_w_dkv': nrm(ks[20], (D, KV_LORA + MLA_ROPE), D ** -0.5),
        'kv_lat_norm': gain(ks[21], (KV_LORA,)),
        'kv_w_ukv': nrm(ks[22], (KV_LORA, MLA_HEADS * (MLA_NOPE + MLA_V)), KV_LORA ** -0.5),
        'kv_k_norm': gain(ks[23], (MLA_QK,)),
        'ffn_w_gu': nrm(ks[24], (DEPTH, D, 2 * D_FF), D ** -0.5),
        'ffn_w_down': nrm(ks[25], (DEPTH, D_FF, D), D_FF ** -0.5),
    }


def reference(x, c, positions, ada_w, ada_b, norm_mix, norm_ffn, gla_w_in, gla_w_alpha, gla_b_alpha,
              gla_onorm, gla_w_out, mla_w_dq, mla_q_lat_norm, mla_w_uq, mla_q_norm, mla_w_out,
              kv_ada_w, kv_ada_b, kv_norm, kv_w_dkv, kv_lat_norm, kv_w_ukv, kv_k_norm,
              ffn_w_gu, ffn_w_down):
    half = MLA_ROPE // 2
    inv_freq = ROPE_THETA ** (-jnp.arange(half, dtype=jnp.float32) / half)
    ang = positions.astype(jnp.float32)[..., None] * inv_freq
    cos, sin = jnp.cos(ang), jnp.sin(ang)
    k_sh, v_sh = None, None
    for layer in range(DEPTH):
        mod = jax.nn.silu(c) @ ada_w[layer] + ada_b[layer]
        shift_m, scale_m, gate_m, shift_f, scale_f, gate_f = jnp.split(mod, 6, axis=-1)
        if layer < N_A:
            h = modulate(x, norm_mix[layer], shift_m, scale_m)
            mix = gla_mixer(h, gla_w_in[layer], gla_w_alpha[layer], gla_b_alpha[layer],
                            gla_onorm[layer], gla_w_out[layer])
        else:
            j = layer - N_A
            if j == 0:
                k_sh, v_sh = shared_kv(x, c, kv_ada_w, kv_ada_b, kv_norm, kv_w_dkv, kv_lat_norm,
                                       kv_w_ukv, kv_k_norm, cos, sin)
            h = modulate(x, norm_mix[layer], shift_m, scale_m)
            mix = mla_mixer(h, k_sh, v_sh, mla_w_dq[j], mla_q_lat_norm[j], mla_w_uq[j],
                            mla_q_norm[j], mla_w_out[j], cos, sin)
        x = x + gate_m[:, None, :] * mix
        h = modulate(x, norm_ffn[layer], shift_f, scale_f)
        x = x + gate_f[:, None, :] * swiglu(h, ffn_w_gu[layer], ffn_w_down[layer])
    return x
```

```python
import functools

import jax
import jax.numpy as jnp
from jax import lax
from jax.experimental import pallas as pl
from jax.experimental.pallas import tpu as pltpu

GLA_HEADS = 4
GLA_GATE_RANK = 16
GLA_TAU = 16.0
GLA_CHUNK = 64
MLA_HEADS = 16
MLA_NOPE = 128
MLA_ROPE = 64
MLA_V = 128
KV_LORA = 512
ROPE_THETA = 10000.0
EPS = 1e-6

LANE = 128
MIB = 1 << 20

F32 = jnp.float32
BF16 = jnp.bfloat16


def _params(semantics, vmem_mib):
    return pltpu.CompilerParams(dimension_semantics=semantics, vmem_limit_bytes=vmem_mib * MIB)


def _silu(x):
    return x * jax.nn.sigmoid(x)


def _rms_scale(x):
    return x * lax.rsqrt(jnp.mean(x * x, axis=-1, keepdims=True) + EPS)


def _ada_kernel(c_ref, w_ref, b_ref, o_ref):
    nb, tn = o_ref.shape
    for b in range(nb):
        sb = _silu(c_ref[b])
        for j in range(tn // LANE):
            cols = slice(j * LANE, (j + 1) * LANE)
            acc = jnp.sum(w_ref[:, cols] * sb, axis=0, keepdims=True)
            o_ref[b:b + 1, cols] = acc + b_ref[:, cols]


def _ada_mod(c_rep, w, b, layer, tn=1024):
    nb, d, _ = c_rep.shape
    n = w.shape[2]
    return pl.pallas_call(
        _ada_kernel,
        out_shape=jax.ShapeDtypeStruct((nb, n), F32),
        grid=(n // tn,),
        in_specs=[
            pl.BlockSpec((nb, d, LANE), lambda j: (0, 0, 0)),
            pl.BlockSpec((None, d, tn), lambda j: (layer, 0, j)),
            pl.BlockSpec((None, 1, tn), lambda j: (layer, 0, j)),
        ],
        out_specs=pl.BlockSpec((nb, tn), lambda j: (0, j)),
        compiler_params=_params(("parallel",), 40),
        name="ada_mod",
    )(c_rep, w, b[:, None, :])


def _trig_kernel(a_ref, c_ref, s_ref):
    a = a_ref[...]
    c_ref[...] = jnp.cos(a)
    s_ref[...] = jnp.sin(a)


def _trig(ang):
    rows = ang.shape[0]
    tr = min(rows, 512)
    spec = pl.BlockSpec((tr, LANE), lambda i: (i, 0))
    return pl.pallas_call(
        _trig_kernel,
        out_shape=(jax.ShapeDtypeStruct(ang.shape, F32),) * 2,
        grid=(rows // tr,),
        in_specs=[spec],
        out_specs=(spec, spec),
        compiler_params=_params(("parallel",), 16),
        name="rope_trig",
    )(ang)


def _modulate(x, gain, shift, scale):
    return (_rms_scale(x) * gain) * (1.0 + scale) + shift


MOD_ROWS = 256


def _modulate_into(h_ref, x_ref, g_ref, sh_ref, sc_ref):
    tm = x_ref.shape[0]
    step = min(tm, MOD_ROWS)
    gain, shift, scale = g_ref[...], sh_ref[0], sc_ref[0]
    for r in range(tm // step):
        rows = slice(r * step, (r + 1) * step)
        h_ref[rows, :] = _modulate(x_ref[rows, :], gain, shift, scale).astype(h_ref.dtype)


def _modmm_kernel(x_ref, g_ref, sh_ref, sc_ref, w_ref, wa_ref, o_ref, a_ref, h_ref):
    @pl.when(pl.program_id(1) == 0)
    def _():
        _modulate_into(h_ref, x_ref, g_ref, sh_ref, sc_ref)
        a_ref[...] = jnp.dot(h_ref[...], wa_ref[...], preferred_element_type=F32).astype(a_ref.dtype)

    o_ref[...] = jnp.dot(h_ref[...], w_ref[...], preferred_element_type=F32).astype(o_ref.dtype)


def _mod_matmul(x, gain, shift, scale, w, w_a, seq, tm, tn):
    t, d = x.shape
    n = w.shape[1]
    per_b = seq // tm
    vec = lambda i, j: (i // per_b, 0, 0)
    return pl.pallas_call(
        _modmm_kernel,
        out_shape=(jax.ShapeDtypeStruct((t, n), BF16), jax.ShapeDtypeStruct((t, LANE), BF16)),
        grid=(t // tm, n // tn),
        in_specs=[
            pl.BlockSpec((tm, d), lambda i, j: (i, 0)),
            pl.BlockSpec((1, d), lambda i, j: (0, 0)),
            pl.BlockSpec((1, 1, d), vec),
            pl.BlockSpec((1, 1, d), vec),
            pl.BlockSpec((d, tn), lambda i, j: (0, j)),
            pl.BlockSpec((d, LANE), lambda i, j: (0, 0)),
        ],
        out_specs=(pl.BlockSpec((tm, tn), lambda i, j: (i, j)),
                   pl.BlockSpec((tm, LANE), lambda i, j: (i, 0))),
        scratch_shapes=[pltpu.VMEM((tm, d), BF16)],
        compiler_params=_params(("parallel", "arbitrary"), 48),
        name="mod_matmul",
    )(x, gain.reshape(1, d), shift[:, None, :], scale[:, None, :], w, w_a)


def _mm_res_kernel(a_ref, w_ref, x_ref, gate_ref, o_ref):
    acc = jnp.dot(a_ref[...], w_ref[...], preferred_element_type=F32)
    o_ref[...] = x_ref[...] + gate_ref[0] * acc


def _matmul_residual(a, w, x, gate, seq, tm, tn):
    t, k = a.shape
    n = w.shape[1]
    per_b = seq // tm
    return pl.pallas_call(
        _mm_res_kernel,
        out_shape=jax.ShapeDtypeStruct((t, n), F32),
        grid=(t // tm, n // tn),
        in_specs=[
            pl.BlockSpec((tm, k), lambda i, j: (i, 0)),
            pl.BlockSpec((k, tn), lambda i, j: (0, j)),
            pl.BlockSpec((tm, tn), lambda i, j: (i, j)),
            pl.BlockSpec((1, 1, tn), lambda i, j: (i // per_b, 0, j)),
        ],
        out_specs=pl.BlockSpec((tm, tn), lambda i, j: (i, j)),
        compiler_params=_params(("parallel", "arbitrary"), 48),
        name="matmul_residual",
    )(a, w, x, gate[:, None, :])


def _ffn_kernel(x_ref, g_ref, sh_ref, sc_ref, gate_ref, wg_ref, wu_ref, wd_ref, o_ref, h_ref, acc_ref):
    f = pl.program_id(1)

    @pl.when(f == 0)
    def _():
        _modulate_into(h_ref, x_ref, g_ref, sh_ref, sc_ref)
        acc_ref[...] = jnp.zeros_like(acc_ref)

    h = h_ref[...]
    g = jnp.dot(h, wg_ref[...], preferred_element_type=F32)
    u = jnp.dot(h, wu_ref[...], preferred_element_type=F32)
    a = (_silu(g) * u).astype(BF16)
    acc_ref[...] += jnp.dot(a, wd_ref[...], preferred_element_type=F32)

    @pl.when(f == pl.num_programs(1) - 1)
    def _():
        o_ref[...] = x_ref[...] + gate_ref[0] * acc_ref[...]


def _ffn(x, gain, shift, scale, gate, w_gu, w_down, seq, tm, tf):
    t, d = x.shape
    dff = w_down.shape[0]
    nf = dff // tf
    per_b = seq // tm
    vec = lambda i, f: (i // per_b, 0, 0)
    return pl.pallas_call(
        _ffn_kernel,
        out_shape=jax.ShapeDtypeStruct((t, d), F32),
        grid=(t // tm, nf),
        in_specs=[
            pl.BlockSpec((tm, d), lambda i, f: (i, 0)),
            pl.BlockSpec((1, d), lambda i, f: (0, 0)),
            pl.BlockSpec((1, 1, d), vec),
            pl.BlockSpec((1, 1, d), vec),
            pl.BlockSpec((1, 1, d), vec),
            pl.BlockSpec((d, tf), lambda i, f: (0, f)),
            pl.BlockSpec((d, tf), lambda i, f: (0, f + nf)),
            pl.BlockSpec((tf, d), lambda i, f: (f, 0)),
        ],
        out_specs=pl.BlockSpec((tm, d), lambda i, f: (i, 0)),
        scratch_shapes=[pltpu.VMEM((tm, d), BF16), pltpu.VMEM((tm, d), F32)],
        compiler_params=_params(("parallel", "arbitrary"), 56),
        name="ffn",
    )(x, gain.reshape(1, d), shift[:, None, :], scale[:, None, :], gate[:, None, :], w_gu, w_gu, w_down)


GLA_PAIR = 2 * GLA_CHUNK


def _gla_kernel(q_ref, k_ref, v_ref, g_ref, a_ref, wal_ref, bal_ref, on_ref, o_ref, st_ref, *, dk, dv):
    ch = GLA_CHUNK
    tp = q_ref.shape[0]
    qk = q_ref.shape[1]

    @pl.when(pl.program_id(1) == 0)
    def _():
        st_ref[...] = jnp.zeros_like(st_ref)

    z = jnp.dot(a_ref[...], wal_ref[...], preferred_element_type=F32) + bal_ref[...]
    log_alpha = (jnp.minimum(z, 0.0) - jnp.log1p(jnp.exp(-jnp.abs(z)))) / GLA_TAU

    row = lax.broadcasted_iota(jnp.int32, (tp, tp), 0)
    col = lax.broadcasted_iota(jnp.int32, (tp, tp), 1)
    causal = jnp.logical_and(col <= row, col >= (row // ch) * ch)
    tri = jnp.where(causal, 1.0, 0.0).astype(BF16)
    first = lax.broadcasted_iota(jnp.int32, (tp, qk), 0) < ch
    first_h = lax.broadcasted_iota(jnp.int32, (tp, dk), 0) < ch

    la_hi = log_alpha.astype(BF16)
    la_lo = (log_alpha - la_hi.astype(F32)).astype(BF16)
    bcum = jnp.dot(tri, la_hi, preferred_element_type=F32) + jnp.dot(tri, la_lo, preferred_element_type=F32)
    b_last0 = bcum[ch - 1:ch]
    b_last1 = bcum[tp - 1:tp]
    e_q = jnp.exp(bcum)
    e_ki = jnp.exp(-bcum)
    e_ks = jnp.exp(jnp.where(first, b_last0, b_last1) - bcum)
    dec0 = jnp.exp(b_last0)
    dec1 = jnp.exp(b_last1)
    on = on_ref[...]
    tn_dims = (((0,), (0,)), ((), ()))
    nt_dims = (((1,), (1,)), ((), ()))

    def decay_cols(dec, ks):
        return jnp.tile(jnp.broadcast_to(dec[:, ks], (LANE, dk)).T, (1, dv // LANE))

    for h in range(GLA_HEADS):
        ks = slice(h * dk, (h + 1) * dk)
        vs = slice(h * dv, (h + 1) * dv)
        q = q_ref[:, ks].astype(F32) * (dk ** -0.5)
        k = k_ref[:, ks].astype(F32)
        v = v_ref[:, vs]
        q_dec = (q * e_q[:, ks]).astype(BF16)
        k_in = (k * e_ki[:, ks]).astype(BF16)
        k_st = k * e_ks[:, ks]
        att = lax.dot_general(q_dec, k_in, nt_dims, preferred_element_type=F32)
        att = jnp.where(causal, att, 0.0).astype(BF16)
        o_intra = jnp.dot(att, v, preferred_element_type=F32)
        k0 = jnp.where(first_h, k_st, 0.0).astype(BF16)
        k1 = jnp.where(first_h, 0.0, k_st).astype(BF16)
        s0 = st_ref[h]
        o0 = jnp.dot(q_dec[:ch], s0.astype(BF16), preferred_element_type=F32)
        s1 = decay_cols(dec0, ks) * s0 + lax.dot_general(k0, v, tn_dims, preferred_element_type=F32)
        o1 = jnp.dot(q_dec[ch:], s1.astype(BF16), preferred_element_type=F32)
        st_ref[h] = decay_cols(dec1, ks) * s1 + lax.dot_general(k1, v, tn_dims, preferred_element_type=F32)
        o = o_intra + jnp.concatenate([o0, o1], axis=0)
        gg = g_ref[:, vs].astype(F32)
        o_ref[:, vs] = ((_rms_scale(o) * on) * _silu(gg)).astype(o_ref.dtype)


def _gla(proj, a_lr, w_alpha_p, b_alpha, onorm, batch, seq):
    t = proj.shape[0]
    dk = w_alpha_p.shape[1] // GLA_HEADS
    dv = onorm.shape[0]
    qk = GLA_HEADS * dk
    vv = GLA_HEADS * dv
    tp = GLA_PAIR
    nc = seq // tp
    rowblk = lambda b, i: b * nc + i
    return pl.pallas_call(
        functools.partial(_gla_kernel, dk=dk, dv=dv),
        out_shape=jax.ShapeDtypeStruct((t, vv), BF16),
        grid=(batch, nc),
        in_specs=[
            pl.BlockSpec((tp, qk), lambda b, i: (rowblk(b, i), 0)),
            pl.BlockSpec((tp, qk), lambda b, i: (rowblk(b, i), 1)),
            pl.BlockSpec((tp, vv), lambda b, i: (rowblk(b, i), 2 * qk // vv)),
            pl.BlockSpec((tp, vv), lambda b, i: (rowblk(b, i), 2 * qk // vv + 1)),
            pl.BlockSpec((tp, LANE), lambda b, i: (rowblk(b, i), 0)),
            pl.BlockSpec((LANE, qk), lambda b, i: (0, 0)),
            pl.BlockSpec((1, qk), lambda b, i: (0, 0)),
            pl.BlockSpec((1, dv), lambda b, i: (0, 0)),
        ],
        out_specs=pl.BlockSpec((tp, vv), lambda b, i: (rowblk(b, i), 0)),
        scratch_shapes=[pltpu.VMEM((GLA_HEADS, dk, dv), F32)],
        compiler_params=_params(("parallel", "arbitrary"), 32),
        name="gla",
    )(proj, proj, proj, proj, a_lr, w_alpha_p, b_alpha.reshape(1, qk), onorm.reshape(1, dv))


def _rope_pair(x, x_swapped, cos4, sin4):
    return x * cos4 + x_swapped * sin4


def _head_norm_store(o_ref, h, nope, rope, gain_n, gain_r, dim, post_scale):
    ss = jnp.sum(nope * nope + rope * rope, axis=-1, keepdims=True)
    r = lax.rsqrt(ss / dim + EPS)
    if post_scale != 1.0:
        r = r * post_scale
    o_ref[0, h, :, 0:LANE] = ((nope * r) * gain_n).astype(o_ref.dtype)
    o_ref[0, h, :, LANE:2 * LANE] = ((rope * r) * gain_r).astype(o_ref.dtype)


def _mla_kv_kernel(x_ref, g_ref, sh_ref, sc_ref, wd_ref, ln_ref, wk_ref, wv_ref, kn_ref, kr_ref,
                   cos_ref, sin_ref, k_ref, v_ref):
    h_in = _modulate(x_ref[...], g_ref[...], sh_ref[0], sc_ref[0]).astype(BF16)
    ckv = jnp.dot(h_in, wd_ref[...], preferred_element_type=F32)
    lora = ln_ref.shape[1]
    c_lat = (_rms_scale(ckv[:, :lora]) * ln_ref[...]).astype(BF16)
    k_pe = _rope_pair(ckv[:, lora:lora + LANE], ckv[:, lora + LANE:lora + 2 * LANE], cos_ref[...], sin_ref[...])
    kn = jnp.dot(c_lat, wk_ref[...], preferred_element_type=F32)
    vv = jnp.dot(c_lat, wv_ref[...], preferred_element_type=F32)
    gn = kn_ref[...]
    gr = kr_ref[...]
    for h in range(MLA_HEADS):
        cols = slice(h * LANE, (h + 1) * LANE)
        _head_norm_store(k_ref, h, kn[:, cols], k_pe, gn, gr, MLA_NOPE + MLA_ROPE, 1.0)
        v_ref[0, h] = vv[:, cols].astype(v_ref.dtype)


def _mla_q_kernel(x_ref, g_ref, sh_ref, sc_ref, wd_ref, ln_ref, wn_ref, wr_ref, ws_ref, qn_ref, qr_ref,
                  cos_ref, sin_ref, q_ref):
    h_in = _modulate(x_ref[...], g_ref[...], sh_ref[0], sc_ref[0]).astype(BF16)
    cq = jnp.dot(h_in, wd_ref[...], preferred_element_type=F32)
    cq = (_rms_scale(cq) * ln_ref[...]).astype(BF16)
    qn = jnp.dot(cq, wn_ref[...], preferred_element_type=F32)
    qr = jnp.dot(cq, wr_ref[...], preferred_element_type=F32)
    qs = jnp.dot(cq, ws_ref[...], preferred_element_type=F32)
    cos4 = cos_ref[...]
    sin4 = sin_ref[...]
    gn = qn_ref[...]
    gr = qr_ref[...]
    sm_scale = (MLA_NOPE + MLA_ROPE) ** -0.5
    for h in range(MLA_HEADS):
        cols = slice(h * LANE, (h + 1) * LANE)
        rope = _rope_pair(qr[:, cols], qs[:, cols], cos4, sin4)
        _head_norm_store(q_ref, h, qn[:, cols], rope, gn, gr, MLA_NOPE + MLA_ROPE, sm_scale)


def _const_spec(arr):
    nd = arr.ndim
    return pl.BlockSpec(arr.shape, lambda b, i: (0,) * nd)


def _mla_kv(x, gain, shift, scale, w_dkv_p, lat_norm, w_k, w_v, kg_n, kg_r, cos4, sin4, batch, seq, tm):
    t, d = x.shape
    nt = seq // tm
    row = lambda b, i: (b * nt + i, 0)
    vec = lambda b, i: (b, 0, 0)
    g2 = gain.reshape(1, d)
    weights = [w_dkv_p, lat_norm.reshape(1, -1), w_k, w_v, kg_n, kg_r]
    hk = pl.BlockSpec((1, MLA_HEADS, tm, 2 * LANE), lambda b, i: (b, 0, i, 0))
    hv = pl.BlockSpec((1, MLA_HEADS, tm, LANE), lambda b, i: (b, 0, i, 0))
    return pl.pallas_call(
        _mla_kv_kernel,
        out_shape=(jax.ShapeDtypeStruct((batch, MLA_HEADS, seq, 2 * LANE), BF16),
                   jax.ShapeDtypeStruct((batch, MLA_HEADS, seq, LANE), BF16)),
        grid=(batch, nt),
        in_specs=[pl.BlockSpec((tm, d), row), _const_spec(g2),
                  pl.BlockSpec((1, 1, d), vec), pl.BlockSpec((1, 1, d), vec)]
                 + [_const_spec(w) for w in weights]
                 + [pl.BlockSpec((tm, LANE), row), pl.BlockSpec((tm, LANE), row)],
        out_specs=(hk, hv),
        compiler_params=_params(("parallel", "parallel"), 48),
        name="mla_kv",
    )(x, g2, shift[:, None, :], scale[:, None, :], *weights, cos4, sin4)


def _mla_q(x, gain, shift, scale, w_dq, q_lat_norm, w_qn, w_qr, w_qs, qg_n, qg_r, cos4, sin4, batch, seq, tm):
    t, d = x.shape
    nt = seq // tm
    row = lambda b, i: (b * nt + i, 0)
    vec = lambda b, i: (b, 0, 0)
    g2 = gain.reshape(1, d)
    weights = [w_dq, q_lat_norm.reshape(1, -1), w_qn, w_qr, w_qs, qg_n, qg_r]
    hq = pl.BlockSpec((1, MLA_HEADS, tm, 2 * LANE), lambda b, i: (b, 0, i, 0))
    return pl.pallas_call(
        _mla_q_kernel,
        out_shape=jax.ShapeDtypeStruct((batch, MLA_HEADS, seq, 2 * LANE), BF16),
        grid=(batch, nt),
        in_specs=[pl.BlockSpec((tm, d), row), _const_spec(g2),
                  pl.BlockSpec((1, 1, d), vec), pl.BlockSpec((1, 1, d), vec)]
                 + [_const_spec(w) for w in weights]
                 + [pl.BlockSpec((tm, LANE), row), pl.BlockSpec((tm, LANE), row)],
        out_specs=hq,
        compiler_params=_params(("parallel", "parallel"), 48),
        name="mla_q",
    )(x, g2, shift[:, None, :], scale[:, None, :], *weights, cos4, sin4)


def _attn_kernel(q_ref, k_ref, v_ref, o_ref, *, tq, tk):
    qi = pl.program_id(2)
    q = q_ref[0, 0]

    def step(kv_start, carry, masked):
        m, l, acc = carry
        k = k_ref[0, 0, pl.ds(kv_start, tk), :]
        v = v_ref[0, 0, pl.ds(kv_start, tk), :]
        s = lax.dot_general(q, k, (((1,), (1,)), ((), ())), preferred_element_type=F32)
        if masked:
            r = lax.broadcasted_iota(jnp.int32, (tq, tk), 0)
            c = lax.broadcasted_iota(jnp.int32, (tq, tk), 1)
            s = jnp.where(c <= r, s, -jnp.inf)
        m_new = jnp.maximum(m, jnp.max(s, axis=-1, keepdims=True))
        alpha = jnp.exp(m - m_new)
        p = jnp.exp(s - m_new)
        l_new = alpha * l + jnp.sum(p, axis=-1, keepdims=True)
        acc_new = alpha * acc + jnp.dot(p.astype(BF16), v, preferred_element_type=F32)
        return m_new, l_new, acc_new

    init = (jnp.full((tq, 1), -jnp.inf, F32), jnp.zeros((tq, 1), F32), jnp.zeros((tq, v_ref.shape[-1]), F32))
    n_full = qi * (tq // tk)
    carry = lax.fori_loop(0, n_full, lambda j, c: step(pl.multiple_of(j * tk, tk), c, False), init)
    m, l, acc = step(pl.multiple_of(qi * tq, tq), carry, True)
    o_ref[0] = (acc / l).astype(o_ref.dtype)


def _attention(q, k, v, tq):
    b, h, s, dq = q.shape
    dvv = v.shape[-1]
    return pl.pallas_call(
        functools.partial(_attn_kernel, tq=tq, tk=tq),
        out_shape=jax.ShapeDtypeStruct((b, s, h * dvv), BF16),
        grid=(b, h, s // tq),
        in_specs=[
            pl.BlockSpec((1, 1, tq, dq), lambda bi, hi, qi: (bi, hi, qi, 0)),
            pl.BlockSpec((1, 1, s, dq), lambda bi, hi, qi: (bi, hi, 0, 0)),
            pl.BlockSpec((1, 1, s, dvv), lambda bi, hi, qi: (bi, hi, 0, 0)),
        ],
        out_specs=pl.BlockSpec((1, tq, dvv), lambda bi, hi, qi: (bi, qi, hi)),
        compiler_params=_params(("parallel", "parallel", "arbitrary"), 32),
        name="mla_attention",
    )(q, k, v)


def _rope_tables(positions):
    b, s = positions.shape
    half = MLA_ROPE // 2
    inv_freq = ROPE_THETA ** (-jnp.arange(half, dtype=F32) / half)
    ang = positions.astype(F32)[..., None] * inv_freq
    cos, sin = _trig(ang.reshape(-1, LANE))
    cos = cos.reshape(b * s, half)
    sin = sin.reshape(b * s, half)
    zero = jnp.zeros((b * s, LANE - 2 * half), F32)
    cos4 = jnp.concatenate([cos, cos, zero], axis=-1)
    sin4 = jnp.concatenate([-sin, sin, zero], axis=-1)
    return cos4, sin4


def _split_rope_cols(w_rope):
    half = MLA_ROPE // 2
    x1, x2 = w_rope[..., :half], w_rope[..., half:]
    zero = jnp.zeros(w_rope.shape[:-1] + (LANE - MLA_ROPE,), w_rope.dtype)
    return jnp.concatenate([x1, x2, zero], axis=-1), jnp.concatenate([x2, x1, zero], axis=-1)


def _pad_rope_gain(gain):
    g_n = gain[:MLA_NOPE].reshape(1, MLA_NOPE)
    g_r = jnp.concatenate([gain[MLA_NOPE:], jnp.zeros((LANE - MLA_ROPE,), gain.dtype)]).reshape(1, LANE)
    return g_n, g_r


def kernel(x, c, positions, ada_w, ada_b, norm_mix, norm_ffn, gla_w_in, gla_w_alpha, gla_b_alpha, gla_onorm,
           gla_w_out, mla_w_dq, mla_q_lat_norm, mla_w_uq, mla_q_norm, mla_w_out, kv_ada_w, kv_ada_b, kv_norm,
           kv_w_dkv, kv_lat_norm, kv_w_ukv, kv_k_norm, ffn_w_gu, ffn_w_down):
    batch, seq, d = x.shape
    t = batch * seq
    depth = ada_w.shape[0]
    n_gla = gla_w_in.shape[0]
    xf = x.reshape(t, d)

    tm_big = min(seq, 1024)
    tm_mid = min(seq, 512)
    tm_small = min(seq, 256)

    c_rep = jnp.broadcast_to(c[:, :, None], (batch, d, LANE))
    cos4, sin4 = _rope_tables(positions)

    k_sh = v_sh = None
    for layer in range(depth):
        mod = _ada_mod(c_rep, ada_w, ada_b, layer)
        shift_m, scale_m, gate_m, shift_f, scale_f, gate_f = jnp.split(mod, 6, axis=-1)
        if layer < n_gla:
            w_in = gla_w_in[layer]
            qkvg = w_in.shape[1] - GLA_GATE_RANK
            w_main = w_in[:, :qkvg].astype(BF16)
            w_a = jnp.pad(w_in[:, qkvg:], ((0, 0), (0, LANE - GLA_GATE_RANK))).astype(BF16)
            proj, a_lr = _mod_matmul(xf, norm_mix[layer], shift_m, scale_m, w_main, w_a, seq, tm_big, 1024)
            w_alpha_p = jnp.pad(gla_w_alpha[layer], ((0, LANE - GLA_GATE_RANK), (0, 0))).astype(BF16)
            mix_in = _gla(proj, a_lr, w_alpha_p, gla_b_alpha[layer], gla_onorm[layer], batch, seq)
            w_out = gla_w_out[layer].astype(BF16)
        else:
            j = layer - n_gla
            if j == 0:
                kv_mod = _ada_mod(c_rep, kv_ada_w[None], kv_ada_b[None], 0)
                kv_shift, kv_scale = jnp.split(kv_mod, 2, axis=-1)
                pe_p, pe_s = _split_rope_cols(kv_w_dkv[:, KV_LORA:])
                w_dkv_p = jnp.concatenate([kv_w_dkv[:, :KV_LORA], pe_p, pe_s], axis=1).astype(BF16)
                w_ukv = kv_w_ukv.reshape(KV_LORA, MLA_HEADS, MLA_NOPE + MLA_V)
                w_k = w_ukv[:, :, :MLA_NOPE].reshape(KV_LORA, MLA_HEADS * MLA_NOPE).astype(BF16)
                w_v = w_ukv[:, :, MLA_NOPE:].reshape(KV_LORA, MLA_HEADS * MLA_V).astype(BF16)
                kg_n, kg_r = _pad_rope_gain(kv_k_norm)
                k_sh, v_sh = _mla_kv(xf, kv_norm, kv_shift, kv_scale, w_dkv_p, kv_lat_norm, w_k, w_v,
                                     kg_n, kg_r, cos4, sin4, batch, seq, tm_small)
            q_lora = mla_w_dq.shape[2]
            w_uq = mla_w_uq[j].reshape(q_lora, MLA_HEADS, MLA_NOPE + MLA_ROPE)
            w_qn = w_uq[:, :, :MLA_NOPE].reshape(q_lora, MLA_HEADS * MLA_NOPE).astype(BF16)
            r_p, r_s = _split_rope_cols(w_uq[:, :, MLA_NOPE:])
            w_qr = r_p.reshape(q_lora, MLA_HEADS * LANE).astype(BF16)
            w_qs = r_s.reshape(q_lora, MLA_HEADS * LANE).astype(BF16)
            qg_n, qg_r = _pad_rope_gain(mla_q_norm[j])
            q = _mla_q(xf, norm_mix[layer], shift_m, scale_m, mla_w_dq[j].astype(BF16), mla_q_lat_norm[j],
                       w_qn, w_qr, w_qs, qg_n, qg_r, cos4, sin4, batch, seq, tm_small)
            mix_in = _attention(q, k_sh, v_sh, tm_mid).reshape(t, MLA_HEADS * MLA_V)
            w_out = mla_w_out[j].astype(BF16)
        xf = _matmul_residual(mix_in, w_out, xf, gate_m, seq, tm_big, 1024)
        xf = _ffn(xf, norm_ffn[layer], shift_f, scale_f, gate_f,
                  ffn_w_gu[layer].astype(BF16), ffn_w_down[layer].astype(BF16), seq, tm_mid, 512)
    return xf.reshape(batch, seq, d)
```

```python
import functools

import jax
import jax.numpy as jnp
from jax import lax
from jax.experimental import pallas as pl
from jax.experimental.pallas import tpu as pltpu

GLA_HEADS = 4
GLA_GATE_RANK = 16
GLA_TAU = 16.0
GLA_CHUNK = 64
MLA_HEADS = 16
MLA_NOPE = 128
MLA_ROPE = 64
MLA_V = 128
KV_LORA = 512
ROPE_THETA = 10000.0
EPS = 1e-6
LOG2_E = 1.4426950408889634

LANE = 128
MIB = 1 << 20

F32 = jnp.float32
BF16 = jnp.bfloat16


def _params(semantics, vmem_mib):
    return pltpu.CompilerParams(dimension_semantics=semantics, vmem_limit_bytes=vmem_mib * MIB)


def _silu(x):
    return x * jax.nn.sigmoid(x)


def _rms_scale(x):
    return x * lax.rsqrt(jnp.mean(x * x, axis=-1, keepdims=True) + EPS)


def _ada_kernel(c_ref, w_ref, b_ref, o_ref):
    nb, tn = o_ref.shape
    for b in range(nb):
        sb = _silu(c_ref[b])
        for j in range(tn // LANE):
            cols = slice(j * LANE, (j + 1) * LANE)
            acc = jnp.sum(w_ref[:, cols] * sb, axis=0, keepdims=True)
            o_ref[b:b + 1, cols] = acc + b_ref[:, cols]


def _ada_mod(c_rep, w, b, layer, tn=1024):
    nb, d, _ = c_rep.shape
    n = w.shape[2]
    return pl.pallas_call(
        _ada_kernel,
        out_shape=jax.ShapeDtypeStruct((nb, n), F32),
        grid=(n // tn,),
        in_specs=[
            pl.BlockSpec((nb, d, LANE), lambda j: (0, 0, 0)),
            pl.BlockSpec((None, d, tn), lambda j: (layer, 0, j)),
            pl.BlockSpec((None, 1, tn), lambda j: (layer, 0, j)),
        ],
        out_specs=pl.BlockSpec((nb, tn), lambda j: (0, j)),
        compiler_params=_params(("parallel",), 40),
        name="ada_mod",
    )(c_rep, w, b[:, None, :])


def _trig_kernel(a_ref, c_ref, s_ref):
    a = a_ref[...]
    c_ref[...] = jnp.cos(a)
    s_ref[...] = jnp.sin(a)


def _trig(ang):
    rows = ang.shape[0]
    tr = min(rows, 512)
    spec = pl.BlockSpec((tr, LANE), lambda i: (i, 0))
    return pl.pallas_call(
        _trig_kernel,
        out_shape=(jax.ShapeDtypeStruct(ang.shape, F32),) * 2,
        grid=(rows // tr,),
        in_specs=[spec],
        out_specs=(spec, spec),
        compiler_params=_params(("parallel",), 16),
        name="rope_trig",
    )(ang)


def _modulate(x, gain, shift, scale):
    return (_rms_scale(x) * gain) * (1.0 + scale) + shift


MOD_ROWS = 256


def _modulate_into(h_ref, x_ref, g_ref, sh_ref, sc_ref):
    tm = x_ref.shape[0]
    step = min(tm, MOD_ROWS)
    gain, shift, scale = g_ref[...], sh_ref[0], sc_ref[0]
    for r in range(tm // step):
        rows = slice(r * step, (r + 1) * step)
        h_ref[rows, :] = _modulate(x_ref[rows, :], gain, shift, scale).astype(h_ref.dtype)


def _modmm_kernel(x_ref, g_ref, sh_ref, sc_ref, w_ref, wa_ref, o_ref, a_ref, h_ref):
    @pl.when(pl.program_id(1) == 0)
    def _():
        _modulate_into(h_ref, x_ref, g_ref, sh_ref, sc_ref)
        a_ref[...] = jnp.dot(h_ref[...], wa_ref[...], preferred_element_type=F32).astype(a_ref.dtype)

    o_ref[...] = jnp.dot(h_ref[...], w_ref[...].astype(BF16), preferred_element_type=F32).astype(o_ref.dtype)


def _mod_matmul(x, gain, shift, scale, w, layer, n, w_a, seq, tm, tn):
    t, d = x.shape
    per_b = seq // tm
    vec = lambda i, j: (i // per_b, 0, 0)
    return pl.pallas_call(
        _modmm_kernel,
        out_shape=(jax.ShapeDtypeStruct((t, n), BF16), jax.ShapeDtypeStruct((t, LANE), BF16)),
        grid=(t // tm, n // tn),
        in_specs=[
            pl.BlockSpec((tm, d), lambda i, j: (i, 0)),
            pl.BlockSpec((1, d), lambda i, j: (0, 0)),
            pl.BlockSpec((1, 1, d), vec),
            pl.BlockSpec((1, 1, d), vec),
            pl.BlockSpec((None, d, tn), lambda i, j: (layer, 0, j)),
            pl.BlockSpec((d, LANE), lambda i, j: (0, 0)),
        ],
        out_specs=(pl.BlockSpec((tm, tn), lambda i, j: (i, j)),
                   pl.BlockSpec((tm, LANE), lambda i, j: (i, 0))),
        scratch_shapes=[pltpu.VMEM((tm, d), BF16)],
        compiler_params=_params(("parallel", "arbitrary"), 56),
        name="mod_matmul",
    )(x, gain.reshape(1, d), shift[:, None, :], scale[:, None, :], w, w_a)


def _mm_res_kernel(a_ref, w_ref, x_ref, gate_ref, o_ref):
    acc = jnp.dot(a_ref[...], w_ref[...], preferred_element_type=F32)
    o_ref[...] = x_ref[...] + gate_ref[0] * acc


def _matmul_residual(a, w, x, gate, seq, tm, tn):
    t, k = a.shape
    n = w.shape[1]
    per_b = seq // tm
    return pl.pallas_call(
        _mm_res_kernel,
        out_shape=jax.ShapeDtypeStruct((t, n), F32),
        grid=(t // tm, n // tn),
        in_specs=[
            pl.BlockSpec((tm, k), lambda i, j: (i, 0)),
            pl.BlockSpec((k, tn), lambda i, j: (0, j)),
            pl.BlockSpec((tm, tn), lambda i, j: (i, j)),
            pl.BlockSpec((1, 1, tn), lambda i, j: (i // per_b, 0, j)),
        ],
        out_specs=pl.BlockSpec((tm, tn), lambda i, j: (i, j)),
        compiler_params=_params(("parallel", "arbitrary"), 48),
        name="matmul_residual",
    )(a, w, x, gate[:, None, :])


def _ffn_kernel(x_ref, g_ref, sh_ref, sc_ref, gate_ref, wg_ref, wu_ref, wd_ref, o_ref, h_ref, acc_ref):
    f = pl.program_id(1)

    @pl.when(f == 0)
    def _():
        _modulate_into(h_ref, x_ref, g_ref, sh_ref, sc_ref)
        acc_ref[...] = jnp.zeros_like(acc_ref)

    h = h_ref[...]
    g = jnp.dot(h, wg_ref[...], preferred_element_type=F32)
    u = jnp.dot(h, wu_ref[...], preferred_element_type=F32)
    a = (_silu(g) * u).astype(BF16)
    acc_ref[...] += jnp.dot(a, wd_ref[...], preferred_element_type=F32)

    @pl.when(f == pl.num_programs(1) - 1)
    def _():
        o_ref[...] = x_ref[...] + gate_ref[0] * acc_ref[...]


def _ffn(x, gain, shift, scale, gate, w_gu, w_down, layer, seq, tm, tf):
    t, d = x.shape
    dff = w_down.shape[1]
    nf = dff // tf
    per_b = seq // tm
    vec = lambda i, f: (i // per_b, 0, 0)
    return pl.pallas_call(
        _ffn_kernel,
        out_shape=jax.ShapeDtypeStruct((t, d), F32),
        grid=(t // tm, nf),
        in_specs=[
            pl.BlockSpec((tm, d), lambda i, f: (i, 0)),
            pl.BlockSpec((1, d), lambda i, f: (0, 0)),
            pl.BlockSpec((1, 1, d), vec),
            pl.BlockSpec((1, 1, d), vec),
            pl.BlockSpec((1, 1, d), vec),
            pl.BlockSpec((None, d, tf), lambda i, f: (layer, 0, f)),
            pl.BlockSpec((None, d, tf), lambda i, f: (layer, 0, f + nf)),
            pl.BlockSpec((None, tf, d), lambda i, f: (layer, f, 0)),
        ],
        out_specs=pl.BlockSpec((tm, d), lambda i, f: (i, 0)),
        scratch_shapes=[pltpu.VMEM((tm, d), BF16), pltpu.VMEM((tm, d), F32)],
        compiler_params=_params(("parallel", "arbitrary"), 56),
        name="ffn",
    )(x, gain.reshape(1, d), shift[:, None, :], scale[:, None, :], gate[:, None, :], w_gu, w_gu, w_down)


GLA_PAIR = 2 * GLA_CHUNK


def _gla_kernel(q_ref, k_ref, v_ref, g_ref, a_ref, wal_ref, bal_ref, on_ref, o_ref, st_ref, *, dk, dv):
    ch = GLA_CHUNK
    tp = q_ref.shape[0]
    qk = q_ref.shape[1]

    @pl.when(pl.program_id(1) == 0)
    def _():
        st_ref[...] = jnp.zeros_like(st_ref)

    z = jnp.dot(a_ref[...], wal_ref[...], preferred_element_type=F32) + bal_ref[...]
    log_alpha = (jnp.minimum(z, 0.0) - jnp.log1p(jnp.exp(-jnp.abs(z)))) / GLA_TAU

    row = lax.broadcasted_iota(jnp.int32, (tp, tp), 0)
    col = lax.broadcasted_iota(jnp.int32, (tp, tp), 1)
    causal = jnp.logical_and(col <= row, col >= (row // ch) * ch)
    tri = jnp.where(causal, 1.0, 0.0).astype(BF16)
    first = lax.broadcasted_iota(jnp.int32, (tp, qk), 0) < ch
    first_h = lax.broadcasted_iota(jnp.int32, (tp, dk), 0) < ch

    la_hi = log_alpha.astype(BF16)
    la_lo = (log_alpha - la_hi.astype(F32)).astype(BF16)
    bcum = jnp.dot(tri, la_hi, preferred_element_type=F32) + jnp.dot(tri, la_lo, preferred_element_type=F32)
    b_last0 = bcum[ch - 1:ch]
    b_last1 = bcum[tp - 1:tp]
    e_q = jnp.exp(bcum)
    e_ki = jnp.exp(-bcum)
    e_ks = jnp.exp(jnp.where(first, b_last0, b_last1) - bcum)
    dec0 = jnp.exp(b_last0)
    dec1 = jnp.exp(b_last1)
    on = on_ref[...]
    tn_dims = (((0,), (0,)), ((), ()))
    nt_dims = (((1,), (1,)), ((), ()))

    def decay_cols(dec, ks):
        return jnp.tile(jnp.broadcast_to(dec[:, ks], (LANE, dk)).T, (1, dv // LANE))

    for h in range(GLA_HEADS):
        ks = slice(h * dk, (h + 1) * dk)
        vs = slice(h * dv, (h + 1) * dv)
        q = q_ref[:, ks].astype(F32) * (dk ** -0.5)
        k = k_ref[:, ks].astype(F32)
        v = v_ref[:, vs]
        q_dec = (q * e_q[:, ks]).astype(BF16)
        k_in = (k * e_ki[:, ks]).astype(BF16)
        k_st = k * e_ks[:, ks]
        att = lax.dot_general(q_dec, k_in, nt_dims, preferred_element_type=F32)
        att = jnp.where(causal, att, 0.0).astype(BF16)
        o_intra = jnp.dot(att, v, preferred_element_type=F32)
        k0 = jnp.where(first_h, k_st, 0.0).astype(BF16)
        k1 = jnp.where(first_h, 0.0, k_st).astype(BF16)
        s0 = st_ref[h]
        o0 = jnp.dot(q_dec[:ch], s0.astype(BF16), preferred_element_type=F32)
        s1 = decay_cols(dec0, ks) * s0 + lax.dot_general(k0, v, tn_dims, preferred_element_type=F32)
        o1 = jnp.dot(q_dec[ch:], s1.astype(BF16), preferred_element_type=F32)
        st_ref[h] = decay_cols(dec1, ks) * s1 + lax.dot_general(k1, v, tn_dims, preferred_element_type=F32)
        o = o_intra + jnp.concatenate([o0, o1], axis=0)
        gg = g_ref[:, vs].astype(F32)
        o_ref[:, vs] = ((_rms_scale(o) * on) * _silu(gg)).astype(o_ref.dtype)


def _gla(proj, a_lr, w_alpha_p, b_alpha, onorm, batch, seq):
    t = proj.shape[0]
    dk = w_alpha_p.shape[1] // GLA_HEADS
    dv = onorm.shape[0]
    qk = GLA_HEADS * dk
    vv = GLA_HEADS * dv
    tp = GLA_PAIR
    nc = seq // tp
    rowblk = lambda b, i: b * nc + i
    return pl.pallas_call(
        functools.partial(_gla_kernel, dk=dk, dv=dv),
        out_shape=jax.ShapeDtypeStruct((t, vv), BF16),
        grid=(batch, nc),
        in_specs=[
            pl.BlockSpec((tp, qk), lambda b, i: (rowblk(b, i), 0)),
            pl.BlockSpec((tp, qk), lambda b, i: (rowblk(b, i), 1)),
            pl.BlockSpec((tp, vv), lambda b, i: (rowblk(b, i), 2 * qk // vv)),
            pl.BlockSpec((tp, vv), lambda b, i: (rowblk(b, i), 2 * qk // vv + 1)),
            pl.BlockSpec((tp, LANE), lambda b, i: (rowblk(b, i), 0)),
            pl.BlockSpec((LANE, qk), lambda b, i: (0, 0)),
            pl.BlockSpec((1, qk), lambda b, i: (0, 0)),
            pl.BlockSpec((1, dv), lambda b, i: (0, 0)),
        ],
        out_specs=pl.BlockSpec((tp, vv), lambda b, i: (rowblk(b, i), 0)),
        scratch_shapes=[pltpu.VMEM((GLA_HEADS, dk, dv), F32)],
        compiler_params=_params(("parallel", "arbitrary"), 32),
        name="gla",
    )(proj, proj, proj, proj, a_lr, w_alpha_p, b_alpha.reshape(1, qk), onorm.reshape(1, dv))


def _rope_pair(x, x_swapped, cos4, sin4):
    return x * cos4 + x_swapped * sin4


def _head_norm_store(o_ref, h, nope, rope, gain_n, gain_r, dim, post_scale):
    ss = jnp.sum(nope * nope + rope * rope, axis=-1, keepdims=True)
    r = lax.rsqrt(ss / dim + EPS)
    if post_scale != 1.0:
        r = r * post_scale
    o_ref[0, h, :, 0:LANE] = ((nope * r) * gain_n).astype(o_ref.dtype)
    o_ref[0, h, :, LANE:2 * LANE] = ((rope * r) * gain_r).astype(o_ref.dtype)


def _mla_kv_kernel(x_ref, g_ref, sh_ref, sc_ref, wd_ref, ln_ref, wk_ref, wv_ref, kn_ref, kr_ref,
                   cos_ref, sin_ref, k_ref, v_ref):
    h_in = _modulate(x_ref[...], g_ref[...], sh_ref[0], sc_ref[0]).astype(BF16)
    ckv = jnp.dot(h_in, wd_ref[...], preferred_element_type=F32)
    lora = ln_ref.shape[1]
    c_lat = (_rms_scale(ckv[:, :lora]) * ln_ref[...]).astype(BF16)
    k_pe = _rope_pair(ckv[:, lora:lora + LANE], ckv[:, lora + LANE:lora + 2 * LANE], cos_ref[...], sin_ref[...])
    kn = jnp.dot(c_lat, wk_ref[...], preferred_element_type=F32)
    vv = jnp.dot(c_lat, wv_ref[...], preferred_element_type=F32)
    gn = kn_ref[...]
    gr = kr_ref[...]
    for h in range(MLA_HEADS):
        cols = slice(h * LANE, (h + 1) * LANE)
        _head_norm_store(k_ref, h, kn[:, cols], k_pe, gn, gr, MLA_NOPE + MLA_ROPE, 1.0)
        v_ref[0, h, 0] = vv[:, cols].T.astype(v_ref.dtype)


def _mla_q_kernel(x_ref, g_ref, sh_ref, sc_ref, wd_ref, ln_ref, wn_ref, wr_ref, ws_ref, qn_ref, qr_ref,
                  cos_ref, sin_ref, q_ref):
    h_in = _modulate(x_ref[...], g_ref[...], sh_ref[0], sc_ref[0]).astype(BF16)
    cq = jnp.dot(h_in, wd_ref[...], preferred_element_type=F32)
    cq = (_rms_scale(cq) * ln_ref[...]).astype(BF16)
    qn = jnp.dot(cq, wn_ref[...], preferred_element_type=F32)
    qr = jnp.dot(cq, wr_ref[...], preferred_element_type=F32)
    qs = jnp.dot(cq, ws_ref[...], preferred_element_type=F32)
    cos4 = cos_ref[...]
    sin4 = sin_ref[...]
    gn = qn_ref[...]
    gr = qr_ref[...]
    sm_scale = (MLA_NOPE + MLA_ROPE) ** -0.5 * LOG2_E
    for h in range(MLA_HEADS):
        cols = slice(h * LANE, (h + 1) * LANE)
        rope = _rope_pair(qr[:, cols], qs[:, cols], cos4, sin4)
        _head_norm_store(q_ref, h, qn[:, cols], rope, gn, gr, MLA_NOPE + MLA_ROPE, sm_scale)


def _const_spec(arr):
    nd = arr.ndim
    return pl.BlockSpec(arr.shape, lambda b, i: (0,) * nd)


def _mla_kv(x, gain, shift, scale, w_dkv_p, lat_norm, w_k, w_v, kg_n, kg_r, cos4, sin4, batch, seq, tm):
    t, d = x.shape
    nt = seq // tm
    row = lambda b, i: (b * nt + i, 0)
    vec = lambda b, i: (b, 0, 0)
    g2 = gain.reshape(1, d)
    weights = [w_dkv_p, lat_norm.reshape(1, -1), w_k, w_v, kg_n, kg_r]
    hk = pl.BlockSpec((1, MLA_HEADS, tm, 2 * LANE), lambda b, i: (b, 0, i, 0))
    hv = pl.BlockSpec((1, MLA_HEADS, 1, MLA_V, tm), lambda b, i: (b, 0, i, 0, 0))
    return pl.pallas_call(
        _mla_kv_kernel,
        out_shape=(jax.ShapeDtypeStruct((batch, MLA_HEADS, seq, 2 * LANE), BF16),
                   jax.ShapeDtypeStruct((batch, MLA_HEADS, nt, MLA_V, tm), BF16)),
        grid=(batch, nt),
        in_specs=[pl.BlockSpec((tm, d), row), _const_spec(g2),
                  pl.BlockSpec((1, 1, d), vec), pl.BlockSpec((1, 1, d), vec)]
                 + [_const_spec(w) for w in weights]
                 + [pl.BlockSpec((tm, LANE), row), pl.BlockSpec((tm, LANE), row)],
        out_specs=(hk, hv),
        compiler_params=_params(("parallel", "parallel"), 48),
        name="mla_kv",
    )(x, g2, shift[:, None, :], scale[:, None, :], *weights, cos4, sin4)


def _mla_q(x, gain, shift, scale, w_dq, q_lat_norm, w_qn, w_qr, w_qs, qg_n, qg_r, cos4, sin4, batch, seq, tm):
    t, d = x.shape
    nt = seq // tm
    row = lambda b, i: (b * nt + i, 0)
    vec = lambda b, i: (b, 0, 0)
    g2 = gain.reshape(1, d)
    weights = [w_dq, q_lat_norm.reshape(1, -1), w_qn, w_qr, w_qs, qg_n, qg_r]
    hq = pl.BlockSpec((1, MLA_HEADS, tm, 2 * LANE), lambda b, i: (b, 0, i, 0))
    return pl.pallas_call(
        _mla_q_kernel,
        out_shape=jax.ShapeDtypeStruct((batch, MLA_HEADS, seq, 2 * LANE), BF16),
        grid=(batch, nt),
        in_specs=[pl.BlockSpec((tm, d), row), _const_spec(g2),
                  pl.BlockSpec((1, 1, d), vec), pl.BlockSpec((1, 1, d), vec)]
                 + [_const_spec(w) for w in weights]
                 + [pl.BlockSpec((tm, LANE), row), pl.BlockSpec((tm, LANE), row)],
        out_specs=hq,
        compiler_params=_params(("parallel", "parallel"), 48),
        name="mla_q",
    )(x, g2, shift[:, None, :], scale[:, None, :], *weights, cos4, sin4)


ATTN_HEADS_PER_STEP = 4


def _attn_kernel(q_ref, k_ref, vt_ref, o_ref, *, tq, tk):
    qi = pl.program_id(2)
    nh = q_ref.shape[1]
    tc = vt_ref.shape[-1]
    dv = vt_ref.shape[-2]
    nt_dims = (((1,), (1,)), ((), ()))

    def step(j, carry, masked):
        scores = []
        for h in range(nh):
            k = k_ref[0, h, pl.ds(pl.multiple_of(j * tk, tk), tk), :]
            scores.append(lax.dot_general(k, q_ref[0, h], nt_dims, preferred_element_type=F32))
        out = []
        for h in range(nh):
            m, l, acc = carry[h]
            st = scores[h]
            if masked:
                kv_i = lax.broadcasted_iota(jnp.int32, (tk, tq), 0)
                q_i = lax.broadcasted_iota(jnp.int32, (tk, tq), 1)
                st = jnp.where(kv_i <= q_i, st, -jnp.inf)
            m_new = jnp.maximum(m, jnp.max(st, axis=0, keepdims=True))
            alpha = jnp.exp2(m - m_new)
            pt = jnp.exp2(st - m_new)
            l_new = alpha * l + jnp.sum(pt, axis=0, keepdims=True)
            pt = pt.astype(BF16)
            pv = jnp.dot(vt_ref[0, h, j * (tk // tc)], pt[0:tc], preferred_element_type=F32)
            for c in range(1, tk // tc):
                pv += jnp.dot(vt_ref[0, h, j * (tk // tc) + c], pt[c * tc:(c + 1) * tc],
                              preferred_element_type=F32)
            out.append((m_new, l_new, alpha * acc + pv))
        return tuple(out)

    init = tuple((jnp.full((1, tq), -jnp.inf, F32), jnp.zeros((1, tq), F32), jnp.zeros((dv, tq), F32))
                 for _ in range(nh))
    carry = lax.fori_loop(0, qi * (tq // tk), lambda j, c: step(j, c, False), init)
    carry = step(qi, carry, True)
    for h in range(nh):
        m, l, acc = carry[h]
        o_ref[0, :, h * dv:(h + 1) * dv] = (acc / l).T.astype(o_ref.dtype)


def _attention(q, k, vt, tq):
    b, h, s, dq = q.shape
    _, _, nslab, dvv, tc = vt.shape
    nh = ATTN_HEADS_PER_STEP
    return pl.pallas_call(
        functools.partial(_attn_kernel, tq=tq, tk=tq),
        out_shape=jax.ShapeDtypeStruct((b, s, h * dvv), BF16),
        grid=(b, h // nh, s // tq),
        in_specs=[
            pl.BlockSpec((1, nh, tq, dq), lambda bi, hi, qi: (bi, hi, qi, 0)),
            pl.BlockSpec((1, nh, s, dq), lambda bi, hi, qi: (bi, hi, 0, 0)),
            pl.BlockSpec((1, nh, nslab, dvv, tc), lambda bi, hi, qi: (bi, hi, 0, 0, 0)),
        ],
        out_specs=pl.BlockSpec((1, tq, nh * dvv), lambda bi, hi, qi: (bi, qi, hi)),
        compiler_params=_params(("parallel", "parallel", "arbitrary"), 40),
        name="mla_attention",
    )(q, k, vt)


def _rope_tables(positions):
    b, s = positions.shape
    half = MLA_ROPE // 2
    inv_freq = ROPE_THETA ** (-jnp.arange(half, dtype=F32) / half)
    ang = positions.astype(F32)[..., None] * inv_freq
    cos, sin = _trig(ang.reshape(-1, LANE))
    cos = cos.reshape(b * s, half)
    sin = sin.reshape(b * s, half)
    zero = jnp.zeros((b * s, LANE - 2 * half), F32)
    cos4 = jnp.concatenate([cos, cos, zero], axis=-1)
    sin4 = jnp.concatenate([-sin, sin, zero], axis=-1)
    return cos4, sin4


def _split_rope_cols(w_rope):
    half = MLA_ROPE // 2
    x1, x2 = w_rope[..., :half], w_rope[..., half:]
    zero = jnp.zeros(w_rope.shape[:-1] + (LANE - MLA_ROPE,), w_rope.dtype)
    return jnp.concatenate([x1, x2, zero], axis=-1), jnp.concatenate([x2, x1, zero], axis=-1)


def _pad_rope_gain(gain):
    g_n = gain[:MLA_NOPE].reshape(1, MLA_NOPE)
    g_r = jnp.concatenate([gain[MLA_NOPE:], jnp.zeros((LANE - MLA_ROPE,), gain.dtype)]).reshape(1, LANE)
    return g_n, g_r


def kernel(x, c, positions, ada_w, ada_b, norm_mix, norm_ffn, gla_w_in, gla_w_alpha, gla_b_alpha, gla_onorm,
           gla_w_out, mla_w_dq, mla_q_lat_norm, mla_w_uq, mla_q_norm, mla_w_out, kv_ada_w, kv_ada_b, kv_norm,
           kv_w_dkv, kv_lat_norm, kv_w_ukv, kv_k_norm, ffn_w_gu, ffn_w_down):
    batch, seq, d = x.shape
    t = batch * seq
    depth = ada_w.shape[0]
    n_gla = gla_w_in.shape[0]
    xf = x.reshape(t, d)

    tm_big = min(seq, 1024)
    tm_mid = min(seq, 512)
    tm_small = min(seq, 256)

    c_rep = jnp.broadcast_to(c[:, :, None], (batch, d, LANE))
    cos4, sin4 = _rope_tables(positions)
    w_gu_bf = ffn_w_gu.astype(BF16)
    w_down_bf = ffn_w_down.astype(BF16)

    k_sh = v_sh = None
    for layer in range(depth):
        mod = _ada_mod(c_rep, ada_w, ada_b, layer)
        shift_m, scale_m, gate_m, shift_f, scale_f, gate_f = jnp.split(mod, 6, axis=-1)
        if layer < n_gla:
            qkvg = gla_w_in.shape[2] - GLA_GATE_RANK
            w_a = jnp.pad(gla_w_in[layer, :, qkvg:], ((0, 0), (0, LANE - GLA_GATE_RANK))).astype(BF16)
            proj, a_lr = _mod_matmul(xf, norm_mix[layer], shift_m, scale_m, gla_w_in, layer, qkvg, w_a,
                                     seq, tm_big, 1024)
            w_alpha_p = jnp.pad(gla_w_alpha[layer], ((0, LANE - GLA_GATE_RANK), (0, 0))).astype(BF16)
            mix_in = _gla(proj, a_lr, w_alpha_p, gla_b_alpha[layer], gla_onorm[layer], batch, seq)
            w_out = gla_w_out[layer].astype(BF16)
        else:
            j = layer - n_gla
            if j == 0:
                kv_mod = _ada_mod(c_rep, kv_ada_w[None], kv_ada_b[None], 0)
                kv_shift, kv_scale = jnp.split(kv_mod, 2, axis=-1)
                pe_p, pe_s = _split_rope_cols(kv_w_dkv[:, KV_LORA:])
                w_dkv_p = jnp.concatenate([kv_w_dkv[:, :KV_LORA], pe_p, pe_s], axis=1).astype(BF16)
                w_ukv = kv_w_ukv.reshape(KV_LORA, MLA_HEADS, MLA_NOPE + MLA_V)
                w_k = w_ukv[:, :, :MLA_NOPE].reshape(KV_LORA, MLA_HEADS * MLA_NOPE).astype(BF16)
                w_v = w_ukv[:, :, MLA_NOPE:].reshape(KV_LORA, MLA_HEADS * MLA_V).astype(BF16)
                kg_n, kg_r = _pad_rope_gain(kv_k_norm)
                k_sh, v_sh = _mla_kv(xf, kv_norm, kv_shift, kv_scale, w_dkv_p, kv_lat_norm, w_k, w_v,
                                     kg_n, kg_r, cos4, sin4, batch, seq, tm_small)
            q_lora = mla_w_dq.shape[2]
            w_uq = mla_w_uq[j].reshape(q_lora, MLA_HEADS, MLA_NOPE + MLA_ROPE)
            w_qn = w_uq[:, :, :MLA_NOPE].reshape(q_lora, MLA_HEADS * MLA_NOPE).astype(BF16)
            r_p, r_s = _split_rope_cols(w_uq[:, :, MLA_NOPE:])
            w_qr = r_p.reshape(q_lora, MLA_HEADS * LANE).astype(BF16)
            w_qs = r_s.reshape(q_lora, MLA_HEADS * LANE).astype(BF16)
            qg_n, qg_r = _pad_rope_gain(mla_q_norm[j])
            q = _mla_q(xf, norm_mix[layer], shift_m, scale_m, mla_w_dq[j].astype(BF16), mla_q_lat_norm[j],
                       w_qn, w_qr, w_qs, qg_n, qg_r, cos4, sin4, batch, seq, tm_small)
            mix_in = _attention(q, k_sh, v_sh, tm_mid).reshape(t, MLA_HEADS * MLA_V)
            w_out = mla_w_out[j].astype(BF16)
        xf = _matmul_residual(mix_in, w_out, xf, gate_m, seq, tm_big, 1024)
        xf = _ffn(xf, norm_ffn[layer], shift_f, scale_f, gate_f, w_gu_bf, w_down_bf, layer, seq, tm_mid, 512)
    return xf.reshape(batch, seq, d)
```

```python
import functools

import jax
import jax.numpy as jnp
from jax import lax
from jax.experimental import pallas as pl
from jax.experimental.pallas import tpu as pltpu

GLA_HEADS = 4
GLA_GATE_RANK = 16
GLA_TAU = 16.0
GLA_CHUNK = 64
MLA_HEADS = 16
MLA_NOPE = 128
MLA_ROPE = 64
MLA_V = 128
KV_LORA = 512
ROPE_THETA = 10000.0
EPS = 1e-6
LOG2_E = 1.4426950408889634

LANE = 128
MIB = 1 << 20

F32 = jnp.float32
BF16 = jnp.bfloat16


def _params(semantics, vmem_mib):
    return pltpu.CompilerParams(dimension_semantics=semantics, vmem_limit_bytes=vmem_mib * MIB)


def _silu(x):
    return x * jax.nn.sigmoid(x)


def _rms_scale(x):
    return x * lax.rsqrt(jnp.mean(x * x, axis=-1, keepdims=True) + EPS)


def _ada_kernel(c_ref, w_ref, b_ref, o_ref):
    nb, tn = o_ref.shape
    for b in range(nb):
        sb = _silu(c_ref[b])
        for j in range(tn // LANE):
            cols = slice(j * LANE, (j + 1) * LANE)
            acc = jnp.sum(w_ref[:, cols] * sb, axis=0, keepdims=True)
            o_ref[b:b + 1, cols] = acc + b_ref[:, cols]


def _ada_mod(c_rep, w, b, layer, tn=1024):
    nb, d, _ = c_rep.shape
    n = w.shape[2]
    return pl.pallas_call(
        _ada_kernel,
        out_shape=jax.ShapeDtypeStruct((nb, n), F32),
        grid=(n // tn,),
        in_specs=[
            pl.BlockSpec((nb, d, LANE), lambda j: (0, 0, 0)),
            pl.BlockSpec((None, d, tn), lambda j: (layer, 0, j)),
            pl.BlockSpec((None, 1, tn), lambda j: (layer, 0, j)),
        ],
        out_specs=pl.BlockSpec((nb, tn), lambda j: (0, j)),
        compiler_params=_params(("parallel",), 40),
        name="ada_mod",
    )(c_rep, w, b[:, None, :])


def _trig_kernel(a_ref, c_ref, s_ref):
    a = a_ref[...]
    c_ref[...] = jnp.cos(a)
    s_ref[...] = jnp.sin(a)


def _trig(ang):
    rows = ang.shape[0]
    tr = min(rows, 512)
    spec = pl.BlockSpec((tr, LANE), lambda i: (i, 0))
    return pl.pallas_call(
        _trig_kernel,
        out_shape=(jax.ShapeDtypeStruct(ang.shape, F32),) * 2,
        grid=(rows // tr,),
        in_specs=[spec],
        out_specs=(spec, spec),
        compiler_params=_params(("parallel",), 16),
        name="rope_trig",
    )(ang)


def _modulate(x, gain, shift, scale):
    return (_rms_scale(x) * gain) * (1.0 + scale) + shift


MOD_ROWS = 256


def _modulate_into(h_ref, x_ref, g_ref, sh_ref, sc_ref):
    tm = x_ref.shape[0]
    step = min(tm, MOD_ROWS)
    gain, shift, scale = g_ref[...], sh_ref[0], sc_ref[0]
    for r in range(tm // step):
        rows = slice(r * step, (r + 1) * step)
        h_ref[rows, :] = _modulate(x_ref[rows, :], gain, shift, scale).astype(h_ref.dtype)


def _modmm_kernel(x_ref, g_ref, sh_ref, sc_ref, w_ref, wa_ref, o_ref, a_ref, h_ref):
    @pl.when(pl.program_id(1) == 0)
    def _():
        _modulate_into(h_ref, x_ref, g_ref, sh_ref, sc_ref)
        a_ref[...] = jnp.dot(h_ref[...], wa_ref[...], preferred_element_type=F32).astype(a_ref.dtype)

    o_ref[...] = jnp.dot(h_ref[...], w_ref[...], preferred_element_type=F32).astype(o_ref.dtype)


def _mod_matmul(x, gain, shift, scale, w, layer, n, w_a, seq, tm, tn):
    t, d = x.shape
    per_b = seq // tm
    vec = lambda i, j: (i // per_b, 0, 0)
    return pl.pallas_call(
        _modmm_kernel,
        out_shape=(jax.ShapeDtypeStruct((t, n), BF16), jax.ShapeDtypeStruct((t, LANE), BF16)),
        grid=(t // tm, n // tn),
        in_specs=[
            pl.BlockSpec((tm, d), lambda i, j: (i, 0)),
            pl.BlockSpec((1, d), lambda i, j: (0, 0)),
            pl.BlockSpec((1, 1, d), vec),
            pl.BlockSpec((1, 1, d), vec),
            pl.BlockSpec((None, d, tn), lambda i, j: (layer, 0, j)),
            pl.BlockSpec((d, LANE), lambda i, j: (0, 0)),
        ],
        out_specs=(pl.BlockSpec((tm, tn), lambda i, j: (i, j)),
                   pl.BlockSpec((tm, LANE), lambda i, j: (i, 0))),
        scratch_shapes=[pltpu.VMEM((tm, d), BF16)],
        compiler_params=_params(("parallel", "arbitrary"), 56),
        name="mod_matmul",
    )(x, gain.reshape(1, d), shift[:, None, :], scale[:, None, :], w, w_a)


def _mm_res_kernel(a_ref, w_ref, x_ref, gate_ref, o_ref):
    acc = jnp.dot(a_ref[...], w_ref[...], preferred_element_type=F32)
    o_ref[...] = x_ref[...] + gate_ref[0] * acc


def _matmul_residual(a, w, x, gate, seq, tm, tn):
    t, k = a.shape
    n = w.shape[1]
    per_b = seq // tm
    return pl.pallas_call(
        _mm_res_kernel,
        out_shape=jax.ShapeDtypeStruct((t, n), F32),
        grid=(t // tm, n // tn),
        in_specs=[
            pl.BlockSpec((tm, k), lambda i, j: (i, 0)),
            pl.BlockSpec((k, tn), lambda i, j: (0, j)),
            pl.BlockSpec((tm, tn), lambda i, j: (i, j)),
            pl.BlockSpec((1, 1, tn), lambda i, j: (i // per_b, 0, j)),
        ],
        out_specs=pl.BlockSpec((tm, tn), lambda i, j: (i, j)),
        compiler_params=_params(("parallel", "arbitrary"), 48),
        name="matmul_residual",
    )(a, w, x, gate[:, None, :])


def _ffn_kernel(x_ref, g_ref, sh_ref, sc_ref, gate_ref, wg_ref, wu_ref, wd_ref, o_ref, h_ref, acc_ref):
    f = pl.program_id(1)

    @pl.when(f == 0)
    def _():
        _modulate_into(h_ref, x_ref, g_ref, sh_ref, sc_ref)
        acc_ref[...] = jnp.zeros_like(acc_ref)

    h = h_ref[...]
    g = jnp.dot(h, wg_ref[...], preferred_element_type=F32)
    u = jnp.dot(h, wu_ref[...], preferred_element_type=F32)
    a = (_silu(g) * u).astype(BF16)
    acc_ref[...] += jnp.dot(a, wd_ref[...], preferred_element_type=F32)

    @pl.when(f == pl.num_programs(1) - 1)
    def _():
        o_ref[...] = x_ref[...] + gate_ref[0] * acc_ref[...]


def _ffn(x, gain, shift, scale, gate, w_gu, w_down, layer, seq, tm, tf):
    t, d = x.shape
    dff = w_down.shape[1]
    nf = dff // tf
    per_b = seq // tm
    vec = lambda i, f: (i // per_b, 0, 0)
    return pl.pallas_call(
        _ffn_kernel,
        out_shape=jax.ShapeDtypeStruct((t, d), F32),
        grid=(t // tm, nf),
        in_specs=[
            pl.BlockSpec((tm, d), lambda i, f: (i, 0)),
            pl.BlockSpec((1, d), lambda i, f: (0, 0)),
            pl.BlockSpec((1, 1, d), vec),
            pl.BlockSpec((1, 1, d), vec),
            pl.BlockSpec((1, 1, d), vec),
            pl.BlockSpec((None, d, tf), lambda i, f: (layer, 0, f)),
            pl.BlockSpec((None, d, tf), lambda i, f: (layer, 0, f + nf)),
            pl.BlockSpec((None, tf, d), lambda i, f: (layer, f, 0)),
        ],
        out_specs=pl.BlockSpec((tm, d), lambda i, f: (i, 0)),
        scratch_shapes=[pltpu.VMEM((tm, d), BF16), pltpu.VMEM((tm, d), F32)],
        compiler_params=_params(("parallel", "arbitrary"), 56),
        name="ffn",
    )(x, gain.reshape(1, d), shift[:, None, :], scale[:, None, :], gate[:, None, :], w_gu, w_gu, w_down)


GLA_PAIR = 2 * GLA_CHUNK


GLA_BATCH_PER_STEP = 2


def _gla_pair(q_ref, k_ref, v_ref, g_ref, a_ref, wal_ref, bal_ref, on_ref, o_ref, st_ref, bb, dk, dv):
    ch = GLA_CHUNK
    tp = q_ref.shape[1]
    qk = q_ref.shape[2]

    z = jnp.dot(a_ref[bb], wal_ref[...], preferred_element_type=F32) + bal_ref[...]
    la = (jnp.minimum(z, 0.0) - jnp.log(1.0 + jnp.exp(-jnp.abs(z)))) * (LOG2_E / GLA_TAU)

    row = lax.broadcasted_iota(jnp.int32, (tp, tp), 0)
    col = lax.broadcasted_iota(jnp.int32, (tp, tp), 1)
    causal = jnp.logical_and(col <= row, col >= (row // ch) * ch)
    cross = jnp.logical_and(row >= ch, col < ch)
    tri = jnp.where(causal, 1.0, 0.0).astype(BF16)
    first = lax.broadcasted_iota(jnp.int32, (tp, qk), 0) < ch
    first_h = lax.broadcasted_iota(jnp.int32, (tp, dk), 0) < ch

    la_hi = la.astype(BF16)
    la_lo = (la - la_hi.astype(F32)).astype(BF16)
    bcum = jnp.dot(tri, la_hi, preferred_element_type=F32) + jnp.dot(tri, la_lo, preferred_element_type=F32)
    b_last0 = bcum[ch - 1:ch]
    b_last1 = bcum[tp - 1:tp]
    e_q = jnp.exp2(bcum)
    e_ki = jnp.exp2(-bcum)
    e_ks = jnp.exp2(jnp.where(first, b_last0, b_last1) - bcum)
    dec0 = jnp.exp2(b_last0)
    dec1 = jnp.exp2(b_last1)
    dec01 = jnp.exp2(b_last0 + b_last1)
    on = on_ref[...]
    tn_dims = (((0,), (0,)), ((), ()))
    nt_dims = (((1,), (1,)), ((), ()))

    for h in range(GLA_HEADS):
        ks = slice(h * dk, (h + 1) * dk)
        vs = slice(h * dv, (h + 1) * dv)
        q = q_ref[bb, :, ks].astype(F32) * (dk ** -0.5)
        k = k_ref[bb, :, ks].astype(F32)
        v = v_ref[bb, :, vs]
        q_dec = q * e_q[:, ks]
        k_st = k * e_ks[:, ks]
        q_b = q_dec.astype(BF16)
        k_in = (k * e_ki[:, ks]).astype(BF16)
        k0 = jnp.where(first_h, k_st, 0.0).astype(BF16)
        att = lax.dot_general(q_b, k_in, nt_dims, preferred_element_type=F32)
        att_x = lax.dot_general(q_b, k0, nt_dims, preferred_element_type=F32)
        att = jnp.where(causal, att, jnp.where(cross, att_x, 0.0)).astype(BF16)
        s0 = st_ref[bb, h]
        q_s = jnp.where(first_h, q_dec, q_dec * dec0[:, ks]).astype(BF16)
        o = (jnp.dot(att, v, preferred_element_type=F32)
             + jnp.dot(q_s, s0.astype(BF16), preferred_element_type=F32))
        k_u = jnp.where(first_h, k_st * dec1[:, ks], k_st).astype(BF16)
        dec_cols = jnp.tile(jnp.broadcast_to(dec01[:, ks], (LANE, dk)).T, (1, dv // LANE))
        st_ref[bb, h] = dec_cols * s0 + lax.dot_general(k_u, v, tn_dims, preferred_element_type=F32)
        gg = g_ref[bb, :, vs].astype(F32)
        o_ref[bb, :, vs] = ((_rms_scale(o) * on) * _silu(gg)).astype(o_ref.dtype)


def _gla_kernel(q_ref, k_ref, v_ref, g_ref, a_ref, wal_ref, bal_ref, on_ref, o_ref, st_ref, *, dk, dv):
    @pl.when(pl.program_id(1) == 0)
    def _():
        st_ref[...] = jnp.zeros_like(st_ref)

    for bb in range(q_ref.shape[0]):
        _gla_pair(q_ref, k_ref, v_ref, g_ref, a_ref, wal_ref, bal_ref, on_ref, o_ref, st_ref, bb, dk, dv)


def _gla(proj, a_lr, w_alpha_p, b_alpha, onorm, batch, seq):
    dk = w_alpha_p.shape[1] // GLA_HEADS
    dv = onorm.shape[0]
    qk = GLA_HEADS * dk
    vv = GLA_HEADS * dv
    tp = GLA_PAIR
    nb = GLA_BATCH_PER_STEP if batch % GLA_BATCH_PER_STEP == 0 else 1
    return pl.pallas_call(
        functools.partial(_gla_kernel, dk=dk, dv=dv),
        out_shape=jax.ShapeDtypeStruct((batch, seq, vv), BF16),
        grid=(batch // nb, seq // tp),
        in_specs=[
            pl.BlockSpec((nb, tp, qk), lambda b, i: (b, i, 0)),
            pl.BlockSpec((nb, tp, qk), lambda b, i: (b, i, 1)),
            pl.BlockSpec((nb, tp, vv), lambda b, i: (b, i, 2 * qk // vv)),
            pl.BlockSpec((nb, tp, vv), lambda b, i: (b, i, 2 * qk // vv + 1)),
            pl.BlockSpec((nb, tp, LANE), lambda b, i: (b, i, 0)),
            pl.BlockSpec((LANE, qk), lambda b, i: (0, 0)),
            pl.BlockSpec((1, qk), lambda b, i: (0, 0)),
            pl.BlockSpec((1, dv), lambda b, i: (0, 0)),
        ],
        out_specs=pl.BlockSpec((nb, tp, vv), lambda b, i: (b, i, 0)),
        scratch_shapes=[pltpu.VMEM((nb, GLA_HEADS, dk, dv), F32)],
        compiler_params=_params(("parallel", "arbitrary"), 32),
        name="gla",
    )(proj, proj, proj, proj, a_lr, w_alpha_p, b_alpha.reshape(1, qk), onorm.reshape(1, dv))


def _rope_pair(x, x_swapped, cos4, sin4):
    return x * cos4 + x_swapped * sin4


def _head_norm_store(o_ref, h, rows, nope, rope, gain_n, gain_r, dim, post_scale):
    ss = jnp.sum(nope * nope + rope * rope, axis=-1, keepdims=True)
    r = lax.rsqrt(ss / dim + EPS)
    if post_scale != 1.0:
        r = r * post_scale
    o_ref[0, h, rows, 0:LANE] = ((nope * r) * gain_n).astype(o_ref.dtype)
    o_ref[0, h, rows, LANE:2 * LANE] = ((rope * r) * gain_r).astype(o_ref.dtype)


MLA_SUB = 256


def _sub_tiles(tm):
    step = min(tm, MLA_SUB)
    return [slice(r * step, (r + 1) * step) for r in range(tm // step)]


def _mla_kv_kernel(x_ref, g_ref, sh_ref, sc_ref, wd_ref, ln_ref, wk_ref, wv_ref, kn_ref, kr_ref,
                   cos_ref, sin_ref, k_ref, v_ref):
    lora = ln_ref.shape[1]
    gn = kn_ref[...]
    gr = kr_ref[...]
    for rows in _sub_tiles(x_ref.shape[0]):
        h_in = _modulate(x_ref[rows, :], g_ref[...], sh_ref[0], sc_ref[0]).astype(BF16)
        ckv = jnp.dot(h_in, wd_ref[...], preferred_element_type=F32)
        c_lat = (_rms_scale(ckv[:, :lora]) * ln_ref[...]).astype(BF16)
        k_pe = _rope_pair(ckv[:, lora:lora + LANE], ckv[:, lora + LANE:lora + 2 * LANE],
                          cos_ref[rows, :], sin_ref[rows, :])
        kn = jnp.dot(c_lat, wk_ref[...], preferred_element_type=F32)
        vv = jnp.dot(c_lat, wv_ref[...], preferred_element_type=F32)
        for h in range(MLA_HEADS):
            cols = slice(h * LANE, (h + 1) * LANE)
            _head_norm_store(k_ref, h, rows, kn[:, cols], k_pe, gn, gr, MLA_NOPE + MLA_ROPE, 1.0)
            v_ref[0, h, 0, :, rows] = vv[:, cols].T.astype(v_ref.dtype)


def _mla_q_kernel(x_ref, g_ref, sh_ref, sc_ref, wd_ref, ln_ref, wn_ref, wr_ref, ws_ref, qn_ref, qr_ref,
                  cos_ref, sin_ref, q_ref):
    gn = qn_ref[...]
    gr = qr_ref[...]
    sm_scale = (MLA_NOPE + MLA_ROPE) ** -0.5 * LOG2_E
    for rows in _sub_tiles(x_ref.shape[0]):
        h_in = _modulate(x_ref[rows, :], g_ref[...], sh_ref[0], sc_ref[0]).astype(BF16)
        cq = jnp.dot(h_in, wd_ref[...], preferred_element_type=F32)
        cq = (_rms_scale(cq) * ln_ref[...]).astype(BF16)
        qn = jnp.dot(cq, wn_ref[...], preferred_element_type=F32)
        qr = jnp.dot(cq, wr_ref[...], preferred_element_type=F32)
        qs = jnp.dot(cq, ws_ref[...], preferred_element_type=F32)
        cos4 = cos_ref[rows, :]
        sin4 = sin_ref[rows, :]
        for h in range(MLA_HEADS):
            cols = slice(h * LANE, (h + 1) * LANE)
            rope = _rope_pair(qr[:, cols], qs[:, cols], cos4, sin4)
            _head_norm_store(q_ref, h, rows, qn[:, cols], rope, gn, gr, MLA_NOPE + MLA_ROPE, sm_scale)


def _const_spec(arr):
    nd = arr.ndim
    return pl.BlockSpec(arr.shape, lambda b, i: (0,) * nd)


def _mla_kv(x, gain, shift, scale, w_dkv_p, lat_norm, w_k, w_v, kg_n, kg_r, cos4, sin4, batch, seq, tm):
    t, d = x.shape
    nt = seq // tm
    row = lambda b, i: (b * nt + i, 0)
    vec = lambda b, i: (b, 0, 0)
    g2 = gain.reshape(1, d)
    weights = [w_dkv_p, lat_norm.reshape(1, -1), w_k, w_v, kg_n, kg_r]
    hk = pl.BlockSpec((1, MLA_HEADS, tm, 2 * LANE), lambda b, i: (b, 0, i, 0))
    hv = pl.BlockSpec((1, MLA_HEADS, 1, MLA_V, tm), lambda b, i: (b, 0, i, 0, 0))
    return pl.pallas_call(
        _mla_kv_kernel,
        out_shape=(jax.ShapeDtypeStruct((batch, MLA_HEADS, seq, 2 * LANE), BF16),
                   jax.ShapeDtypeStruct((batch, MLA_HEADS, nt, MLA_V, tm), BF16)),
        grid=(batch, nt),
        in_specs=[pl.BlockSpec((tm, d), row), _const_spec(g2),
                  pl.BlockSpec((1, 1, d), vec), pl.BlockSpec((1, 1, d), vec)]
                 + [_const_spec(w) for w in weights]
                 + [pl.BlockSpec((tm, LANE), row), pl.BlockSpec((tm, LANE), row)],
        out_specs=(hk, hv),
        compiler_params=_params(("parallel", "parallel"), 48),
        name="mla_kv",
    )(x, g2, shift[:, None, :], scale[:, None, :], *weights, cos4, sin4)


def _mla_q(x, gain, shift, scale, w_dq, q_lat_norm, w_qn, w_qr, w_qs, qg_n, qg_r, cos4, sin4, batch, seq, tm):
    t, d = x.shape
    nt = seq // tm
    row = lambda b, i: (b * nt + i, 0)
    vec = lambda b, i: (b, 0, 0)
    g2 = gain.reshape(1, d)
    weights = [w_dq, q_lat_norm.reshape(1, -1), w_qn, w_qr, w_qs, qg_n, qg_r]
    hq = pl.BlockSpec((1, MLA_HEADS, tm, 2 * LANE), lambda b, i: (b, 0, i, 0))
    return pl.pallas_call(
        _mla_q_kernel,
        out_shape=jax.ShapeDtypeStruct((batch, MLA_HEADS, seq, 2 * LANE), BF16),
        grid=(batch, nt),
        in_specs=[pl.BlockSpec((tm, d), row), _const_spec(g2),
                  pl.BlockSpec((1, 1, d), vec), pl.BlockSpec((1, 1, d), vec)]
                 + [_const_spec(w) for w in weights]
                 + [pl.BlockSpec((tm, LANE), row), pl.BlockSpec((tm, LANE), row)],
        out_specs=hq,
        compiler_params=_params(("parallel", "parallel"), 48),
        name="mla_q",
    )(x, g2, shift[:, None, :], scale[:, None, :], *weights, cos4, sin4)


ATTN_HEADS_PER_STEP = 4


def _attn_kernel(q_ref, k_ref, vt_ref, o_ref, *, tq, tk):
    qi = pl.program_id(2)
    nh = q_ref.shape[1]
    tc = vt_ref.shape[-1]
    dv = vt_ref.shape[-2]
    nt_dims = (((1,), (1,)), ((), ()))

    def step(j, carry, masked):
        scores = []
        for h in range(nh):
            k = k_ref[0, h, pl.ds(pl.multiple_of(j * tk, tk), tk), :]
            scores.append(lax.dot_general(k, q_ref[0, h], nt_dims, preferred_element_type=F32))
        out = []
        for h in range(nh):
            m, l, acc = carry[h]
            st = scores[h]
            if masked:
                kv_i = lax.broadcasted_iota(jnp.int32, (tk, tq), 0)
                q_i = lax.broadcasted_iota(jnp.int32, (tk, tq), 1)
                st = jnp.where(kv_i <= q_i, st, -jnp.inf)
            m_new = jnp.maximum(m, jnp.max(st, axis=0, keepdims=True))
            alpha = jnp.exp2(m - m_new)
            pt = jnp.exp2(st - m_new)
            l_new = alpha * l + jnp.sum(pt, axis=0, keepdims=True)
            pt = pt.astype(BF16)
            pv = jnp.dot(vt_ref[0, h, j * (tk // tc)], pt[0:tc], preferred_element_type=F32)
            for c in range(1, tk // tc):
                pv += jnp.dot(vt_ref[0, h, j * (tk // tc) + c], pt[c * tc:(c + 1) * tc],
                              preferred_element_type=F32)
            out.append((m_new, l_new, alpha * acc + pv))
        return tuple(out)

    init = tuple((jnp.full((1, tq), -jnp.inf, F32), jnp.zeros((1, tq), F32), jnp.zeros((dv, tq), F32))
                 for _ in range(nh))
    carry = lax.fori_loop(0, qi * (tq // tk), lambda j, c: step(j, c, False), init)
    carry = step(qi, carry, True)
    for h in range(nh):
        m, l, acc = carry[h]
        o_ref[0, :, h * dv:(h + 1) * dv] = (acc / l).T.astype(o_ref.dtype)


def _attention(q, k, vt, tq):
    b, h, s, dq = q.shape
    _, _, nslab, dvv, tc = vt.shape
    nh = ATTN_HEADS_PER_STEP
    return pl.pallas_call(
        functools.partial(_attn_kernel, tq=tq, tk=tq),
        out_shape=jax.ShapeDtypeStruct((b, s, h * dvv), BF16),
        grid=(b, h // nh, s // tq),
        in_specs=[
            pl.BlockSpec((1, nh, tq, dq), lambda bi, hi, qi: (bi, hi, qi, 0)),
            pl.BlockSpec((1, nh, s, dq), lambda bi, hi, qi: (bi, hi, 0, 0)),
            pl.BlockSpec((1, nh, nslab, dvv, tc), lambda bi, hi, qi: (bi, hi, 0, 0, 0)),
        ],
        out_specs=pl.BlockSpec((1, tq, nh * dvv), lambda bi, hi, qi: (bi, qi, hi)),
        compiler_params=_params(("parallel", "parallel", "arbitrary"), 40),
        name="mla_attention",
    )(q, k, vt)


def _rope_tables(positions):
    b, s = positions.shape
    half = MLA_ROPE // 2
    inv_freq = ROPE_THETA ** (-jnp.arange(half, dtype=F32) / half)
    ang = positions.astype(F32)[..., None] * inv_freq
    cos, sin = _trig(ang.reshape(-1, LANE))
    cos = cos.reshape(b * s, half)
    sin = sin.reshape(b * s, half)
    zero = jnp.zeros((b * s, LANE - 2 * half), F32)
    cos4 = jnp.concatenate([cos, cos, zero], axis=-1)
    sin4 = jnp.concatenate([-sin, sin, zero], axis=-1)
    return cos4, sin4


def _split_rope_cols(w_rope):
    half = MLA_ROPE // 2
    x1, x2 = w_rope[..., :half], w_rope[..., half:]
    zero = jnp.zeros(w_rope.shape[:-1] + (LANE - MLA_ROPE,), w_rope.dtype)
    return jnp.concatenate([x1, x2, zero], axis=-1), jnp.concatenate([x2, x1, zero], axis=-1)


def _pad_rope_gain(gain):
    g_n = gain[:MLA_NOPE].reshape(1, MLA_NOPE)
    g_r = jnp.concatenate([gain[MLA_NOPE:], jnp.zeros((LANE - MLA_ROPE,), gain.dtype)]).reshape(1, LANE)
    return g_n, g_r


def kernel(x, c, positions, ada_w, ada_b, norm_mix, norm_ffn, gla_w_in, gla_w_alpha, gla_b_alpha, gla_onorm,
           gla_w_out, mla_w_dq, mla_q_lat_norm, mla_w_uq, mla_q_norm, mla_w_out, kv_ada_w, kv_ada_b, kv_norm,
           kv_w_dkv, kv_lat_norm, kv_w_ukv, kv_k_norm, ffn_w_gu, ffn_w_down):
    batch, seq, d = x.shape
    t = batch * seq
    depth = ada_w.shape[0]
    n_gla = gla_w_in.shape[0]
    xf = x.reshape(t, d)

    tm_big = min(seq, 1024)
    tm_mid = min(seq, 512)

    c_rep = jnp.broadcast_to(c[:, :, None], (batch, d, LANE))
    cos4, sin4 = _rope_tables(positions)
    w_gu_bf = ffn_w_gu.astype(BF16)
    w_down_bf = ffn_w_down.astype(BF16)

    k_sh = v_sh = None
    for layer in range(depth):
        mod = _ada_mod(c_rep, ada_w, ada_b, layer)
        shift_m, scale_m, gate_m, shift_f, scale_f, gate_f = jnp.split(mod, 6, axis=-1)
        if layer < n_gla:
            qkvg = gla_w_in.shape[2] - GLA_GATE_RANK
            w_a = jnp.pad(gla_w_in[layer, :, qkvg:], ((0, 0), (0, LANE - GLA_GATE_RANK))).astype(BF16)
            proj, a_lr = _mod_matmul(xf, norm_mix[layer], shift_m, scale_m, gla_w_in.astype(BF16), layer, qkvg,
                                     w_a, seq, tm_big, 1024)
            w_alpha_p = jnp.pad(gla_w_alpha[layer], ((0, LANE - GLA_GATE_RANK), (0, 0))).astype(BF16)
            mix_in = _gla(proj.reshape(batch, seq, qkvg), a_lr.reshape(batch, seq, LANE), w_alpha_p,
                          gla_b_alpha[layer], gla_onorm[layer], batch, seq).reshape(t, -1)
            w_out = gla_w_out[layer].astype(BF16)
        else:
            j = layer - n_gla
            if j == 0:
                kv_mod = _ada_mod(c_rep, kv_ada_w[None], kv_ada_b[None], 0)
                kv_shift, kv_scale = jnp.split(kv_mod, 2, axis=-1)
                pe_p, pe_s = _split_rope_cols(kv_w_dkv[:, KV_LORA:])
                w_dkv_p = jnp.concatenate([kv_w_dkv[:, :KV_LORA], pe_p, pe_s], axis=1).astype(BF16)
                w_ukv = kv_w_ukv.reshape(KV_LORA, MLA_HEADS, MLA_NOPE + MLA_V)
                w_k = w_ukv[:, :, :MLA_NOPE].reshape(KV_LORA, MLA_HEADS * MLA_NOPE).astype(BF16)
                w_v = w_ukv[:, :, MLA_NOPE:].reshape(KV_LORA, MLA_HEADS * MLA_V).astype(BF16)
                kg_n, kg_r = _pad_rope_gain(kv_k_norm)
                k_sh, v_sh = _mla_kv(xf, kv_norm, kv_shift, kv_scale, w_dkv_p, kv_lat_norm, w_k, w_v,
                                     kg_n, kg_r, cos4, sin4, batch, seq, tm_mid)
            q_lora = mla_w_dq.shape[2]
            w_uq = mla_w_uq[j].reshape(q_lora, MLA_HEADS, MLA_NOPE + MLA_ROPE)
            w_qn = w_uq[:, :, :MLA_NOPE].reshape(q_lora, MLA_HEADS * MLA_NOPE).astype(BF16)
            r_p, r_s = _split_rope_cols(w_uq[:, :, MLA_NOPE:])
            w_qr = r_p.reshape(q_lora, MLA_HEADS * LANE).astype(BF16)
            w_qs = r_s.reshape(q_lora, MLA_HEADS * LANE).astype(BF16)
            qg_n, qg_r = _pad_rope_gain(mla_q_norm[j])
            q = _mla_q(xf, norm_mix[layer], shift_m, scale_m, mla_w_dq[j].astype(BF16), mla_q_lat_norm[j],
                       w_qn, w_qr, w_qs, qg_n, qg_r, cos4, sin4, batch, seq, tm_mid)
            mix_in = _attention(q, k_sh, v_sh, tm_mid).reshape(t, MLA_HEADS * MLA_V)
            w_out = mla_w_out[j].astype(BF16)
        xf = _matmul_residual(mix_in, w_out, xf, gate_m, seq, tm_big, 1024)
        xf = _ffn(xf, norm_ffn[layer], shift_f, scale_f, gate_f, w_gu_bf, w_down_bf, layer, seq, tm_mid, 512)
    return xf.reshape(batch, seq, d)
```

```python
import functools

import jax
import jax.numpy as jnp
from jax import lax
from jax.experimental import pallas as pl
from jax.experimental.pallas import tpu as pltpu

GLA_HEADS = 4
GLA_GATE_RANK = 16
GLA_TAU = 16.0
GLA_CHUNK = 64
MLA_HEADS = 16
MLA_NOPE = 128
MLA_ROPE = 64
MLA_V = 128
KV_LORA = 512
ROPE_THETA = 10000.0
EPS = 1e-6
LOG2_E = 1.4426950408889634

LANE = 128
MIB = 1 << 20

F32 = jnp.float32
BF16 = jnp.bfloat16


def _params(semantics, vmem_mib):
    return pltpu.CompilerParams(dimension_semantics=semantics, vmem_limit_bytes=vmem_mib * MIB)


def _silu(x):
    return x * jax.nn.sigmoid(x)


def _rms_scale(x):
    return x * lax.rsqrt(jnp.mean(x * x, axis=-1, keepdims=True) + EPS)


def _ada_kernel(c_ref, w_ref, b_ref, o_ref):
    nb, tn = o_ref.shape
    for b in range(nb):
        sb = _silu(c_ref[b])
        for j in range(tn // LANE):
            cols = slice(j * LANE, (j + 1) * LANE)
            acc = jnp.sum(w_ref[:, cols] * sb, axis=0, keepdims=True)
            o_ref[b:b + 1, cols] = acc + b_ref[:, cols]


def _ada_mod(c_rep, w, b, layer, tn=1024):
    nb, d, _ = c_rep.shape
    n = w.shape[2]
    return pl.pallas_call(
        _ada_kernel,
        out_shape=jax.ShapeDtypeStruct((nb, n), F32),
        grid=(n // tn,),
        in_specs=[
            pl.BlockSpec((nb, d, LANE), lambda j: (0, 0, 0)),
            pl.BlockSpec((None, d, tn), lambda j: (layer, 0, j)),
            pl.BlockSpec((None, 1, tn), lambda j: (layer, 0, j)),
        ],
        out_specs=pl.BlockSpec((nb, tn), lambda j: (0, j)),
        compiler_params=_params(("parallel",), 40),
        name="ada_mod",
    )(c_rep, w, b[:, None, :])


def _trig_kernel(a_ref, c_ref, s_ref):
    a = a_ref[...]
    c_ref[...] = jnp.cos(a)
    s_ref[...] = jnp.sin(a)


def _trig(ang):
    rows = ang.shape[0]
    tr = min(rows, 512)
    spec = pl.BlockSpec((tr, LANE), lambda i: (i, 0))
    return pl.pallas_call(
        _trig_kernel,
        out_shape=(jax.ShapeDtypeStruct(ang.shape, F32),) * 2,
        grid=(rows // tr,),
        in_specs=[spec],
        out_specs=(spec, spec),
        compiler_params=_params(("parallel",), 16),
        name="rope_trig",
    )(ang)


MOD_ROWS = 16


def _modulate_into(h_ref, x_ref, g_ref, sh_ref, sc_ref):
    tm = x_ref.shape[0]
    step = min(tm, MOD_ROWS)
    gs = g_ref[...] * (1.0 + sc_ref[0])
    shift = sh_ref[0]
    for r in range(tm // step):
        rows = slice(r * step, (r + 1) * step)
        h_ref[rows, :] = (_rms_scale(x_ref[rows, :]) * gs + shift).astype(h_ref.dtype)


def _modmm_kernel(x_ref, g_ref, sh_ref, sc_ref, w_ref, wa_ref, o_ref, a_ref, h_ref):
    @pl.when(pl.program_id(1) == 0)
    def _():
        _modulate_into(h_ref, x_ref, g_ref, sh_ref, sc_ref)
        a_ref[...] = jnp.dot(h_ref[...], wa_ref[...], preferred_element_type=F32).astype(a_ref.dtype)

    o_ref[...] = jnp.dot(h_ref[...], w_ref[...], preferred_element_type=F32).astype(o_ref.dtype)


def _mod_matmul(x, gain, shift, scale, w, layer, n, w_a, seq, tm, tn):
    t, d = x.shape
    per_b = seq // tm
    vec = lambda i, j: (i // per_b, 0, 0)
    return pl.pallas_call(
        _modmm_kernel,
        out_shape=(jax.ShapeDtypeStruct((t, n), BF16), jax.ShapeDtypeStruct((t, LANE), BF16)),
        grid=(t // tm, n // tn),
        in_specs=[
            pl.BlockSpec((tm, d), lambda i, j: (i, 0)),
            pl.BlockSpec((1, d), lambda i, j: (0, 0)),
            pl.BlockSpec((1, 1, d), vec),
            pl.BlockSpec((1, 1, d), vec),
            pl.BlockSpec((None, d, tn), lambda i, j: (layer, 0, j)),
            pl.BlockSpec((d, LANE), lambda i, j: (0, 0)),
        ],
        out_specs=(pl.BlockSpec((tm, tn), lambda i, j: (i, j)),
                   pl.BlockSpec((tm, LANE), lambda i, j: (i, 0))),
        scratch_shapes=[pltpu.VMEM((tm, d), BF16)],
        compiler_params=_params(("parallel", "arbitrary"), 56),
        name="mod_matmul",
    )(x, gain.reshape(1, d), shift[:, None, :], scale[:, None, :], w, w_a)


def _mm_res_kernel(a_ref, w_ref, x_ref, gate_ref, o_ref):
    acc = jnp.dot(a_ref[...], w_ref[...], preferred_element_type=F32)
    o_ref[...] = x_ref[...] + gate_ref[0] * acc


def _matmul_residual(a, w, x, gate, seq, tm, tn):
    t, k = a.shape
    n = w.shape[1]
    per_b = seq // tm
    return pl.pallas_call(
        _mm_res_kernel,
        out_shape=jax.ShapeDtypeStruct((t, n), F32),
        grid=(t // tm, n // tn),
        in_specs=[
            pl.BlockSpec((tm, k), lambda i, j: (i, 0)),
            pl.BlockSpec((k, tn), lambda i, j: (0, j)),
            pl.BlockSpec((tm, tn), lambda i, j: (i, j)),
            pl.BlockSpec((1, 1, tn), lambda i, j: (i // per_b, 0, j)),
        ],
        out_specs=pl.BlockSpec((tm, tn), lambda i, j: (i, j)),
        compiler_params=_params(("parallel", "arbitrary"), 48),
        name="matmul_residual",
    )(a, w, x, gate[:, None, :])


def _ffn_kernel(x_ref, g_ref, sh_ref, sc_ref, gate_ref, wg_ref, wu_ref, wd_ref, o_ref, h_ref, acc_ref):
    f = pl.program_id(1)

    @pl.when(f == 0)
    def _():
        _modulate_into(h_ref, x_ref, g_ref, sh_ref, sc_ref)
        acc_ref[...] = jnp.zeros_like(acc_ref)

    h = h_ref[...]
    g = jnp.dot(h, wg_ref[...], preferred_element_type=F32)
    u = jnp.dot(h, wu_ref[...], preferred_element_type=F32)
    a = (_silu(g) * u).astype(BF16)
    acc_ref[...] += jnp.dot(a, wd_ref[...], preferred_element_type=F32)

    @pl.when(f == pl.num_programs(1) - 1)
    def _():
        o_ref[...] = x_ref[...] + gate_ref[0] * acc_ref[...]


def _ffn(x, gain, shift, scale, gate, w_gu, w_down, layer, seq, tm, tf):
    t, d = x.shape
    dff = w_down.shape[1]
    nf = dff // tf
    per_b = seq // tm
    vec = lambda i, f: (i // per_b, 0, 0)
    return pl.pallas_call(
        _ffn_kernel,
        out_shape=jax.ShapeDtypeStruct((t, d), F32),
        grid=(t // tm, nf),
        in_specs=[
            pl.BlockSpec((tm, d), lambda i, f: (i, 0)),
            pl.BlockSpec((1, d), lambda i, f: (0, 0)),
            pl.BlockSpec((1, 1, d), vec),
            pl.BlockSpec((1, 1, d), vec),
            pl.BlockSpec((1, 1, d), vec),
            pl.BlockSpec((None, d, tf), lambda i, f: (layer, 0, f)),
            pl.BlockSpec((None, d, tf), lambda i, f: (layer, 0, f + nf)),
            pl.BlockSpec((None, tf, d), lambda i, f: (layer, f, 0)),
        ],
        out_specs=pl.BlockSpec((tm, d), lambda i, f: (i, 0)),
        scratch_shapes=[pltpu.VMEM((tm, d), BF16), pltpu.VMEM((tm, d), F32)],
        compiler_params=_params(("parallel", "arbitrary"), 56),
        name="ffn",
    )(x, gain.reshape(1, d), shift[:, None, :], scale[:, None, :], gate[:, None, :], w_gu, w_gu, w_down)


GLA_PAIR = 2 * GLA_CHUNK


GLA_BATCH_PER_STEP = 2


def _gla_pair(q_ref, k_ref, v_ref, g_ref, a_ref, wal_ref, bal_ref, on_ref, o_ref, st_ref, bb, dk, dv):
    ch = GLA_CHUNK
    tp = q_ref.shape[1]
    qk = q_ref.shape[2]

    z = jnp.dot(a_ref[bb], wal_ref[...], preferred_element_type=F32) + bal_ref[...]
    la = (jnp.minimum(z, 0.0) - jnp.log(1.0 + jnp.exp(-jnp.abs(z)))) * (LOG2_E / GLA_TAU)

    row = lax.broadcasted_iota(jnp.int32, (tp, tp), 0)
    col = lax.broadcasted_iota(jnp.int32, (tp, tp), 1)
    causal = jnp.logical_and(col <= row, col >= (row // ch) * ch)
    cross = jnp.logical_and(row >= ch, col < ch)
    tri = jnp.where(causal, 1.0, 0.0).astype(BF16)
    first = lax.broadcasted_iota(jnp.int32, (tp, qk), 0) < ch
    first_h = lax.broadcasted_iota(jnp.int32, (tp, dk), 0) < ch

    la_hi = la.astype(BF16)
    la_lo = (la - la_hi.astype(F32)).astype(BF16)
    bcum = jnp.dot(tri, la_hi, preferred_element_type=F32) + jnp.dot(tri, la_lo, preferred_element_type=F32)
    b_last0 = bcum[ch - 1:ch]
    b_last1 = bcum[tp - 1:tp]
    e_q = jnp.exp2(bcum)
    e_ki = jnp.exp2(-bcum)
    e_ks = jnp.exp2(jnp.where(first, b_last0, b_last1) - bcum)
    dec0 = jnp.exp2(b_last0)
    dec1 = jnp.exp2(b_last1)
    dec01 = jnp.exp2(b_last0 + b_last1)
    on = on_ref[...]
    tn_dims = (((0,), (0,)), ((), ()))
    nt_dims = (((1,), (1,)), ((), ()))

    for h in range(GLA_HEADS):
        ks = slice(h * dk, (h + 1) * dk)
        vs = slice(h * dv, (h + 1) * dv)
        q = q_ref[bb, :, ks].astype(F32) * (dk ** -0.5)
        k = k_ref[bb, :, ks].astype(F32)
        v = v_ref[bb, :, vs]
        q_dec = q * e_q[:, ks]
        k_st = k * e_ks[:, ks]
        q_b = q_dec.astype(BF16)
        k_in = (k * e_ki[:, ks]).astype(BF16)
        k0 = jnp.where(first_h, k_st, 0.0).astype(BF16)
        att = lax.dot_general(q_b, k_in, nt_dims, preferred_element_type=F32)
        att_x = lax.dot_general(q_b, k0, nt_dims, preferred_element_type=F32)
        att = jnp.where(causal, att, jnp.where(cross, att_x, 0.0)).astype(BF16)
        s0 = st_ref[bb, h]
        q_s = jnp.where(first_h, q_dec, q_dec * dec0[:, ks]).astype(BF16)
        o = (jnp.dot(att, v, preferred_element_type=F32)
             + jnp.dot(q_s, s0.astype(BF16), preferred_element_type=F32))
        k_u = jnp.where(first_h, k_st * dec1[:, ks], k_st).astype(BF16)
        dec_cols = jnp.tile(jnp.broadcast_to(dec01[:, ks], (LANE, dk)).T, (1, dv // LANE))
        st_ref[bb, h] = dec_cols * s0 + lax.dot_general(k_u, v, tn_dims, preferred_element_type=F32)
        gg = g_ref[bb, :, vs].astype(F32)
        o_ref[bb, :, vs] = ((_rms_scale(o) * on) * _silu(gg)).astype(o_ref.dtype)


def _gla_kernel(q_ref, k_ref, v_ref, g_ref, a_ref, wal_ref, bal_ref, on_ref, o_ref, st_ref, *, dk, dv):
    @pl.when(pl.program_id(1) == 0)
    def _():
        st_ref[...] = jnp.zeros_like(st_ref)

    for bb in range(q_ref.shape[0]):
        _gla_pair(q_ref, k_ref, v_ref, g_ref, a_ref, wal_ref, bal_ref, on_ref, o_ref, st_ref, bb, dk, dv)


def _gla(proj, a_lr, w_alpha_p, b_alpha, onorm, batch, seq):
    dk = w_alpha_p.shape[1] // GLA_HEADS
    dv = onorm.shape[0]
    qk = GLA_HEADS * dk
    vv = GLA_HEADS * dv
    tp = GLA_PAIR
    nb = GLA_BATCH_PER_STEP if batch % GLA_BATCH_PER_STEP == 0 else 1
    return pl.pallas_call(
        functools.partial(_gla_kernel, dk=dk, dv=dv),
        out_shape=jax.ShapeDtypeStruct((batch, seq, vv), BF16),
        grid=(batch // nb, seq // tp),
        in_specs=[
            pl.BlockSpec((nb, tp, qk), lambda b, i: (b, i, 0)),
            pl.BlockSpec((nb, tp, qk), lambda b, i: (b, i, 1)),
            pl.BlockSpec((nb, tp, vv), lambda b, i: (b, i, 2 * qk // vv)),
            pl.BlockSpec((nb, tp, vv), lambda b, i: (b, i, 2 * qk // vv + 1)),
            pl.BlockSpec((nb, tp, LANE), lambda b, i: (b, i, 0)),
            pl.BlockSpec((LANE, qk), lambda b, i: (0, 0)),
            pl.BlockSpec((1, qk), lambda b, i: (0, 0)),
            pl.BlockSpec((1, dv), lambda b, i: (0, 0)),
        ],
        out_specs=pl.BlockSpec((nb, tp, vv), lambda b, i: (b, i, 0)),
        scratch_shapes=[pltpu.VMEM((nb, GLA_HEADS, dk, dv), F32)],
        compiler_params=_params(("parallel", "arbitrary"), 32),
        name="gla",
    )(proj, proj, proj, proj, a_lr, w_alpha_p, b_alpha.reshape(1, qk), onorm.reshape(1, dv))


def _rope_pair(x, x_swapped, cos4, sin4):
    return x * cos4 + x_swapped * sin4


def _head_norm_store(o_ref, h, rows, nope, rope, gain_n, gain_r, dim, post_scale):
    ss = jnp.sum(nope * nope + rope * rope, axis=-1, keepdims=True)
    r = lax.rsqrt(ss / dim + EPS)
    if post_scale != 1.0:
        r = r * post_scale
    o_ref[0, h, rows, 0:LANE] = ((nope * r) * gain_n).astype(o_ref.dtype)
    o_ref[0, h, rows, LANE:2 * LANE] = ((rope * r) * gain_r).astype(o_ref.dtype)


MLA_SUB = 256


def _sub_tiles(tm):
    step = min(tm, MLA_SUB)
    return [slice(r * step, (r + 1) * step) for r in range(tm // step)]


def _mla_kv_kernel(x_ref, g_ref, sh_ref, sc_ref, wd_ref, ln_ref, wk_ref, wv_ref, kn_ref, kr_ref,
                   cos_ref, sin_ref, k_ref, v_ref, h_ref):
    lora = ln_ref.shape[1]
    gn = kn_ref[...]
    gr = kr_ref[...]
    _modulate_into(h_ref, x_ref, g_ref, sh_ref, sc_ref)
    for rows in _sub_tiles(x_ref.shape[0]):
        ckv = jnp.dot(h_ref[rows, :], wd_ref[...], preferred_element_type=F32)
        c_lat = (_rms_scale(ckv[:, :lora]) * ln_ref[...]).astype(BF16)
        k_pe = _rope_pair(ckv[:, lora:lora + LANE], ckv[:, lora + LANE:lora + 2 * LANE],
                          cos_ref[rows, :], sin_ref[rows, :])
        kn = jnp.dot(c_lat, wk_ref[...], preferred_element_type=F32)
        vv = jnp.dot(c_lat, wv_ref[...], preferred_element_type=F32)
        for h in range(MLA_HEADS):
            cols = slice(h * LANE, (h + 1) * LANE)
            _head_norm_store(k_ref, h, rows, kn[:, cols], k_pe, gn, gr, MLA_NOPE + MLA_ROPE, 1.0)
            v_ref[0, h, 0, :, rows] = vv[:, cols].T.astype(v_ref.dtype)


def _mla_q_kernel(x_ref, g_ref, sh_ref, sc_ref, wd_ref, ln_ref, wn_ref, wr_ref, ws_ref, qn_ref, qr_ref,
                  cos_ref, sin_ref, q_ref, h_ref):
    gn = qn_ref[...]
    gr = qr_ref[...]
    sm_scale = (MLA_NOPE + MLA_ROPE) ** -0.5 * LOG2_E
    _modulate_into(h_ref, x_ref, g_ref, sh_ref, sc_ref)
    for rows in _sub_tiles(x_ref.shape[0]):
        cq = jnp.dot(h_ref[rows, :], wd_ref[...], preferred_element_type=F32)
        cq = (_rms_scale(cq) * ln_ref[...]).astype(BF16)
        qn = jnp.dot(cq, wn_ref[...], preferred_element_type=F32)
        qr = jnp.dot(cq, wr_ref[...], preferred_element_type=F32)
        qs = jnp.dot(cq, ws_ref[...], preferred_element_type=F32)
        cos4 = cos_ref[rows, :]
        sin4 = sin_ref[rows, :]
        for h in range(MLA_HEADS):
            cols = slice(h * LANE, (h + 1) * LANE)
            rope = _rope_pair(qr[:, cols], qs[:, cols], cos4, sin4)
            _head_norm_store(q_ref, h, rows, qn[:, cols], rope, gn, gr, MLA_NOPE + MLA_ROPE, sm_scale)


def _const_spec(arr):
    nd = arr.ndim
    return pl.BlockSpec(arr.shape, lambda b, i: (0,) * nd)


def _mla_kv(x, gain, shift, scale, w_dkv_p, lat_norm, w_k, w_v, kg_n, kg_r, cos4, sin4, batch, seq, tm):
    t, d = x.shape
    nt = seq // tm
    row = lambda b, i: (b * nt + i, 0)
    vec = lambda b, i: (b, 0, 0)
    g2 = gain.reshape(1, d)
    weights = [w_dkv_p, lat_norm.reshape(1, -1), w_k, w_v, kg_n, kg_r]
    hk = pl.BlockSpec((1, MLA_HEADS, tm, 2 * LANE), lambda b, i: (b, 0, i, 0))
    hv = pl.BlockSpec((1, MLA_HEADS, 1, MLA_V, tm), lambda b, i: (b, 0, i, 0, 0))
    return pl.pallas_call(
        _mla_kv_kernel,
        out_shape=(jax.ShapeDtypeStruct((batch, MLA_HEADS, seq, 2 * LANE), BF16),
                   jax.ShapeDtypeStruct((batch, MLA_HEADS, nt, MLA_V, tm), BF16)),
        grid=(batch, nt),
        in_specs=[pl.BlockSpec((tm, d), row), _const_spec(g2),
                  pl.BlockSpec((1, 1, d), vec), pl.BlockSpec((1, 1, d), vec)]
                 + [_const_spec(w) for w in weights]
                 + [pl.BlockSpec((tm, LANE), row), pl.BlockSpec((tm, LANE), row)],
        out_specs=(hk, hv),
        scratch_shapes=[pltpu.VMEM((tm, d), BF16)],
        compiler_params=_params(("parallel", "parallel"), 48),
        name="mla_kv",
    )(x, g2, shift[:, None, :], scale[:, None, :], *weights, cos4, sin4)


def _mla_q(x, gain, shift, scale, w_dq, q_lat_norm, w_qn, w_qr, w_qs, qg_n, qg_r, cos4, sin4, batch, seq, tm):
    t, d = x.shape
    nt = seq // tm
    row = lambda b, i: (b * nt + i, 0)
    vec = lambda b, i: (b, 0, 0)
    g2 = gain.reshape(1, d)
    weights = [w_dq, q_lat_norm.reshape(1, -1), w_qn, w_qr, w_qs, qg_n, qg_r]
    hq = pl.BlockSpec((1, MLA_HEADS, tm, 2 * LANE), lambda b, i: (b, 0, i, 0))
    return pl.pallas_call(
        _mla_q_kernel,
        out_shape=jax.ShapeDtypeStruct((batch, MLA_HEADS, seq, 2 * LANE), BF16),
        grid=(batch, nt),
        in_specs=[pl.BlockSpec((tm, d), row), _const_spec(g2),
                  pl.BlockSpec((1, 1, d), vec), pl.BlockSpec((1, 1, d), vec)]
                 + [_const_spec(w) for w in weights]
                 + [pl.BlockSpec((tm, LANE), row), pl.BlockSpec((tm, LANE), row)],
        out_specs=hq,
        scratch_shapes=[pltpu.VMEM((tm, d), BF16)],
        compiler_params=_params(("parallel", "parallel"), 48),
        name="mla_q",
    )(x, g2, shift[:, None, :], scale[:, None, :], *weights, cos4, sin4)


ATTN_HEADS_PER_STEP = 4


def _attn_kernel(q_ref, k_ref, vt_ref, o_ref, *, tq):
    qi = pl.program_id(2)
    nh = q_ref.shape[1]
    tc = vt_ref.shape[-1]
    dv = vt_ref.shape[-2]
    nt_dims = (((1,), (1,)), ((), ()))

    def step(blk0, nblk, carry, masked):
        tk = nblk * tq
        scores = []
        for h in range(nh):
            k = k_ref[0, h, pl.ds(pl.multiple_of(blk0 * tq, tq), tk), :]
            scores.append(lax.dot_general(k, q_ref[0, h], nt_dims, preferred_element_type=F32))
        out = []
        for h in range(nh):
            m, l, acc = carry[h]
            st = scores[h]
            if masked:
                kv_i = lax.broadcasted_iota(jnp.int32, (tk, tq), 0)
                q_i = lax.broadcasted_iota(jnp.int32, (tk, tq), 1)
                st = jnp.where(kv_i - (nblk - 1) * tq <= q_i, st, -jnp.inf)
            m_new = jnp.maximum(m, jnp.max(st, axis=0, keepdims=True))
            alpha = jnp.exp2(m - m_new)
            pt = jnp.exp2(st - m_new)
            l_new = alpha * l + jnp.sum(pt, axis=0, keepdims=True)
            pt = pt.astype(BF16)
            slab0 = blk0 * (tq // tc)
            pv = jnp.dot(vt_ref[0, h, slab0], pt[0:tc], preferred_element_type=F32)
            for c in range(1, tk // tc):
                pv += jnp.dot(vt_ref[0, h, slab0 + c], pt[c * tc:(c + 1) * tc], preferred_element_type=F32)
            out.append((m_new, l_new, alpha * acc + pv))
        return tuple(out)

    init = tuple((jnp.full((1, tq), -jnp.inf, F32), jnp.zeros((1, tq), F32), jnp.zeros((dv, tq), F32))
                 for _ in range(nh))
    carry = lax.fori_loop(0, lax.shift_right_logical(qi, 1), lambda j, c: step(2 * j, 2, c, False), init)
    carry = lax.cond(jnp.bitwise_and(qi, 1) == 1,
                     lambda c: step(qi - 1, 2, c, True),
                     lambda c: step(qi, 1, c, True), carry)
    for h in range(nh):
        m, l, acc = carry[h]
        o_ref[0, :, h * dv:(h + 1) * dv] = (acc / l).T.astype(o_ref.dtype)


def _attention(q, k, vt, tq):
    b, h, s, dq = q.shape
    _, _, nslab, dvv, tc = vt.shape
    nh = ATTN_HEADS_PER_STEP
    return pl.pallas_call(
        functools.partial(_attn_kernel, tq=tq),
        out_shape=jax.ShapeDtypeStruct((b, s, h * dvv), BF16),
        grid=(b, h // nh, s // tq),
        in_specs=[
            pl.BlockSpec((1, nh, tq, dq), lambda bi, hi, qi: (bi, hi, qi, 0)),
            pl.BlockSpec((1, nh, s, dq), lambda bi, hi, qi: (bi, hi, 0, 0)),
            pl.BlockSpec((1, nh, nslab, dvv, tc), lambda bi, hi, qi: (bi, hi, 0, 0, 0)),
        ],
        out_specs=pl.BlockSpec((1, tq, nh * dvv), lambda bi, hi, qi: (bi, qi, hi)),
        compiler_params=_params(("parallel", "parallel", "arbitrary"), 48),
        name="mla_attention",
    )(q, k, vt)


def _rope_tables(positions):
    b, s = positions.shape
    half = MLA_ROPE // 2
    inv_freq = ROPE_THETA ** (-jnp.arange(half, dtype=F32) / half)
    ang = positions.astype(F32)[..., None] * inv_freq
    cos, sin = _trig(ang.reshape(-1, LANE))
    cos = cos.reshape(b * s, half)
    sin = sin.reshape(b * s, half)
    zero = jnp.zeros((b * s, LANE - 2 * half), F32)
    cos4 = jnp.concatenate([cos, cos, zero], axis=-1)
    sin4 = jnp.concatenate([-sin, sin, zero], axis=-1)
    return cos4, sin4


def _split_rope_cols(w_rope):
    half = MLA_ROPE // 2
    x1, x2 = w_rope[..., :half], w_rope[..., half:]
    zero = jnp.zeros(w_rope.shape[:-1] + (LANE - MLA_ROPE,), w_rope.dtype)
    return jnp.concatenate([x1, x2, zero], axis=-1), jnp.concatenate([x2, x1, zero], axis=-1)


def _pad_rope_gain(gain):
    g_n = gain[:MLA_NOPE].reshape(1, MLA_NOPE)
    g_r = jnp.concatenate([gain[MLA_NOPE:], jnp.zeros((LANE - MLA_ROPE,), gain.dtype)]).reshape(1, LANE)
    return g_n, g_r


def kernel(x, c, positions, ada_w, ada_b, norm_mix, norm_ffn, gla_w_in, gla_w_alpha, gla_b_alpha, gla_onorm,
           gla_w_out, mla_w_dq, mla_q_lat_norm, mla_w_uq, mla_q_norm, mla_w_out, kv_ada_w, kv_ada_b, kv_norm,
           kv_w_dkv, kv_lat_norm, kv_w_ukv, kv_k_norm, ffn_w_gu, ffn_w_down):
    batch, seq, d = x.shape
    t = batch * seq
    depth = ada_w.shape[0]
    n_gla = gla_w_in.shape[0]
    xf = x.reshape(t, d)

    tm_big = min(seq, 1024)
    tm_mid = min(seq, 512)

    c_rep = jnp.broadcast_to(c[:, :, None], (batch, d, LANE))
    cos4, sin4 = _rope_tables(positions)
    w_gu_bf = ffn_w_gu.astype(BF16)
    w_down_bf = ffn_w_down.astype(BF16)

    k_sh = v_sh = None
    for layer in range(depth):
        mod = _ada_mod(c_rep, ada_w, ada_b, layer)
        shift_m, scale_m, gate_m, shift_f, scale_f, gate_f = jnp.split(mod, 6, axis=-1)
        if layer < n_gla:
            qkvg = gla_w_in.shape[2] - GLA_GATE_RANK
            w_a = jnp.pad(gla_w_in[layer, :, qkvg:], ((0, 0), (0, LANE - GLA_GATE_RANK))).astype(BF16)
            proj, a_lr = _mod_matmul(xf, norm_mix[layer], shift_m, scale_m, gla_w_in.astype(BF16), layer, qkvg,
                                     w_a, seq, tm_big, 1024)
            w_alpha_p = jnp.pad(gla_w_alpha[layer], ((0, LANE - GLA_GATE_RANK), (0, 0))).astype(BF16)
            mix_in = _gla(proj.reshape(batch, seq, qkvg), a_lr.reshape(batch, seq, LANE), w_alpha_p,
                          gla_b_alpha[layer], gla_onorm[layer], batch, seq).reshape(t, -1)
            w_out = gla_w_out[layer].astype(BF16)
        else:
            j = layer - n_gla
            if j == 0:
                kv_mod = _ada_mod(c_rep, kv_ada_w[None], kv_ada_b[None], 0)
                kv_shift, kv_scale = jnp.split(kv_mod, 2, axis=-1)
                pe_p, pe_s = _split_rope_cols(kv_w_dkv[:, KV_LORA:])
                w_dkv_p = jnp.concatenate([kv_w_dkv[:, :KV_LORA], pe_p, pe_s], axis=1).astype(BF16)
                w_ukv = kv_w_ukv.reshape(KV_LORA, MLA_HEADS, MLA_NOPE + MLA_V)
                w_k = w_ukv[:, :, :MLA_NOPE].reshape(KV_LORA, MLA_HEADS * MLA_NOPE).astype(BF16)
                w_v = w_ukv[:, :, MLA_NOPE:].reshape(KV_LORA, MLA_HEADS * MLA_V).astype(BF16)
                kg_n, kg_r = _pad_rope_gain(kv_k_norm)
                k_sh, v_sh = _mla_kv(xf, kv_norm, kv_shift, kv_scale, w_dkv_p, kv_lat_norm, w_k, w_v,
                                     kg_n, kg_r, cos4, sin4, batch, seq, tm_mid)
            q_lora = mla_w_dq.shape[2]
            w_uq = mla_w_uq[j].reshape(q_lora, MLA_HEADS, MLA_NOPE + MLA_ROPE)
            w_qn = w_uq[:, :, :MLA_NOPE].reshape(q_lora, MLA_HEADS * MLA_NOPE).astype(BF16)
            r_p, r_s = _split_rope_cols(w_uq[:, :, MLA_NOPE:])
            w_qr = r_p.reshape(q_lora, MLA_HEADS * LANE).astype(BF16)
            w_qs = r_s.reshape(q_lora, MLA_HEADS * LANE).astype(BF16)
            qg_n, qg_r = _pad_rope_gain(mla_q_norm[j])
            q = _mla_q(xf, norm_mix[layer], shift_m, scale_m, mla_w_dq[j].astype(BF16), mla_q_lat_norm[j],
                       w_qn, w_qr, w_qs, qg_n, qg_r, cos4, sin4, batch, seq, tm_mid)
            mix_in = _attention(q, k_sh, v_sh, tm_mid).reshape(t, MLA_HEADS * MLA_V)
            w_out = mla_w_out[j].astype(BF16)
        xf = _matmul_residual(mix_in, w_out, xf, gate_m, seq, tm_big, 1024)
        xf = _ffn(xf, norm_ffn[layer], shift_f, scale_f, gate_f, w_gu_bf, w_down_bf, layer, seq, tm_mid, 512)
    return xf.reshape(batch, seq, d)
```

```python
import functools

import jax
import jax.numpy as jnp
from jax import lax
from jax.experimental import pallas as pl
from jax.experimental.pallas import tpu as pltpu

GLA_HEADS = 4
GLA_GATE_RANK = 16
GLA_TAU = 16.0
GLA_CHUNK = 64
MLA_HEADS = 16
MLA_NOPE = 128
MLA_ROPE = 64
MLA_V = 128
KV_LORA = 512
ROPE_THETA = 10000.0
EPS = 1e-6
LOG2_E = 1.4426950408889634

LANE = 128
MIB = 1 << 20

F32 = jnp.float32
BF16 = jnp.bfloat16


def _params(semantics, vmem_mib):
    return pltpu.CompilerParams(dimension_semantics=semantics, vmem_limit_bytes=vmem_mib * MIB)


def _silu(x):
    return x * jax.nn.sigmoid(x)


def _rms_scale(x):
    return x * lax.rsqrt(jnp.mean(x * x, axis=-1, keepdims=True) + EPS)


ADA_ROWS = 16


def _split_bf16(x):
    hi = x.astype(BF16)
    return hi, (x - hi.astype(F32)).astype(BF16)


def _ada_kernel(c_ref, w_ref, b_ref, o_ref):
    s_hi, s_lo = _split_bf16(_silu(c_ref[...]))
    w_hi, w_lo = _split_bf16(w_ref[...])
    acc = jnp.dot(s_hi, w_lo, preferred_element_type=F32) + jnp.dot(s_lo, w_hi, preferred_element_type=F32)
    o_ref[...] = (acc + jnp.dot(s_hi, w_hi, preferred_element_type=F32)) + b_ref[...]


def _ada_mod(c_pad, w, b, layer, tn=1024):
    rows, d = c_pad.shape
    n = w.shape[2]
    return pl.pallas_call(
        _ada_kernel,
        out_shape=jax.ShapeDtypeStruct((rows, n), F32),
        grid=(n // tn,),
        in_specs=[
            pl.BlockSpec((rows, d), lambda j: (0, 0)),
            pl.BlockSpec((None, d, tn), lambda j: (layer, 0, j)),
            pl.BlockSpec((None, 1, tn), lambda j: (layer, 0, j)),
        ],
        out_specs=pl.BlockSpec((rows, tn), lambda j: (0, j)),
        compiler_params=_params(("parallel",), 48),
        name="ada_mod",
    )(c_pad, w, b[:, None, :])


def _trig_kernel(a_ref, c_ref, s_ref):
    a = a_ref[...]
    c_ref[...] = jnp.cos(a)
    s_ref[...] = jnp.sin(a)


def _trig(ang):
    rows = ang.shape[0]
    tr = min(rows, 512)
    spec = pl.BlockSpec((tr, LANE), lambda i: (i, 0))
    return pl.pallas_call(
        _trig_kernel,
        out_shape=(jax.ShapeDtypeStruct(ang.shape, F32),) * 2,
        grid=(rows // tr,),
        in_specs=[spec],
        out_specs=(spec, spec),
        compiler_params=_params(("parallel",), 16),
        name="rope_trig",
    )(ang)


MOD_ROWS = 16


def _modulate_into(h_ref, x_ref, g_ref, sh_ref, sc_ref):
    tm = x_ref.shape[0]
    step = min(tm, MOD_ROWS)
    gs = g_ref[...] * (1.0 + sc_ref[0])
    shift = sh_ref[0]
    for r in range(tm // step):
        rows = slice(r * step, (r + 1) * step)
        h_ref[rows, :] = (_rms_scale(x_ref[rows, :]) * gs + shift).astype(h_ref.dtype)


def _modmm_kernel(x_ref, g_ref, sh_ref, sc_ref, w_ref, wa_ref, o_ref, a_ref, h_ref):
    @pl.when(pl.program_id(1) == 0)
    def _():
        _modulate_into(h_ref, x_ref, g_ref, sh_ref, sc_ref)
        a_ref[...] = jnp.dot(h_ref[...], wa_ref[...], preferred_element_type=F32).astype(a_ref.dtype)

    o_ref[...] = jnp.dot(h_ref[...], w_ref[...], preferred_element_type=F32).astype(o_ref.dtype)


def _mod_matmul(x, gain, shift, scale, w, layer, n, w_a, seq, tm, tn):
    t, d = x.shape
    per_b = seq // tm
    vec = lambda i, j: (i // per_b, 0, 0)
    return pl.pallas_call(
        _modmm_kernel,
        out_shape=(jax.ShapeDtypeStruct((t, n), BF16), jax.ShapeDtypeStruct((t, LANE), BF16)),
        grid=(t // tm, n // tn),
        in_specs=[
            pl.BlockSpec((tm, d), lambda i, j: (i, 0)),
            pl.BlockSpec((1, d), lambda i, j: (0, 0)),
            pl.BlockSpec((1, 1, d), vec),
            pl.BlockSpec((1, 1, d), vec),
            pl.BlockSpec((None, d, tn), lambda i, j: (layer, 0, j)),
            pl.BlockSpec((d, LANE), lambda i, j: (0, 0)),
        ],
        out_specs=(pl.BlockSpec((tm, tn), lambda i, j: (i, j)),
                   pl.BlockSpec((tm, LANE), lambda i, j: (i, 0))),
        scratch_shapes=[pltpu.VMEM((tm, d), BF16)],
        compiler_params=_params(("parallel", "arbitrary"), 56),
        name="mod_matmul",
    )(x, gain.reshape(1, d), shift[:, None, :], scale[:, None, :], w, w_a)


def _mm_res_kernel(a_ref, w_ref, x_ref, gate_ref, o_ref):
    acc = jnp.dot(a_ref[...], w_ref[...], preferred_element_type=F32)
    o_ref[...] = x_ref[...] + gate_ref[0] * acc


def _matmul_residual(a, w, x, gate, seq, tm, tn):
    t, k = a.shape
    n = w.shape[1]
    per_b = seq // tm
    return pl.pallas_call(
        _mm_res_kernel,
        out_shape=jax.ShapeDtypeStruct((t, n), F32),
        grid=(t // tm, n // tn),
        in_specs=[
            pl.BlockSpec((tm, k), lambda i, j: (i, 0)),
            pl.BlockSpec((k, tn), lambda i, j: (0, j)),
            pl.BlockSpec((tm, tn), lambda i, j: (i, j)),
            pl.BlockSpec((1, 1, tn), lambda i, j: (i // per_b, 0, j)),
        ],
        out_specs=pl.BlockSpec((tm, tn), lambda i, j: (i, j)),
        compiler_params=_params(("parallel", "arbitrary"), 48),
        name="matmul_residual",
    )(a, w, x, gate[:, None, :])


def _ffn_kernel(x_ref, g_ref, sh_ref, sc_ref, gate_ref, wg_ref, wu_ref, wd_ref, o_ref, h_ref):
    f = pl.program_id(1)

    @pl.when(f == 0)
    def _():
        _modulate_into(h_ref, x_ref, g_ref, sh_ref, sc_ref)
        o_ref[...] = jnp.zeros_like(o_ref)

    h = h_ref[...]
    g = jnp.dot(h, wg_ref[...], preferred_element_type=F32)
    u = jnp.dot(h, wu_ref[...], preferred_element_type=F32)
    a = (_silu(g) * u).astype(BF16)
    o_ref[...] += jnp.dot(a, wd_ref[...], preferred_element_type=F32)

    @pl.when(f == pl.num_programs(1) - 1)
    def _():
        o_ref[...] = x_ref[...] + gate_ref[0] * o_ref[...]


def _ffn(x, gain, shift, scale, gate, w_gu, w_down, layer, seq, tm, tf):
    t, d = x.shape
    dff = w_down.shape[1]
    nf = dff // tf
    per_b = seq // tm
    vec = lambda i, f: (i // per_b, 0, 0)
    return pl.pallas_call(
        _ffn_kernel,
        out_shape=jax.ShapeDtypeStruct((t, d), F32),
        grid=(t // tm, nf),
        in_specs=[
            pl.BlockSpec((tm, d), lambda i, f: (i, 0)),
            pl.BlockSpec((1, d), lambda i, f: (0, 0)),
            pl.BlockSpec((1, 1, d), vec),
            pl.BlockSpec((1, 1, d), vec),
            pl.BlockSpec((1, 1, d), vec),
            pl.BlockSpec((None, d, tf), lambda i, f: (layer, 0, f)),
            pl.BlockSpec((None, d, tf), lambda i, f: (layer, 0, f + nf)),
            pl.BlockSpec((None, tf, d), lambda i, f: (layer, f, 0)),
        ],
        out_specs=pl.BlockSpec((tm, d), lambda i, f: (i, 0)),
        scratch_shapes=[pltpu.VMEM((tm, d), BF16)],
        compiler_params=_params(("parallel", "arbitrary"), 60),
        name="ffn",
    )(x, gain.reshape(1, d), shift[:, None, :], scale[:, None, :], gate[:, None, :], w_gu, w_gu, w_down)


GLA_PAIR = 2 * GLA_CHUNK


GLA_BATCH_PER_STEP = 2


def _gla_pair(q_ref, k_ref, v_ref, g_ref, a_ref, wal_ref, bal_ref, on_ref, o_ref, st_ref, bb, dk, dv):
    ch = GLA_CHUNK
    tp = q_ref.shape[1]
    qk = q_ref.shape[2]

    z = jnp.dot(a_ref[bb], wal_ref[...], preferred_element_type=F32) + bal_ref[...]
    la = (jnp.minimum(z, 0.0) - jnp.log(1.0 + jnp.exp(-jnp.abs(z)))) * (LOG2_E / GLA_TAU)

    row = lax.broadcasted_iota(jnp.int32, (tp, tp), 0)
    col = lax.broadcasted_iota(jnp.int32, (tp, tp), 1)
    causal = jnp.logical_and(col <= row, col >= (row // ch) * ch)
    cross = jnp.logical_and(row >= ch, col < ch)
    tri = jnp.where(causal, 1.0, 0.0).astype(BF16)
    first = lax.broadcasted_iota(jnp.int32, (tp, qk), 0) < ch
    first_h = lax.broadcasted_iota(jnp.int32, (tp, dk), 0) < ch

    la_hi = la.astype(BF16)
    la_lo = (la - la_hi.astype(F32)).astype(BF16)
    bcum = jnp.dot(tri, la_hi, preferred_element_type=F32) + jnp.dot(tri, la_lo, preferred_element_type=F32)
    b_last0 = bcum[ch - 1:ch]
    b_last1 = bcum[tp - 1:tp]
    e_q = jnp.exp2(bcum)
    e_ki = jnp.exp2(-bcum)
    e_ks = jnp.exp2(jnp.where(first, b_last0, b_last1) - bcum)
    dec0 = jnp.exp2(b_last0)
    dec1 = jnp.exp2(b_last1)
    dec01 = jnp.exp2(b_last0 + b_last1)
    on = on_ref[...]
    tn_dims = (((0,), (0,)), ((), ()))
    nt_dims = (((1,), (1,)), ((), ()))

    for h in range(GLA_HEADS):
        ks = slice(h * dk, (h + 1) * dk)
        vs = slice(h * dv, (h + 1) * dv)
        q = q_ref[bb, :, ks].astype(F32) * (dk ** -0.5)
        k = k_ref[bb, :, ks].astype(F32)
        v = v_ref[bb, :, vs]
        q_dec = q * e_q[:, ks]
        k_st = k * e_ks[:, ks]
        q_b = q_dec.astype(BF16)
        k_in = (k * e_ki[:, ks]).astype(BF16)
        k0 = jnp.where(first_h, k_st, 0.0).astype(BF16)
        att = lax.dot_general(q_b, k_in, nt_dims, preferred_element_type=F32)
        att_x = lax.dot_general(q_b, k0, nt_dims, preferred_element_type=F32)
        att = jnp.where(causal, att, jnp.where(cross, att_x, 0.0)).astype(BF16)
        s0 = st_ref[bb, h]
        q_s = jnp.where(first_h, q_dec, q_dec * dec0[:, ks]).astype(BF16)
        o = (jnp.dot(att, v, preferred_element_type=F32)
             + jnp.dot(q_s, s0.astype(BF16), preferred_element_type=F32))
        k_u = jnp.where(first_h, k_st * dec1[:, ks], k_st).astype(BF16)
        dec_cols = jnp.tile(jnp.broadcast_to(dec01[:, ks], (LANE, dk)).T, (1, dv // LANE))
        st_ref[bb, h] = dec_cols * s0 + lax.dot_general(k_u, v, tn_dims, preferred_element_type=F32)
        gg = g_ref[bb, :, vs].astype(F32)
        o_ref[bb, :, vs] = ((_rms_scale(o) * on) * _silu(gg)).astype(o_ref.dtype)


def _gla_kernel(q_ref, k_ref, v_ref, g_ref, a_ref, wal_ref, bal_ref, on_ref, o_ref, st_ref, *, dk, dv):
    @pl.when(pl.program_id(1) == 0)
    def _():
        st_ref[...] = jnp.zeros_like(st_ref)

    for bb in range(q_ref.shape[0]):
        _gla_pair(q_ref, k_ref, v_ref, g_ref, a_ref, wal_ref, bal_ref, on_ref, o_ref, st_ref, bb, dk, dv)


def _gla(proj, a_lr, w_alpha_p, b_alpha, onorm, batch, seq):
    dk = w_alpha_p.shape[1] // GLA_HEADS
    dv = onorm.shape[0]
    qk = GLA_HEADS * dk
    vv = GLA_HEADS * dv
    tp = GLA_PAIR
    nb = GLA_BATCH_PER_STEP if batch % GLA_BATCH_PER_STEP == 0 else 1
    return pl.pallas_call(
        functools.partial(_gla_kernel, dk=dk, dv=dv),
        out_shape=jax.ShapeDtypeStruct((batch, seq, vv), BF16),
        grid=(batch // nb, seq // tp),
        in_specs=[
            pl.BlockSpec((nb, tp, qk), lambda b, i: (b, i, 0)),
            pl.BlockSpec((nb, tp, qk), lambda b, i: (b, i, 1)),
            pl.BlockSpec((nb, tp, vv), lambda b, i: (b, i, 2 * qk // vv)),
            pl.BlockSpec((nb, tp, vv), lambda b, i: (b, i, 2 * qk // vv + 1)),
            pl.BlockSpec((nb, tp, LANE), lambda b, i: (b, i, 0)),
            pl.BlockSpec((LANE, qk), lambda b, i: (0, 0)),
            pl.BlockSpec((1, qk), lambda b, i: (0, 0)),
            pl.BlockSpec((1, dv), lambda b, i: (0, 0)),
        ],
        out_specs=pl.BlockSpec((nb, tp, vv), lambda b, i: (b, i, 0)),
        scratch_shapes=[pltpu.VMEM((nb, GLA_HEADS, dk, dv), F32)],
        compiler_params=_params(("parallel", "arbitrary"), 32),
        name="gla",
    )(proj, proj, proj, proj, a_lr, w_alpha_p, b_alpha.reshape(1, qk), onorm.reshape(1, dv))


def _rope_pair(x, x_swapped, cos4, sin4):
    return x * cos4 + x_swapped * sin4


def _head_norm_store(o_ref, h, rows, nope, rope, gain_n, gain_r, dim, post_scale):
    ss = jnp.sum(nope * nope + rope * rope, axis=-1, keepdims=True)
    r = lax.rsqrt(ss / dim + EPS)
    if post_scale != 1.0:
        r = r * post_scale
    o_ref[0, h, rows, 0:LANE] = ((nope * r) * gain_n).astype(o_ref.dtype)
    o_ref[0, h, rows, LANE:2 * LANE] = ((rope * r) * gain_r).astype(o_ref.dtype)


MLA_SUB = 256


def _sub_tiles(tm):
    step = min(tm, MLA_SUB)
    return [slice(r * step, (r + 1) * step) for r in range(tm // step)]


def _mla_kv_kernel(x_ref, g_ref, sh_ref, sc_ref, wd_ref, ln_ref, wk_ref, wv_ref, kn_ref, kr_ref,
                   cos_ref, sin_ref, k_ref, v_ref, h_ref):
    lora = ln_ref.shape[1]
    gn = kn_ref[...]
    gr = kr_ref[...]
    _modulate_into(h_ref, x_ref, g_ref, sh_ref, sc_ref)
    for rows in _sub_tiles(x_ref.shape[0]):
        ckv = jnp.dot(h_ref[rows, :], wd_ref[...], preferred_element_type=F32)
        c_lat = (_rms_scale(ckv[:, :lora]) * ln_ref[...]).astype(BF16)
        k_pe = _rope_pair(ckv[:, lora:lora + LANE], ckv[:, lora + LANE:lora + 2 * LANE],
                          cos_ref[rows, :], sin_ref[rows, :])
        kn = jnp.dot(c_lat, wk_ref[...], preferred_element_type=F32)
        vv = jnp.dot(c_lat, wv_ref[...], preferred_element_type=F32)
        for h in range(MLA_HEADS):
            cols = slice(h * LANE, (h + 1) * LANE)
            _head_norm_store(k_ref, h, rows, kn[:, cols], k_pe, gn, gr, MLA_NOPE + MLA_ROPE, 1.0)
            v_ref[0, h, 0, :, rows] = vv[:, cols].T.astype(v_ref.dtype)


def _mla_q_kernel(x_ref, g_ref, sh_ref, sc_ref, wd_ref, ln_ref, wn_ref, wr_ref, ws_ref, qn_ref, qr_ref,
                  cos_ref, sin_ref, q_ref, h_ref):
    gn = qn_ref[...]
    gr = qr_ref[...]
    sm_scale = (MLA_NOPE + MLA_ROPE) ** -0.5 * LOG2_E
    _modulate_into(h_ref, x_ref, g_ref, sh_ref, sc_ref)
    for rows in _sub_tiles(x_ref.shape[0]):
        cq = jnp.dot(h_ref[rows, :], wd_ref[...], preferred_element_type=F32)
        cq = (_rms_scale(cq) * ln_ref[...]).astype(BF16)
        qn = jnp.dot(cq, wn_ref[...], preferred_element_type=F32)
        qr = jnp.dot(cq, wr_ref[...], preferred_element_type=F32)
        qs = jnp.dot(cq, ws_ref[...], preferred_element_type=F32)
        cos4 = cos_ref[rows, :]
        sin4 = sin_ref[rows, :]
        for h in range(MLA_HEADS):
            cols = slice(h * LANE, (h + 1) * LANE)
            rope = _rope_pair(qr[:, cols], qs[:, cols], cos4, sin4)
            _head_norm_store(q_ref, h, rows, qn[:, cols], rope, gn, gr, MLA_NOPE + MLA_ROPE, sm_scale)


def _const_spec(arr):
    nd = arr.ndim
    return pl.BlockSpec(arr.shape, lambda b, i: (0,) * nd)


def _mla_kv(x, gain, shift, scale, w_dkv_p, lat_norm, w_k, w_v, kg_n, kg_r, cos4, sin4, batch, seq, tm):
    t, d = x.shape
    nt = seq // tm
    row = lambda b, i: (b * nt + i, 0)
    vec = lambda b, i: (b, 0, 0)
    g2 = gain.reshape(1, d)
    weights = [w_dkv_p, lat_norm.reshape(1, -1), w_k, w_v, kg_n, kg_r]
    hk = pl.BlockSpec((1, MLA_HEADS, tm, 2 * LANE), lambda b, i: (b, 0, i, 0))
    hv = pl.BlockSpec((1, MLA_HEADS, 1, MLA_V, tm), lambda b, i: (b, 0, i, 0, 0))
    return pl.pallas_call(
        _mla_kv_kernel,
        out_shape=(jax.ShapeDtypeStruct((batch, MLA_HEADS, seq, 2 * LANE), BF16),
                   jax.ShapeDtypeStruct((batch, MLA_HEADS, nt, MLA_V, tm), BF16)),
        grid=(batch, nt),
        in_specs=[pl.BlockSpec((tm, d), row), _const_spec(g2),
                  pl.BlockSpec((1, 1, d), vec), pl.BlockSpec((1, 1, d), vec)]
                 + [_const_spec(w) for w in weights]
                 + [pl.BlockSpec((tm, LANE), row), pl.BlockSpec((tm, LANE), row)],
        out_specs=(hk, hv),
        scratch_shapes=[pltpu.VMEM((tm, d), BF16)],
        compiler_params=_params(("parallel", "parallel"), 48),
        name="mla_kv",
    )(x, g2, shift[:, None, :], scale[:, None, :], *weights, cos4, sin4)


def _mla_q(x, gain, shift, scale, w_dq, q_lat_norm, w_qn, w_qr, w_qs, qg_n, qg_r, cos4, sin4, batch, seq, tm):
    t, d = x.shape
    nt = seq // tm
    row = lambda b, i: (b * nt + i, 0)
    vec = lambda b, i: (b, 0, 0)
    g2 = gain.reshape(1, d)
    weights = [w_dq, q_lat_norm.reshape(1, -1), w_qn, w_qr, w_qs, qg_n, qg_r]
    hq = pl.BlockSpec((1, MLA_HEADS, tm, 2 * LANE), lambda b, i: (b, 0, i, 0))
    return pl.pallas_call(
        _mla_q_kernel,
        out_shape=jax.ShapeDtypeStruct((batch, MLA_HEADS, seq, 2 * LANE), BF16),
        grid=(batch, nt),
        in_specs=[pl.BlockSpec((tm, d), row), _const_spec(g2),
                  pl.BlockSpec((1, 1, d), vec), pl.BlockSpec((1, 1, d), vec)]
                 + [_const_spec(w) for w in weights]
                 + [pl.BlockSpec((tm, LANE), row), pl.BlockSpec((tm, LANE), row)],
        out_specs=hq,
        scratch_shapes=[pltpu.VMEM((tm, d), BF16)],
        compiler_params=_params(("parallel", "parallel"), 48),
        name="mla_q",
    )(x, g2, shift[:, None, :], scale[:, None, :], *weights, cos4, sin4)


ATTN_HEADS_PER_STEP = 4


def _attn_kernel(q_ref, k_ref, vt_ref, o_ref, *, tq):
    qi = pl.program_id(2)
    nh = q_ref.shape[1]
    tc = vt_ref.shape[-1]
    dv = vt_ref.shape[-2]
    nt_dims = (((1,), (1,)), ((), ()))

    def step(blk0, nblk, carry, masked):
        tk = nblk * tq
        scores = []
        for h in range(nh):
            k = k_ref[0, h, pl.ds(pl.multiple_of(blk0 * tq, tq), tk), :]
            scores.append(lax.dot_general(k, q_ref[0, h], nt_dims, preferred_element_type=F32))
        out = []
        for h in range(nh):
            m, l, acc = carry[h]
            st = scores[h]
            if masked:
                kv_i = lax.broadcasted_iota(jnp.int32, (tk, tq), 0)
                q_i = lax.broadcasted_iota(jnp.int32, (tk, tq), 1)
                st = jnp.where(kv_i - (nblk - 1) * tq <= q_i, st, -jnp.inf)
            m_new = jnp.maximum(m, jnp.max(st, axis=0, keepdims=True))
            alpha = jnp.exp2(m - m_new)
            pt = jnp.exp2(st - m_new)
            l_new = alpha * l + jnp.sum(pt, axis=0, keepdims=True)
            pt = pt.astype(BF16)
            slab0 = blk0 * (tq // tc)
            pv = jnp.dot(vt_ref[0, h, slab0], pt[0:tc], preferred_element_type=F32)
            for c in range(1, tk // tc):
                pv += jnp.dot(vt_ref[0, h, slab0 + c], pt[c * tc:(c + 1) * tc], preferred_element_type=F32)
            out.append((m_new, l_new, alpha * acc + pv))
        return tuple(out)

    init = tuple((jnp.full((1, tq), -jnp.inf, F32), jnp.zeros((1, tq), F32), jnp.zeros((dv, tq), F32))
                 for _ in range(nh))
    carry = lax.fori_loop(0, lax.shift_right_logical(qi, 1), lambda j, c: step(2 * j, 2, c, False), init)
    carry = lax.cond(jnp.bitwise_and(qi, 1) == 1,
                     lambda c: step(qi - 1, 2, c, True),
                     lambda c: step(qi, 1, c, True), carry)
    for h in range(nh):
        m, l, acc = carry[h]
        o_ref[0, :, h * dv:(h + 1) * dv] = (acc / l).T.astype(o_ref.dtype)


def _attention(q, k, vt, tq):
    b, h, s, dq = q.shape
    _, _, nslab, dvv, tc = vt.shape
    nh = ATTN_HEADS_PER_STEP
    return pl.pallas_call(
        functools.partial(_attn_kernel, tq=tq),
        out_shape=jax.ShapeDtypeStruct((b, s, h * dvv), BF16),
        grid=(b, h // nh, s // tq),
        in_specs=[
            pl.BlockSpec((1, nh, tq, dq), lambda bi, hi, qi: (bi, hi, qi, 0)),
            pl.BlockSpec((1, nh, s, dq), lambda bi, hi, qi: (bi, hi, 0, 0)),
            pl.BlockSpec((1, nh, nslab, dvv, tc), lambda bi, hi, qi: (bi, hi, 0, 0, 0)),
        ],
        out_specs=pl.BlockSpec((1, tq, nh * dvv), lambda bi, hi, qi: (bi, qi, hi)),
        compiler_params=_params(("parallel", "parallel", "arbitrary"), 48),
        name="mla_attention",
    )(q, k, vt)


def _rope_tables(positions):
    b, s = positions.shape
    half = MLA_ROPE // 2
    inv_freq = ROPE_THETA ** (-jnp.arange(half, dtype=F32) / half)
    ang = positions.astype(F32)[..., None] * inv_freq
    cos, sin = _trig(ang.reshape(-1, LANE))
    cos = cos.reshape(b * s, half)
    sin = sin.reshape(b * s, half)
    zero = jnp.zeros((b * s, LANE - 2 * half), F32)
    cos4 = jnp.concatenate([cos, cos, zero], axis=-1)
    sin4 = jnp.concatenate([-sin, sin, zero], axis=-1)
    return cos4, sin4


def _split_rope_cols(w_rope):
    half = MLA_ROPE // 2
    x1, x2 = w_rope[..., :half], w_rope[..., half:]
    zero = jnp.zeros(w_rope.shape[:-1] + (LANE - MLA_ROPE,), w_rope.dtype)
    return jnp.concatenate([x1, x2, zero], axis=-1), jnp.concatenate([x2, x1, zero], axis=-1)


def _pad_rope_gain(gain):
    g_n = gain[:MLA_NOPE].reshape(1, MLA_NOPE)
    g_r = jnp.concatenate([gain[MLA_NOPE:], jnp.zeros((LANE - MLA_ROPE,), gain.dtype)]).reshape(1, LANE)
    return g_n, g_r


def kernel(x, c, positions, ada_w, ada_b, norm_mix, norm_ffn, gla_w_in, gla_w_alpha, gla_b_alpha, gla_onorm,
           gla_w_out, mla_w_dq, mla_q_lat_norm, mla_w_uq, mla_q_norm, mla_w_out, kv_ada_w, kv_ada_b, kv_norm,
           kv_w_dkv, kv_lat_norm, kv_w_ukv, kv_k_norm, ffn_w_gu, ffn_w_down):
    batch, seq, d = x.shape
    t = batch * seq
    depth = ada_w.shape[0]
    n_gla = gla_w_in.shape[0]
    xf = x.reshape(t, d)

    tm_big = min(seq, 1024)
    tm_mid = min(seq, 512)

    c_pad = jnp.pad(c, ((0, ADA_ROWS - batch), (0, 0)))
    cos4, sin4 = _rope_tables(positions)
    w_gu_bf = ffn_w_gu.astype(BF16)
    w_down_bf = ffn_w_down.astype(BF16)

    k_sh = v_sh = None
    for layer in range(depth):
        mod = _ada_mod(c_pad, ada_w, ada_b, layer)[:batch]
        shift_m, scale_m, gate_m, shift_f, scale_f, gate_f = jnp.split(mod, 6, axis=-1)
        if layer < n_gla:
            qkvg = gla_w_in.shape[2] - GLA_GATE_RANK
            w_a = jnp.pad(gla_w_in[layer, :, qkvg:], ((0, 0), (0, LANE - GLA_GATE_RANK))).astype(BF16)
            proj, a_lr = _mod_matmul(xf, norm_mix[layer], shift_m, scale_m, gla_w_in.astype(BF16), layer, qkvg,
                                     w_a, seq, tm_big, 1024)
            w_alpha_p = jnp.pad(gla_w_alpha[layer], ((0, LANE - GLA_GATE_RANK), (0, 0))).astype(BF16)
            mix_in = _gla(proj.reshape(batch, seq, qkvg), a_lr.reshape(batch, seq, LANE), w_alpha_p,
                          gla_b_alpha[layer], gla_onorm[layer], batch, seq).reshape(t, -1)
            w_out = gla_w_out[layer].astype(BF16)
        else:
            j = layer - n_gla
            if j == 0:
                kv_mod = _ada_mod(c_pad, kv_ada_w[None], kv_ada_b[None], 0)[:batch]
                kv_shift, kv_scale = jnp.split(kv_mod, 2, axis=-1)
                pe_p, pe_s = _split_rope_cols(kv_w_dkv[:, KV_LORA:])
                w_dkv_p = jnp.concatenate([kv_w_dkv[:, :KV_LORA], pe_p, pe_s], axis=1).astype(BF16)
                w_ukv = kv_w_ukv.reshape(KV_LORA, MLA_HEADS, MLA_NOPE + MLA_V)
                w_k = w_ukv[:, :, :MLA_NOPE].reshape(KV_LORA, MLA_HEADS * MLA_NOPE).astype(BF16)
                w_v = w_ukv[:, :, MLA_NOPE:].reshape(KV_LORA, MLA_HEADS * MLA_V).astype(BF16)
                kg_n, kg_r = _pad_rope_gain(kv_k_norm)
                k_sh, v_sh = _mla_kv(xf, kv_norm, kv_shift, kv_scale, w_dkv_p, kv_lat_norm, w_k, w_v,
                                     kg_n, kg_r, cos4, sin4, batch, seq, tm_mid)
            q_lora = mla_w_dq.shape[2]
            w_uq = mla_w_uq[j].reshape(q_lora, MLA_HEADS, MLA_NOPE + MLA_ROPE)
            w_qn = w_uq[:, :, :MLA_NOPE].reshape(q_lora, MLA_HEADS * MLA_NOPE).astype(BF16)
            r_p, r_s = _split_rope_cols(w_uq[:, :, MLA_NOPE:])
            w_qr = r_p.reshape(q_lora, MLA_HEADS * LANE).astype(BF16)
            w_qs = r_s.reshape(q_lora, MLA_HEADS * LANE).astype(BF16)
            qg_n, qg_r = _pad_rope_gain(mla_q_norm[j])
            q = _mla_q(xf, norm_mix[layer], shift_m, scale_m, mla_w_dq[j].astype(BF16), mla_q_lat_norm[j],
                       w_qn, w_qr, w_qs, qg_n, qg_r, cos4, sin4, batch, seq, tm_mid)
            mix_in = _attention(q, k_sh, v_sh, tm_mid).reshape(t, MLA_HEADS * MLA_V)
            w_out = mla_w_out[j].astype(BF16)
        xf = _matmul_residual(mix_in, w_out, xf, gate_m, seq, tm_mid, d)
        xf = _ffn(xf, norm_ffn[layer], shift_f, scale_f, gate_f, w_gu_bf, w_down_bf, layer, seq, tm_big, 512)
    return xf.reshape(batch, seq, d)
```

```python
import functools

import jax
import jax.numpy as jnp
from jax import lax
from jax.experimental import pallas as pl
from jax.experimental.pallas import tpu as pltpu

GLA_HEADS = 4
GLA_GATE_RANK = 16
GLA_TAU = 16.0
GLA_CHUNK = 64
MLA_HEADS = 16
MLA_NOPE = 128
MLA_ROPE = 64
MLA_V = 128
KV_LORA = 512
ROPE_THETA = 10000.0
EPS = 1e-6
LOG2_E = 1.4426950408889634

LANE = 128
MIB = 1 << 20

F32 = jnp.float32
BF16 = jnp.bfloat16


def _params(semantics, vmem_mib):
    return pltpu.CompilerParams(dimension_semantics=semantics, vmem_limit_bytes=vmem_mib * MIB)


def _silu(x):
    return x * jax.nn.sigmoid(x)


def _rms_scale(x):
    return x * lax.rsqrt(jnp.mean(x * x, axis=-1, keepdims=True) + EPS)


ADA_ROWS = 16


def _split_bf16(x):
    hi = x.astype(BF16)
    return hi, (x - hi.astype(F32)).astype(BF16)


def _ada_kernel(c_ref, w_ref, b_ref, o_ref):
    s_hi, s_lo = _split_bf16(_silu(c_ref[...]))
    w_hi, w_lo = _split_bf16(w_ref[...])
    acc = jnp.dot(s_hi, w_lo, preferred_element_type=F32) + jnp.dot(s_lo, w_hi, preferred_element_type=F32)
    o_ref[...] = (acc + jnp.dot(s_hi, w_hi, preferred_element_type=F32)) + b_ref[...]


def _ada_mod(c_pad, w, b, layer, tn=1024):
    rows, d = c_pad.shape
    n = w.shape[2]
    return pl.pallas_call(
        _ada_kernel,
        out_shape=jax.ShapeDtypeStruct((rows, n), F32),
        grid=(n // tn,),
        in_specs=[
            pl.BlockSpec((rows, d), lambda j: (0, 0)),
            pl.BlockSpec((None, d, tn), lambda j: (layer, 0, j)),
            pl.BlockSpec((None, 1, tn), lambda j: (layer, 0, j)),
        ],
        out_specs=pl.BlockSpec((rows, tn), lambda j: (0, j)),
        compiler_params=_params(("parallel",), 48),
        name="ada_mod",
    )(c_pad, w, b[:, None, :])


def _trig_kernel(a_ref, c_ref, s_ref):
    a = a_ref[...]
    c_ref[...] = jnp.cos(a)
    s_ref[...] = jnp.sin(a)


def _trig(ang):
    rows = ang.shape[0]
    tr = min(rows, 512)
    spec = pl.BlockSpec((tr, LANE), lambda i: (i, 0))
    return pl.pallas_call(
        _trig_kernel,
        out_shape=(jax.ShapeDtypeStruct(ang.shape, F32),) * 2,
        grid=(rows // tr,),
        in_specs=[spec],
        out_specs=(spec, spec),
        compiler_params=_params(("parallel",), 16),
        name="rope_trig",
    )(ang)


MOD_ROWS = 16


def _modulate_into(h_ref, x_ref, g_ref, sh_ref, sc_ref):
    tm = x_ref.shape[0]
    step = min(tm, MOD_ROWS)
    gs = g_ref[...] * (1.0 + sc_ref[0])
    shift = sh_ref[0]
    for r in range(tm // step):
        rows = slice(r * step, (r + 1) * step)
        h_ref[rows, :] = (_rms_scale(x_ref[rows, :]) * gs + shift).astype(h_ref.dtype)


def _modmm_kernel(x_ref, g_ref, sh_ref, sc_ref, w_ref, wa_ref, o_ref, a_ref, h_ref):
    @pl.when(pl.program_id(1) == 0)
    def _():
        _modulate_into(h_ref, x_ref, g_ref, sh_ref, sc_ref)
        a_ref[...] = jnp.dot(h_ref[...], wa_ref[...], preferred_element_type=F32).astype(a_ref.dtype)

    o_ref[...] = jnp.dot(h_ref[...], w_ref[...], preferred_element_type=F32).astype(o_ref.dtype)


def _mod_matmul(x, gain, shift, scale, w, layer, n, w_a, seq, tm, tn):
    t, d = x.shape
    per_b = seq // tm
    vec = lambda i, j: (i // per_b, 0, 0)
    return pl.pallas_call(
        _modmm_kernel,
        out_shape=(jax.ShapeDtypeStruct((t, n), BF16), jax.ShapeDtypeStruct((t, LANE), BF16)),
        grid=(t // tm, n // tn),
        in_specs=[
            pl.BlockSpec((tm, d), lambda i, j: (i, 0)),
            pl.BlockSpec((1, d), lambda i, j: (0, 0)),
            pl.BlockSpec((1, 1, d), vec),
            pl.BlockSpec((1, 1, d), vec),
            pl.BlockSpec((None, d, tn), lambda i, j: (layer, 0, j)),
            pl.BlockSpec((d, LANE), lambda i, j: (0, 0)),
        ],
        out_specs=(pl.BlockSpec((tm, tn), lambda i, j: (i, j)),
                   pl.BlockSpec((tm, LANE), lambda i, j: (i, 0))),
        scratch_shapes=[pltpu.VMEM((tm, d), BF16)],
        compiler_params=_params(("parallel", "arbitrary"), 56),
        name="mod_matmul",
    )(x, gain.reshape(1, d), shift[:, None, :], scale[:, None, :], w, w_a)


def _mm_res_kernel(a_ref, w_ref, x_ref, gate_ref, o_ref):
    acc = jnp.dot(a_ref[...], w_ref[...], preferred_element_type=F32)
    o_ref[...] = x_ref[...] + gate_ref[0] * acc


def _matmul_residual(a, w, x, gate, seq, tm, tn):
    t, k = a.shape
    n = w.shape[1]
    per_b = seq // tm
    return pl.pallas_call(
        _mm_res_kernel,
        out_shape=jax.ShapeDtypeStruct((t, n), F32),
        grid=(t // tm, n // tn),
        in_specs=[
            pl.BlockSpec((tm, k), lambda i, j: (i, 0)),
            pl.BlockSpec((k, tn), lambda i, j: (0, j)),
            pl.BlockSpec((tm, tn), lambda i, j: (i, j)),
            pl.BlockSpec((1, 1, tn), lambda i, j: (i // per_b, 0, j)),
        ],
        out_specs=pl.BlockSpec((tm, tn), lambda i, j: (i, j)),
        compiler_params=_params(("parallel", "arbitrary"), 48),
        name="matmul_residual",
    )(a, w, x, gate[:, None, :])


def _ffn_kernel(x_ref, g_ref, sh_ref, sc_ref, gate_ref, wg_ref, wu_ref, wd_ref, o_ref, h_ref):
    f = pl.program_id(1)

    @pl.when(f == 0)
    def _():
        _modulate_into(h_ref, x_ref, g_ref, sh_ref, sc_ref)
        o_ref[...] = jnp.zeros_like(o_ref)

    h = h_ref[...]
    g = jnp.dot(h, wg_ref[...], preferred_element_type=F32)
    u = jnp.dot(h, wu_ref[...], preferred_element_type=F32)
    a = (_silu(g) * u).astype(BF16)
    o_ref[...] += jnp.dot(a, wd_ref[...], preferred_element_type=F32)

    @pl.when(f == pl.num_programs(1) - 1)
    def _():
        o_ref[...] = x_ref[...] + gate_ref[0] * o_ref[...]


def _ffn(x, gain, shift, scale, gate, w_gu, w_down, layer, seq, tm, tf):
    t, d = x.shape
    dff = w_down.shape[1]
    nf = dff // tf
    per_b = seq // tm
    vec = lambda i, f: (i // per_b, 0, 0)
    return pl.pallas_call(
        _ffn_kernel,
        out_shape=jax.ShapeDtypeStruct((t, d), F32),
        grid=(t // tm, nf),
        in_specs=[
            pl.BlockSpec((tm, d), lambda i, f: (i, 0)),
            pl.BlockSpec((1, d), lambda i, f: (0, 0)),
            pl.BlockSpec((1, 1, d), vec),
            pl.BlockSpec((1, 1, d), vec),
            pl.BlockSpec((1, 1, d), vec),
            pl.BlockSpec((None, d, tf), lambda i, f: (layer, 0, f)),
            pl.BlockSpec((None, d, tf), lambda i, f: (layer, 0, f + nf)),
            pl.BlockSpec((None, tf, d), lambda i, f: (layer, f, 0)),
        ],
        out_specs=pl.BlockSpec((tm, d), lambda i, f: (i, 0)),
        scratch_shapes=[pltpu.VMEM((tm, d), BF16)],
        compiler_params=_params(("parallel", "arbitrary"), 60),
        name="ffn",
    )(x, gain.reshape(1, d), shift[:, None, :], scale[:, None, :], gate[:, None, :], w_gu, w_gu, w_down)


GLA_PAIR = 2 * GLA_CHUNK


GLA_BATCH_PER_STEP = 2


def _gla_pair(q_ref, k_ref, v_ref, g_ref, a_ref, wal_ref, bal_ref, on_ref, o_ref, st_ref, bb, dk, dv):
    ch = GLA_CHUNK
    tp = q_ref.shape[1]
    qk = q_ref.shape[2]

    z = jnp.dot(a_ref[bb], wal_ref[...], preferred_element_type=F32) + bal_ref[...]
    la = (jnp.minimum(z, 0.0) - jnp.log(1.0 + jnp.exp(-jnp.abs(z)))) * (LOG2_E / GLA_TAU)

    row = lax.broadcasted_iota(jnp.int32, (tp, tp), 0)
    col = lax.broadcasted_iota(jnp.int32, (tp, tp), 1)
    causal = jnp.logical_and(col <= row, col >= (row // ch) * ch)
    cross = jnp.logical_and(row >= ch, col < ch)
    tri = jnp.where(causal, 1.0, 0.0).astype(BF16)
    first = lax.broadcasted_iota(jnp.int32, (tp, qk), 0) < ch
    first_h = lax.broadcasted_iota(jnp.int32, (tp, dk), 0) < ch

    la_hi = la.astype(BF16)
    la_lo = (la - la_hi.astype(F32)).astype(BF16)
    bcum = jnp.dot(tri, la_hi, preferred_element_type=F32) + jnp.dot(tri, la_lo, preferred_element_type=F32)
    b_last0 = bcum[ch - 1:ch]
    b_last1 = bcum[tp - 1:tp]
    e_q = jnp.exp2(bcum)
    e_ki = jnp.exp2(-bcum)
    e_ks = jnp.exp2(jnp.where(first, b_last0, b_last1) - bcum)
    dec0 = jnp.exp2(b_last0)
    dec1 = jnp.exp2(b_last1)
    dec01 = jnp.exp2(b_last0 + b_last1)
    on = on_ref[...]
    tn_dims = (((0,), (0,)), ((), ()))
    nt_dims = (((1,), (1,)), ((), ()))

    for h in range(GLA_HEADS):
        ks = slice(h * dk, (h + 1) * dk)
        vs = slice(h * dv, (h + 1) * dv)
        q = q_ref[bb, :, ks].astype(F32) * (dk ** -0.5)
        k = k_ref[bb, :, ks].astype(F32)
        v = v_ref[bb, :, vs]
        q_dec = q * e_q[:, ks]
        k_st = k * e_ks[:, ks]
        q_b = q_dec.astype(BF16)
        k_in = (k * e_ki[:, ks]).astype(BF16)
        k0 = jnp.where(first_h, k_st, 0.0).astype(BF16)
        att = lax.dot_general(q_b, k_in, nt_dims, preferred_element_type=F32)
        att_x = lax.dot_general(q_b, k0, nt_dims, preferred_element_type=F32)
        att = jnp.where(causal, att, jnp.where(cross, att_x, 0.0)).astype(BF16)
        s0 = st_ref[bb, h]
        q_s = jnp.where(first_h, q_dec, q_dec * dec0[:, ks]).astype(BF16)
        o = (jnp.dot(att, v, preferred_element_type=F32)
             + jnp.dot(q_s, s0.astype(BF16), preferred_element_type=F32))
        k_u = jnp.where(first_h, k_st * dec1[:, ks], k_st).astype(BF16)
        dec_cols = jnp.tile(jnp.broadcast_to(dec01[:, ks], (LANE, dk)).T, (1, dv // LANE))
        st_ref[bb, h] = dec_cols * s0 + lax.dot_general(k_u, v, tn_dims, preferred_element_type=F32)
        gg = g_ref[bb, :, vs].astype(F32)
        o_ref[bb, :, vs] = ((_rms_scale(o) * on) * _silu(gg)).astype(o_ref.dtype)


def _gla_kernel(q_ref, k_ref, v_ref, g_ref, a_ref, wal_ref, bal_ref, on_ref, o_ref, st_ref, *, dk, dv):
    @pl.when(pl.program_id(1) == 0)
    def _():
        st_ref[...] = jnp.zeros_like(st_ref)

    for bb in range(q_ref.shape[0]):
        _gla_pair(q_ref, k_ref, v_ref, g_ref, a_ref, wal_ref, bal_ref, on_ref, o_ref, st_ref, bb, dk, dv)


def _gla(proj, a_lr, w_alpha_p, b_alpha, onorm, batch, seq):
    dk = w_alpha_p.shape[1] // GLA_HEADS
    dv = onorm.shape[0]
    qk = GLA_HEADS * dk
    vv = GLA_HEADS * dv
    tp = GLA_PAIR
    nb = GLA_BATCH_PER_STEP if batch % GLA_BATCH_PER_STEP == 0 else 1
    return pl.pallas_call(
        functools.partial(_gla_kernel, dk=dk, dv=dv),
        out_shape=jax.ShapeDtypeStruct((batch, seq, vv), BF16),
        grid=(batch // nb, seq // tp),
        in_specs=[
            pl.BlockSpec((nb, tp, qk), lambda b, i: (b, i, 0)),
            pl.BlockSpec((nb, tp, qk), lambda b, i: (b, i, 1)),
            pl.BlockSpec((nb, tp, vv), lambda b, i: (b, i, 2 * qk // vv)),
            pl.BlockSpec((nb, tp, vv), lambda b, i: (b, i, 2 * qk // vv + 1)),
            pl.BlockSpec((nb, tp, LANE), lambda b, i: (b, i, 0)),
            pl.BlockSpec((LANE, qk), lambda b, i: (0, 0)),
            pl.BlockSpec((1, qk), lambda b, i: (0, 0)),
            pl.BlockSpec((1, dv), lambda b, i: (0, 0)),
        ],
        out_specs=pl.BlockSpec((nb, tp, vv), lambda b, i: (b, i, 0)),
        scratch_shapes=[pltpu.VMEM((nb, GLA_HEADS, dk, dv), F32)],
        compiler_params=_params(("parallel", "arbitrary"), 32),
        name="gla",
    )(proj, proj, proj, proj, a_lr, w_alpha_p, b_alpha.reshape(1, qk), onorm.reshape(1, dv))


def _rope_pair(x, x_swapped, cos4, sin4):
    return x * cos4 + x_swapped * sin4


def _head_norm_store(o_ref, h, rows, nope, rope, gain_n, gain_r, bias_r, dim, post_scale):
    ss = jnp.sum(nope * nope + rope * rope, axis=-1, keepdims=True)
    r = lax.rsqrt(ss / dim + EPS)
    if post_scale != 1.0:
        r = r * post_scale
    o_ref[0, h, rows, 0:LANE] = ((nope * r) * gain_n).astype(o_ref.dtype)
    o_ref[0, h, rows, LANE:2 * LANE] = ((rope * r) * gain_r + bias_r).astype(o_ref.dtype)


MLA_SUB = 256


def _sub_tiles(tm):
    step = min(tm, MLA_SUB)
    return [slice(r * step, (r + 1) * step) for r in range(tm // step)]


def _mla_kv_kernel(x_ref, g_ref, sh_ref, sc_ref, wd_ref, ln_ref, wk_ref, wv_ref, kn_ref, kr_ref, kb_ref,
                   cos_ref, sin_ref, k_ref, v_ref, h_ref):
    lora = ln_ref.shape[1]
    gn = kn_ref[...]
    gr = kr_ref[...]
    gb = kb_ref[...]
    _modulate_into(h_ref, x_ref, g_ref, sh_ref, sc_ref)
    for rows in _sub_tiles(x_ref.shape[0]):
        ckv = jnp.dot(h_ref[rows, :], wd_ref[...], preferred_element_type=F32)
        c_lat = (_rms_scale(ckv[:, :lora]) * ln_ref[...]).astype(BF16)
        k_pe = _rope_pair(ckv[:, lora:lora + LANE], ckv[:, lora + LANE:lora + 2 * LANE],
                          cos_ref[rows, :], sin_ref[rows, :])
        kn = jnp.dot(c_lat, wk_ref[...], preferred_element_type=F32)
        vv = jnp.dot(c_lat, wv_ref[...], preferred_element_type=F32)
        for h in range(MLA_HEADS):
            cols = slice(h * LANE, (h + 1) * LANE)
            _head_norm_store(k_ref, h, rows, kn[:, cols], k_pe, gn, gr, gb, MLA_NOPE + MLA_ROPE, 1.0)
            v_ref[0, h, 0, :, rows] = vv[:, cols].T.astype(v_ref.dtype)


def _mla_q_kernel(x_ref, g_ref, sh_ref, sc_ref, wd_ref, ln_ref, wn_ref, wr_ref, ws_ref, qn_ref, qr_ref, qb_ref,
                  cos_ref, sin_ref, q_ref, h_ref):
    gn = qn_ref[...]
    gr = qr_ref[...]
    gb = qb_ref[...]
    sm_scale = (MLA_NOPE + MLA_ROPE) ** -0.5 * LOG2_E
    _modulate_into(h_ref, x_ref, g_ref, sh_ref, sc_ref)
    for rows in _sub_tiles(x_ref.shape[0]):
        cq = jnp.dot(h_ref[rows, :], wd_ref[...], preferred_element_type=F32)
        cq = (_rms_scale(cq) * ln_ref[...]).astype(BF16)
        qn = jnp.dot(cq, wn_ref[...], preferred_element_type=F32)
        qr = jnp.dot(cq, wr_ref[...], preferred_element_type=F32)
        qs = jnp.dot(cq, ws_ref[...], preferred_element_type=F32)
        cos4 = cos_ref[rows, :]
        sin4 = sin_ref[rows, :]
        for h in range(MLA_HEADS):
            cols = slice(h * LANE, (h + 1) * LANE)
            rope = _rope_pair(qr[:, cols], qs[:, cols], cos4, sin4)
            _head_norm_store(q_ref, h, rows, qn[:, cols], rope, gn, gr, gb, MLA_NOPE + MLA_ROPE, sm_scale)


def _const_spec(arr):
    nd = arr.ndim
    return pl.BlockSpec(arr.shape, lambda b, i: (0,) * nd)


def _mla_kv(x, gain, shift, scale, w_dkv_p, lat_norm, w_k, w_v, kg_n, kg_r, kg_b, cos4, sin4, batch, seq, tm):
    t, d = x.shape
    nt = seq // tm
    row = lambda b, i: (b * nt + i, 0)
    vec = lambda b, i: (b, 0, 0)
    g2 = gain.reshape(1, d)
    weights = [w_dkv_p, lat_norm.reshape(1, -1), w_k, w_v, kg_n, kg_r, kg_b]
    hk = pl.BlockSpec((1, MLA_HEADS, tm, 2 * LANE), lambda b, i: (b, 0, i, 0))
    hv = pl.BlockSpec((1, MLA_HEADS, 1, MLA_V, tm), lambda b, i: (b, 0, i, 0, 0))
    return pl.pallas_call(
        _mla_kv_kernel,
        out_shape=(jax.ShapeDtypeStruct((batch, MLA_HEADS, seq, 2 * LANE), BF16),
                   jax.ShapeDtypeStruct((batch, MLA_HEADS, nt, MLA_V, tm), BF16)),
        grid=(batch, nt),
        in_specs=[pl.BlockSpec((tm, d), row), _const_spec(g2),
                  pl.BlockSpec((1, 1, d), vec), pl.BlockSpec((1, 1, d), vec)]
                 + [_const_spec(w) for w in weights]
                 + [pl.BlockSpec((tm, LANE), row), pl.BlockSpec((tm, LANE), row)],
        out_specs=(hk, hv),
        scratch_shapes=[pltpu.VMEM((tm, d), BF16)],
        compiler_params=_params(("parallel", "parallel"), 48),
        name="mla_kv",
    )(x, g2, shift[:, None, :], scale[:, None, :], *weights, cos4, sin4)


def _mla_q(x, gain, shift, scale, w_dq, q_lat_norm, w_qn, w_qr, w_qs, qg_n, qg_r, qg_b, cos4, sin4, batch, seq, tm):
    t, d = x.shape
    nt = seq // tm
    row = lambda b, i: (b * nt + i, 0)
    vec = lambda b, i: (b, 0, 0)
    g2 = gain.reshape(1, d)
    weights = [w_dq, q_lat_norm.reshape(1, -1), w_qn, w_qr, w_qs, qg_n, qg_r, qg_b]
    hq = pl.BlockSpec((1, MLA_HEADS, tm, 2 * LANE), lambda b, i: (b, 0, i, 0))
    return pl.pallas_call(
        _mla_q_kernel,
        out_shape=jax.ShapeDtypeStruct((batch, MLA_HEADS, seq, 2 * LANE), BF16),
        grid=(batch, nt),
        in_specs=[pl.BlockSpec((tm, d), row), _const_spec(g2),
                  pl.BlockSpec((1, 1, d), vec), pl.BlockSpec((1, 1, d), vec)]
                 + [_const_spec(w) for w in weights]
                 + [pl.BlockSpec((tm, LANE), row), pl.BlockSpec((tm, LANE), row)],
        out_specs=hq,
        scratch_shapes=[pltpu.VMEM((tm, d), BF16)],
        compiler_params=_params(("parallel", "parallel"), 48),
        name="mla_q",
    )(x, g2, shift[:, None, :], scale[:, None, :], *weights, cos4, sin4)


ATTN_HEADS_PER_STEP = 4
ATTN_MAX_SCORE_BOUND = 60.0


def _attn_kernel(q_ref, k_ref, vt_ref, o_ref, *, tq, running_max):
    qi = pl.program_id(2)
    nh = q_ref.shape[1]
    tc = vt_ref.shape[-1]
    dv = vt_ref.shape[-2]
    nt_dims = (((1,), (1,)), ((), ()))

    def step(blk0, nblk, carry, masked):
        tk = nblk * tq
        scores = []
        for h in range(nh):
            k = k_ref[0, h, pl.ds(pl.multiple_of(blk0 * tq, tq), tk), :]
            scores.append(lax.dot_general(k, q_ref[0, h], nt_dims, preferred_element_type=F32))
        out = []
        for h in range(nh):
            st = scores[h]
            if masked:
                kv_i = lax.broadcasted_iota(jnp.int32, (tk, tq), 0)
                q_i = lax.broadcasted_iota(jnp.int32, (tk, tq), 1)
                st = jnp.where(kv_i - (nblk - 1) * tq <= q_i, st, -jnp.inf)
            if running_max:
                m, l, acc = carry[h]
                m_new = jnp.maximum(m, jnp.max(st, axis=0, keepdims=True))
                alpha = jnp.exp2(m - m_new)
                pt = jnp.exp2(st - m_new)
                l = alpha * l
                acc = alpha * acc
            else:
                l, acc = carry[h]
                pt = jnp.exp2(st)
            l = l + jnp.sum(pt, axis=0, keepdims=True)
            pt = pt.astype(BF16)
            slab0 = blk0 * (tq // tc)
            pv = jnp.dot(vt_ref[0, h, slab0], pt[0:tc], preferred_element_type=F32)
            for c in range(1, tk // tc):
                pv += jnp.dot(vt_ref[0, h, slab0 + c], pt[c * tc:(c + 1) * tc], preferred_element_type=F32)
            out.append((m_new, l, acc + pv) if running_max else (l, acc + pv))
        return tuple(out)

    stats = (jnp.zeros((1, tq), F32), jnp.zeros((dv, tq), F32))
    if running_max:
        stats = (jnp.full((1, tq), -jnp.inf, F32),) + stats
    init = tuple(stats for _ in range(nh))
    carry = lax.fori_loop(0, lax.shift_right_logical(qi, 1), lambda j, c: step(2 * j, 2, c, False), init)
    carry = lax.cond(jnp.bitwise_and(qi, 1) == 1,
                     lambda c: step(qi - 1, 2, c, True),
                     lambda c: step(qi, 1, c, True), carry)
    for h in range(nh):
        l, acc = carry[h][-2:]
        o_ref[0, :, h * dv:(h + 1) * dv] = (acc / l).T.astype(o_ref.dtype)


def _attention(q, k, vt, tq, running_max):
    b, h, s, dq = q.shape
    _, _, nslab, dvv, tc = vt.shape
    nh = ATTN_HEADS_PER_STEP
    return pl.pallas_call(
        functools.partial(_attn_kernel, tq=tq, running_max=running_max),
        out_shape=jax.ShapeDtypeStruct((b, s, h * dvv), BF16),
        grid=(b, h // nh, s // tq),
        in_specs=[
            pl.BlockSpec((1, nh, tq, dq), lambda bi, hi, qi: (bi, hi, qi, 0)),
            pl.BlockSpec((1, nh, s, dq), lambda bi, hi, qi: (bi, hi, 0, 0)),
            pl.BlockSpec((1, nh, nslab, dvv, tc), lambda bi, hi, qi: (bi, hi, 0, 0, 0)),
        ],
        out_specs=pl.BlockSpec((1, tq, nh * dvv), lambda bi, hi, qi: (bi, qi, hi)),
        compiler_params=_params(("parallel", "parallel", "arbitrary"), 48),
        name="mla_attention" if running_max else "mla_attention_bounded",
    )(q, k, vt)


def _rope_tables(positions):
    b, s = positions.shape
    half = MLA_ROPE // 2
    inv_freq = ROPE_THETA ** (-jnp.arange(half, dtype=F32) / half)
    ang = positions.astype(F32)[..., None] * inv_freq
    cos, sin = _trig(ang.reshape(-1, LANE))
    cos = cos.reshape(b * s, half)
    sin = sin.reshape(b * s, half)
    zero = jnp.zeros((b * s, LANE - 2 * half), F32)
    cos4 = jnp.concatenate([cos, cos, zero], axis=-1)
    sin4 = jnp.concatenate([-sin, sin, zero], axis=-1)
    return cos4, sin4


def _split_rope_cols(w_rope):
    half = MLA_ROPE // 2
    x1, x2 = w_rope[..., :half], w_rope[..., half:]
    zero = jnp.zeros(w_rope.shape[:-1] + (LANE - MLA_ROPE,), w_rope.dtype)
    return jnp.concatenate([x1, x2, zero], axis=-1), jnp.concatenate([x2, x1, zero], axis=-1)


def _pad_rope_gain(gain):
    g_n = gain[:MLA_NOPE].reshape(1, MLA_NOPE)
    g_r = jnp.concatenate([gain[MLA_NOPE:], jnp.zeros((LANE - MLA_ROPE,), gain.dtype)]).reshape(1, LANE)
    return g_n, g_r


def kernel(x, c, positions, ada_w, ada_b, norm_mix, norm_ffn, gla_w_in, gla_w_alpha, gla_b_alpha, gla_onorm,
           gla_w_out, mla_w_dq, mla_q_lat_norm, mla_w_uq, mla_q_norm, mla_w_out, kv_ada_w, kv_ada_b, kv_norm,
           kv_w_dkv, kv_lat_norm, kv_w_ukv, kv_k_norm, ffn_w_gu, ffn_w_down):
    batch, seq, d = x.shape
    t = batch * seq
    depth = ada_w.shape[0]
    n_gla = gla_w_in.shape[0]
    xf = x.reshape(t, d)

    tm_big = min(seq, 1024)
    tm_mid = min(seq, 512)

    c_pad = jnp.pad(c, ((0, ADA_ROWS - batch), (0, 0)))
    cos4, sin4 = _rope_tables(positions)
    w_gu_bf = ffn_w_gu.astype(BF16)
    w_down_bf = ffn_w_down.astype(BF16)

    k_sh = v_sh = None
    for layer in range(depth):
        mod = _ada_mod(c_pad, ada_w, ada_b, layer)[:batch]
        shift_m, scale_m, gate_m, shift_f, scale_f, gate_f = jnp.split(mod, 6, axis=-1)
        if layer < n_gla:
            qkvg = gla_w_in.shape[2] - GLA_GATE_RANK
            w_a = jnp.pad(gla_w_in[layer, :, qkvg:], ((0, 0), (0, LANE - GLA_GATE_RANK))).astype(BF16)
            proj, a_lr = _mod_matmul(xf, norm_mix[layer], shift_m, scale_m, gla_w_in.astype(BF16), layer, qkvg,
                                     w_a, seq, tm_big, 1024)
            w_alpha_p = jnp.pad(gla_w_alpha[layer], ((0, LANE - GLA_GATE_RANK), (0, 0))).astype(BF16)
            mix_in = _gla(proj.reshape(batch, seq, qkvg), a_lr.reshape(batch, seq, LANE), w_alpha_p,
                          gla_b_alpha[layer], gla_onorm[layer], batch, seq).reshape(t, -1)
            w_out = gla_w_out[layer].astype(BF16)
        else:
            j = layer - n_gla
            if j == 0:
                kv_mod = _ada_mod(c_pad, kv_ada_w[None], kv_ada_b[None], 0)[:batch]
                kv_shift, kv_scale = jnp.split(kv_mod, 2, axis=-1)
                pe_p, pe_s = _split_rope_cols(kv_w_dkv[:, KV_LORA:])
                w_dkv_p = jnp.concatenate([kv_w_dkv[:, :KV_LORA], pe_p, pe_s], axis=1).astype(BF16)
                w_ukv = kv_w_ukv.reshape(KV_LORA, MLA_HEADS, MLA_NOPE + MLA_V)
                w_k = w_ukv[:, :, :MLA_NOPE].reshape(KV_LORA, MLA_HEADS * MLA_NOPE).astype(BF16)
                w_v = w_ukv[:, :, MLA_NOPE:].reshape(KV_LORA, MLA_HEADS * MLA_V).astype(BF16)
                kg_n, kg_r = _pad_rope_gain(kv_k_norm)
                score_bound = ((MLA_NOPE + MLA_ROPE) ** 0.5 * LOG2_E
                               * jnp.max(jnp.abs(mla_q_norm)) * jnp.max(jnp.abs(kv_k_norm)))
                spare = (jnp.arange(LANE) == MLA_ROPE).astype(F32).reshape(1, LANE)
                k_sh, v_sh = _mla_kv(xf, kv_norm, kv_shift, kv_scale, w_dkv_p, kv_lat_norm, w_k, w_v,
                                     kg_n, kg_r, -score_bound * spare, cos4, sin4, batch, seq, tm_mid)
            q_lora = mla_w_dq.shape[2]
            w_uq = mla_w_uq[j].reshape(q_lora, MLA_HEADS, MLA_NOPE + MLA_ROPE)
            w_qn = w_uq[:, :, :MLA_NOPE].reshape(q_lora, MLA_HEADS * MLA_NOPE).astype(BF16)
            r_p, r_s = _split_rope_cols(w_uq[:, :, MLA_NOPE:])
            w_qr = r_p.reshape(q_lora, MLA_HEADS * LANE).astype(BF16)
            w_qs = r_s.reshape(q_lora, MLA_HEADS * LANE).astype(BF16)
            qg_n, qg_r = _pad_rope_gain(mla_q_norm[j])
            q = _mla_q(xf, norm_mix[layer], shift_m, scale_m, mla_w_dq[j].astype(BF16), mla_q_lat_norm[j],
                       w_qn, w_qr, w_qs, qg_n, qg_r, spare, cos4, sin4, batch, seq, tm_mid)
            mix_in = lax.cond(score_bound <= ATTN_MAX_SCORE_BOUND,
                              lambda: _attention(q, k_sh, v_sh, tm_mid, False),
                              lambda: _attention(q, k_sh, v_sh, tm_mid, True))
            mix_in = mix_in.reshape(t, MLA_HEADS * MLA_V)
            w_out = mla_w_out[j].astype(BF16)
        xf = _matmul_residual(mix_in, w_out, xf, gate_m, seq, tm_mid, d)
        xf = _ffn(xf, norm_ffn[layer], shift_f, scale_f, gate_f, w_gu_bf, w_down_bf, layer, seq, tm_big, 512)
    return xf.reshape(batch, seq, d)
```

```python
import functools

import jax
import jax.numpy as jnp
from jax import lax
from jax.experimental import pallas as pl
from jax.experimental.pallas import tpu as pltpu

GLA_HEADS = 4
GLA_GATE_RANK = 16
GLA_TAU = 16.0
GLA_CHUNK = 64
MLA_HEADS = 16
MLA_NOPE = 128
MLA_ROPE = 64
MLA_V = 128
KV_LORA = 512
ROPE_THETA = 10000.0
EPS = 1e-6
LOG2_E = 1.4426950408889634

LANE = 128
MIB = 1 << 20

F32 = jnp.float32
BF16 = jnp.bfloat16


def _params(semantics, vmem_mib):
    return pltpu.CompilerParams(dimension_semantics=semantics, vmem_limit_bytes=vmem_mib * MIB)


def _silu(x):
    return x * jax.nn.sigmoid(x)


def _rms_scale(x):
    return x * lax.rsqrt(jnp.mean(x * x, axis=-1, keepdims=True) + EPS)


ADA_ROWS = 16


def _split_bf16(x):
    hi = x.astype(BF16)
    return hi, (x - hi.astype(F32)).astype(BF16)


def _ada_kernel(c_ref, w_ref, b_ref, o_ref):
    s_hi, s_lo = _split_bf16(_silu(c_ref[...]))
    w_hi, w_lo = _split_bf16(w_ref[...])
    acc = jnp.dot(s_hi, w_lo, preferred_element_type=F32) + jnp.dot(s_lo, w_hi, preferred_element_type=F32)
    o_ref[...] = (acc + jnp.dot(s_hi, w_hi, preferred_element_type=F32)) + b_ref[...]


def _ada_mod(c_pad, w, b, layer, tn=1024):
    rows, d = c_pad.shape
    n = w.shape[2]
    return pl.pallas_call(
        _ada_kernel,
        out_shape=jax.ShapeDtypeStruct((rows, n), F32),
        grid=(n // tn,),
        in_specs=[
            pl.BlockSpec((rows, d), lambda j: (0, 0)),
            pl.BlockSpec((None, d, tn), lambda j: (layer, 0, j)),
            pl.BlockSpec((None, 1, tn), lambda j: (layer, 0, j)),
        ],
        out_specs=pl.BlockSpec((rows, tn), lambda j: (0, j)),
        compiler_params=_params(("parallel",), 48),
        name="ada_mod",
    )(c_pad, w, b[:, None, :])


def _trig_kernel(a_ref, c_ref, s_ref):
    a = a_ref[...]
    c_ref[...] = jnp.cos(a)
    s_ref[...] = jnp.sin(a)


def _trig(ang):
    rows = ang.shape[0]
    tr = min(rows, 512)
    spec = pl.BlockSpec((tr, LANE), lambda i: (i, 0))
    return pl.pallas_call(
        _trig_kernel,
        out_shape=(jax.ShapeDtypeStruct(ang.shape, F32),) * 2,
        grid=(rows // tr,),
        in_specs=[spec],
        out_specs=(spec, spec),
        compiler_params=_params(("parallel",), 16),
        name="rope_trig",
    )(ang)


MOD_ROWS = 16


def _modulate_into(h_ref, x_ref, g_ref, sh_ref, sc_ref):
    tm = x_ref.shape[0]
    step = min(tm, MOD_ROWS)
    gs = g_ref[...] * (1.0 + sc_ref[0])
    shift = sh_ref[0]
    for r in range(tm // step):
        rows = slice(r * step, (r + 1) * step)
        h_ref[rows, :] = (_rms_scale(x_ref[rows, :]) * gs + shift).astype(h_ref.dtype)


def _modmm_kernel(x_ref, g_ref, sh_ref, sc_ref, w_ref, wa_ref, o_ref, a_ref, h_ref):
    @pl.when(pl.program_id(1) == 0)
    def _():
        _modulate_into(h_ref, x_ref, g_ref, sh_ref, sc_ref)
        a_ref[...] = jnp.dot(h_ref[...], wa_ref[...], preferred_element_type=F32).astype(a_ref.dtype)

    o_ref[...] = jnp.dot(h_ref[...], w_ref[...], preferred_element_type=F32).astype(o_ref.dtype)


def _mod_matmul(x, gain, shift, scale, w, layer, n, w_a, seq, tm, tn):
    t, d = x.shape
    per_b = seq // tm
    vec = lambda i, j: (i // per_b, 0, 0)
    return pl.pallas_call(
        _modmm_kernel,
        out_shape=(jax.ShapeDtypeStruct((t, n), BF16), jax.ShapeDtypeStruct((t, LANE), BF16)),
        grid=(t // tm, n // tn),
        in_specs=[
            pl.BlockSpec((tm, d), lambda i, j: (i, 0)),
            pl.BlockSpec((1, d), lambda i, j: (0, 0)),
            pl.BlockSpec((1, 1, d), vec),
            pl.BlockSpec((1, 1, d), vec),
            pl.BlockSpec((None, d, tn), lambda i, j: (layer, 0, j)),
            pl.BlockSpec((d, LANE), lambda i, j: (0, 0)),
        ],
        out_specs=(pl.BlockSpec((tm, tn), lambda i, j: (i, j)),
                   pl.BlockSpec((tm, LANE), lambda i, j: (i, 0))),
        scratch_shapes=[pltpu.VMEM((tm, d), BF16)],
        compiler_params=_params(("parallel", "arbitrary"), 56),
        name="mod_matmul",
    )(x, gain.reshape(1, d), shift[:, None, :], scale[:, None, :], w, w_a)


def _mm_res_kernel(a_ref, w_ref, x_ref, gate_ref, o_ref):
    acc = jnp.dot(a_ref[...], w_ref[...], preferred_element_type=F32)
    o_ref[...] = x_ref[...] + gate_ref[0] * acc


def _matmul_residual(a, w, x, gate, seq, tm, tn):
    t, k = a.shape
    n = w.shape[1]
    per_b = seq // tm
    return pl.pallas_call(
        _mm_res_kernel,
        out_shape=jax.ShapeDtypeStruct((t, n), F32),
        grid=(t // tm, n // tn),
        in_specs=[
            pl.BlockSpec((tm, k), lambda i, j: (i, 0)),
            pl.BlockSpec((k, tn), lambda i, j: (0, j)),
            pl.BlockSpec((tm, tn), lambda i, j: (i, j)),
            pl.BlockSpec((1, 1, tn), lambda i, j: (i // per_b, 0, j)),
        ],
        out_specs=pl.BlockSpec((tm, tn), lambda i, j: (i, j)),
        compiler_params=_params(("parallel", "arbitrary"), 48),
        name="matmul_residual",
    )(a, w, x, gate[:, None, :])


def _ffn_kernel(x_ref, g_ref, sh_ref, sc_ref, gate_ref, wg_ref, wu_ref, wd_ref, o_ref, h_ref):
    f = pl.program_id(1)

    @pl.when(f == 0)
    def _():
        _modulate_into(h_ref, x_ref, g_ref, sh_ref, sc_ref)
        o_ref[...] = jnp.zeros_like(o_ref)

    h = h_ref[...]
    g = jnp.dot(h, wg_ref[...], preferred_element_type=F32)
    u = jnp.dot(h, wu_ref[...], preferred_element_type=F32)
    a = (_silu(g) * u).astype(BF16)
    o_ref[...] += jnp.dot(a, wd_ref[...], preferred_element_type=F32)

    @pl.when(f == pl.num_programs(1) - 1)
    def _():
        o_ref[...] = x_ref[...] + gate_ref[0] * o_ref[...]


def _ffn(x, gain, shift, scale, gate, w_gu, w_down, layer, seq, tm, tf):
    t, d = x.shape
    dff = w_down.shape[1]
    nf = dff // tf
    per_b = seq // tm
    vec = lambda i, f: (i // per_b, 0, 0)
    return pl.pallas_call(
        _ffn_kernel,
        out_shape=jax.ShapeDtypeStruct((t, d), F32),
        grid=(t // tm, nf),
        in_specs=[
            pl.BlockSpec((tm, d), lambda i, f: (i, 0)),
            pl.BlockSpec((1, d), lambda i, f: (0, 0)),
            pl.BlockSpec((1, 1, d), vec),
            pl.BlockSpec((1, 1, d), vec),
            pl.BlockSpec((1, 1, d), vec),
            pl.BlockSpec((None, d, tf), lambda i, f: (layer, 0, f)),
            pl.BlockSpec((None, d, tf), lambda i, f: (layer, 0, f + nf)),
            pl.BlockSpec((None, tf, d), lambda i, f: (layer, f, 0)),
        ],
        out_specs=pl.BlockSpec((tm, d), lambda i, f: (i, 0)),
        scratch_shapes=[pltpu.VMEM((tm, d), BF16)],
        compiler_params=_params(("parallel", "arbitrary"), 60),
        name="ffn",
    )(x, gain.reshape(1, d), shift[:, None, :], scale[:, None, :], gate[:, None, :], w_gu, w_gu, w_down)


GLA_PAIR = 2 * GLA_CHUNK


GLA_BATCH_PER_STEP = 2


def _gla_pair(q_ref, k_ref, v_ref, g_ref, a_ref, wal_ref, bal_ref, on_ref, o_ref, st_ref, bb, dk, dv):
    ch = GLA_CHUNK
    tp = q_ref.shape[1]
    qk = q_ref.shape[2]

    z = jnp.dot(a_ref[bb], wal_ref[...], preferred_element_type=F32) + bal_ref[...]
    la = (jnp.minimum(z, 0.0) - jnp.log(1.0 + jnp.exp(-jnp.abs(z)))) * (LOG2_E / GLA_TAU)

    row = lax.broadcasted_iota(jnp.int32, (tp, tp), 0)
    col = lax.broadcasted_iota(jnp.int32, (tp, tp), 1)
    causal = jnp.logical_and(col <= row, col >= (row // ch) * ch)
    cross = jnp.logical_and(row >= ch, col < ch)
    tri = jnp.where(causal, 1.0, 0.0).astype(BF16)
    first = lax.broadcasted_iota(jnp.int32, (tp, qk), 0) < ch
    first_h = lax.broadcasted_iota(jnp.int32, (tp, dk), 0) < ch

    la_hi = la.astype(BF16)
    la_lo = (la - la_hi.astype(F32)).astype(BF16)
    bcum = jnp.dot(tri, la_hi, preferred_element_type=F32) + jnp.dot(tri, la_lo, preferred_element_type=F32)
    b_last0 = bcum[ch - 1:ch]
    b_last1 = bcum[tp - 1:tp]
    e_q = jnp.exp2(bcum)
    e_ki = jnp.exp2(-bcum)
    e_ks = jnp.exp2(jnp.where(first, b_last0, b_last1) - bcum)
    dec0 = jnp.exp2(b_last0)
    dec1 = jnp.exp2(b_last1)
    dec01 = jnp.exp2(b_last0 + b_last1)
    on = on_ref[...]
    tn_dims = (((0,), (0,)), ((), ()))
    nt_dims = (((1,), (1,)), ((), ()))

    for h in range(GLA_HEADS):
        ks = slice(h * dk, (h + 1) * dk)
        vs = slice(h * dv, (h + 1) * dv)
        q = q_ref[bb, :, ks].astype(F32) * (dk ** -0.5)
        k = k_ref[bb, :, ks].astype(F32)
        v = v_ref[bb, :, vs]
        q_dec = q * e_q[:, ks]
        k_st = k * e_ks[:, ks]
        q_b = q_dec.astype(BF16)
        k_in = (k * e_ki[:, ks]).astype(BF16)
        k0 = jnp.where(first_h, k_st, 0.0).astype(BF16)
        att = lax.dot_general(q_b, k_in, nt_dims, preferred_element_type=F32)
        att_x = lax.dot_general(q_b, k0, nt_dims, preferred_element_type=F32)
        att = jnp.where(causal, att, jnp.where(cross, att_x, 0.0)).astype(BF16)
        s0 = st_ref[bb, h]
        q_s = jnp.where(first_h, q_dec, q_dec * dec0[:, ks]).astype(BF16)
        o = (jnp.dot(att, v, preferred_element_type=F32)
             + jnp.dot(q_s, s0.astype(BF16), preferred_element_type=F32))
        k_u = jnp.where(first_h, k_st * dec1[:, ks], k_st).astype(BF16)
        dec_cols = jnp.tile(jnp.broadcast_to(dec01[:, ks], (LANE, dk)).T, (1, dv // LANE))
        st_ref[bb, h] = dec_cols * s0 + lax.dot_general(k_u, v, tn_dims, preferred_element_type=F32)
        gg = g_ref[bb, :, vs].astype(F32)
        o_ref[bb, :, vs] = ((_rms_scale(o) * on) * _silu(gg)).astype(o_ref.dtype)


def _gla_kernel(q_ref, k_ref, v_ref, g_ref, a_ref, wal_ref, bal_ref, on_ref, o_ref, st_ref, *, dk, dv):
    @pl.when(pl.program_id(1) == 0)
    def _():
        st_ref[...] = jnp.zeros_like(st_ref)

    for bb in range(q_ref.shape[0]):
        _gla_pair(q_ref, k_ref, v_ref, g_ref, a_ref, wal_ref, bal_ref, on_ref, o_ref, st_ref, bb, dk, dv)


def _gla(proj, a_lr, w_alpha_p, b_alpha, onorm, batch, seq):
    dk = w_alpha_p.shape[1] // GLA_HEADS
    dv = onorm.shape[0]
    qk = GLA_HEADS * dk
    vv = GLA_HEADS * dv
    tp = GLA_PAIR
    nb = GLA_BATCH_PER_STEP if batch % GLA_BATCH_PER_STEP == 0 else 1
    return pl.pallas_call(
        functools.partial(_gla_kernel, dk=dk, dv=dv),
        out_shape=jax.ShapeDtypeStruct((batch, seq, vv), BF16),
        grid=(batch // nb, seq // tp),
        in_specs=[
            pl.BlockSpec((nb, tp, qk), lambda b, i: (b, i, 0)),
            pl.BlockSpec((nb, tp, qk), lambda b, i: (b, i, 1)),
            pl.BlockSpec((nb, tp, vv), lambda b, i: (b, i, 2 * qk // vv)),
            pl.BlockSpec((nb, tp, vv), lambda b, i: (b, i, 2 * qk // vv + 1)),
            pl.BlockSpec((nb, tp, LANE), lambda b, i: (b, i, 0)),
            pl.BlockSpec((LANE, qk), lambda b, i: (0, 0)),
            pl.BlockSpec((1, qk), lambda b, i: (0, 0)),
            pl.BlockSpec((1, dv), lambda b, i: (0, 0)),
        ],
        out_specs=pl.BlockSpec((nb, tp, vv), lambda b, i: (b, i, 0)),
        scratch_shapes=[pltpu.VMEM((nb, GLA_HEADS, dk, dv), F32)],
        compiler_params=_params(("parallel", "arbitrary"), 32),
        name="gla",
    )(proj, proj, proj, proj, a_lr, w_alpha_p, b_alpha.reshape(1, qk), onorm.reshape(1, dv))


def _rope_pair(x, x_swapped, cos4, sin4):
    return x * cos4 + x_swapped * sin4


def _head_norm_store(o_ref, h, rows, nope, rope_sq, rope_g, gain_n, dim, post_scale):
    ss = jnp.sum(nope * nope + rope_sq, axis=-1, keepdims=True)
    r = lax.rsqrt(ss / dim + EPS)
    if post_scale != 1.0:
        r = r * post_scale
    o_ref[0, h, rows, 0:LANE] = ((nope * r) * gain_n).astype(o_ref.dtype)
    o_ref[0, h, rows, LANE:2 * LANE] = (rope_g * r).astype(o_ref.dtype)


MLA_SUB = 256


def _sub_tiles(tm):
    step = min(tm, MLA_SUB)
    return [slice(r * step, (r + 1) * step) for r in range(tm // step)]


def _mla_kv_kernel(x_ref, g_ref, sh_ref, sc_ref, wd_ref, ln_ref, wk_ref, wv_ref, kn_ref, kr_ref,
                   cos_ref, sin_ref, k_ref, v_ref, h_ref):
    lora = ln_ref.shape[1]
    gn = kn_ref[...]
    gr = kr_ref[...]
    _modulate_into(h_ref, x_ref, g_ref, sh_ref, sc_ref)
    for rows in _sub_tiles(x_ref.shape[0]):
        ckv = jnp.dot(h_ref[rows, :], wd_ref[...], preferred_element_type=F32)
        c_lat = (_rms_scale(ckv[:, :lora]) * ln_ref[...]).astype(BF16)
        k_pe = _rope_pair(ckv[:, lora:lora + LANE], ckv[:, lora + LANE:lora + 2 * LANE],
                          cos_ref[rows, :], sin_ref[rows, :])
        kn = jnp.dot(c_lat, wk_ref[...], preferred_element_type=F32)
        vv = jnp.dot(c_lat, wv_ref[...], preferred_element_type=F32)
        pe_sq = k_pe * k_pe
        pe_g = k_pe * gr
        for h in range(MLA_HEADS):
            cols = slice(h * LANE, (h + 1) * LANE)
            _head_norm_store(k_ref, h, rows, kn[:, cols], pe_sq, pe_g, gn, MLA_NOPE + MLA_ROPE, 1.0)
            v_ref[0, h, 0, :, rows] = vv[:, cols].T.astype(v_ref.dtype)


def _mla_q_kernel(x_ref, g_ref, sh_ref, sc_ref, wd_ref, ln_ref, wn_ref, wr_ref, ws_ref, qn_ref, qr_ref,
                  cos_ref, sin_ref, q_ref, h_ref):
    gn = qn_ref[...]
    gr = qr_ref[...]
    sm_scale = (MLA_NOPE + MLA_ROPE) ** -0.5 * LOG2_E
    _modulate_into(h_ref, x_ref, g_ref, sh_ref, sc_ref)
    for rows in _sub_tiles(x_ref.shape[0]):
        cq = jnp.dot(h_ref[rows, :], wd_ref[...], preferred_element_type=F32)
        cq = (_rms_scale(cq) * ln_ref[...]).astype(BF16)
        qn = jnp.dot(cq, wn_ref[...], preferred_element_type=F32)
        qr = jnp.dot(cq, wr_ref[...], preferred_element_type=F32)
        qs = jnp.dot(cq, ws_ref[...], preferred_element_type=F32)
        cos4 = cos_ref[rows, :]
        sin4 = sin_ref[rows, :]
        for h in range(MLA_HEADS):
            cols = slice(h * LANE, (h + 1) * LANE)
            rope = _rope_pair(qr[:, cols], qs[:, cols], cos4, sin4)
            _head_norm_store(q_ref, h, rows, qn[:, cols], rope * rope, rope * gr, gn, MLA_NOPE + MLA_ROPE, sm_scale)


def _const_spec(arr):
    nd = arr.ndim
    return pl.BlockSpec(arr.shape, lambda b, i: (0,) * nd)


def _mla_kv(x, gain, shift, scale, w_dkv_p, lat_norm, w_k, w_v, kg_n, kg_r, cos4, sin4, batch, seq, tm):
    t, d = x.shape
    nt = seq // tm
    row = lambda b, i: (b * nt + i, 0)
    vec = lambda b, i: (b, 0, 0)
    g2 = gain.reshape(1, d)
    weights = [w_dkv_p, lat_norm.reshape(1, -1), w_k, w_v, kg_n, kg_r]
    hk = pl.BlockSpec((1, MLA_HEADS, tm, 2 * LANE), lambda b, i: (b, 0, i, 0))
    hv = pl.BlockSpec((1, MLA_HEADS, 1, MLA_V, tm), lambda b, i: (b, 0, i, 0, 0))
    return pl.pallas_call(
        _mla_kv_kernel,
        out_shape=(jax.ShapeDtypeStruct((batch, MLA_HEADS, seq, 2 * LANE), BF16),
                   jax.ShapeDtypeStruct((batch, MLA_HEADS, nt, MLA_V, tm), BF16)),
        grid=(batch, nt),
        in_specs=[pl.BlockSpec((tm, d), row), _const_spec(g2),
                  pl.BlockSpec((1, 1, d), vec), pl.BlockSpec((1, 1, d), vec)]
                 + [_const_spec(w) for w in weights]
                 + [pl.BlockSpec((tm, LANE), row), pl.BlockSpec((tm, LANE), row)],
        out_specs=(hk, hv),
        scratch_shapes=[pltpu.VMEM((tm, d), BF16)],
        compiler_params=_params(("parallel", "parallel"), 48),
        name="mla_kv",
    )(x, g2, shift[:, None, :], scale[:, None, :], *weights, cos4, sin4)


def _mla_q(x, gain, shift, scale, w_dq, q_lat_norm, w_qn, w_qr, w_qs, qg_n, qg_r, cos4, sin4, batch, seq, tm):
    t, d = x.shape
    nt = seq // tm
    row = lambda b, i: (b * nt + i, 0)
    vec = lambda b, i: (b, 0, 0)
    g2 = gain.reshape(1, d)
    weights = [w_dq, q_lat_norm.reshape(1, -1), w_qn, w_qr, w_qs, qg_n, qg_r]
    hq = pl.BlockSpec((1, MLA_HEADS, tm, 2 * LANE), lambda b, i: (b, 0, i, 0))
    return pl.pallas_call(
        _mla_q_kernel,
        out_shape=jax.ShapeDtypeStruct((batch, MLA_HEADS, seq, 2 * LANE), BF16),
        grid=(batch, nt),
        in_specs=[pl.BlockSpec((tm, d), row), _const_spec(g2),
                  pl.BlockSpec((1, 1, d), vec), pl.BlockSpec((1, 1, d), vec)]
                 + [_const_spec(w) for w in weights]
                 + [pl.BlockSpec((tm, LANE), row), pl.BlockSpec((tm, LANE), row)],
        out_specs=hq,
        scratch_shapes=[pltpu.VMEM((tm, d), BF16)],
        compiler_params=_params(("parallel", "parallel"), 48),
        name="mla_q",
    )(x, g2, shift[:, None, :], scale[:, None, :], *weights, cos4, sin4)


ATTN_HEADS_PER_STEP = 4
ATTN_MAX_SCORE_BOUND = 60.0


def _attn_kernel(q_ref, k_ref, vt_ref, o_ref, *, tq, running_max):
    qi = pl.program_id(2)
    nh = q_ref.shape[1]
    tc = vt_ref.shape[-1]
    dv = vt_ref.shape[-2]
    nt_dims = (((1,), (1,)), ((), ()))

    def step(blk0, nblk, carry, masked):
        tk = nblk * tq
        scores = []
        for h in range(nh):
            k = k_ref[0, h, pl.ds(pl.multiple_of(blk0 * tq, tq), tk), :]
            scores.append(lax.dot_general(k, q_ref[0, h], nt_dims, preferred_element_type=F32))
        out = []
        for h in range(nh):
            st = scores[h]
            if masked:
                kv_i = lax.broadcasted_iota(jnp.int32, (tk, tq), 0)
                q_i = lax.broadcasted_iota(jnp.int32, (tk, tq), 1)
                st = jnp.where(kv_i - (nblk - 1) * tq <= q_i, st, -jnp.inf)
            if running_max:
                m, l, acc = carry[h]
                m_new = jnp.maximum(m, jnp.max(st, axis=0, keepdims=True))
                alpha = jnp.exp2(m - m_new)
                pt = jnp.exp2(st - m_new)
                l = alpha * l
                acc = alpha * acc
            else:
                l, acc = carry[h]
                pt = jnp.exp2(st)
            l = l + jnp.sum(pt, axis=0, keepdims=True)
            pt = pt.astype(BF16)
            slab0 = blk0 * (tq // tc)
            pv = jnp.dot(vt_ref[0, h, slab0], pt[0:tc], preferred_element_type=F32)
            for c in range(1, tk // tc):
                pv += jnp.dot(vt_ref[0, h, slab0 + c], pt[c * tc:(c + 1) * tc], preferred_element_type=F32)
            out.append((m_new, l, acc + pv) if running_max else (l, acc + pv))
        return tuple(out)

    stats = (jnp.zeros((1, tq), F32), jnp.zeros((dv, tq), F32))
    if running_max:
        stats = (jnp.full((1, tq), -jnp.inf, F32),) + stats
    init = tuple(stats for _ in range(nh))
    carry = lax.fori_loop(0, lax.shift_right_logical(qi, 1), lambda j, c: step(2 * j, 2, c, False), init)
    carry = lax.cond(jnp.bitwise_and(qi, 1) == 1,
                     lambda c: step(qi - 1, 2, c, True),
                     lambda c: step(qi, 1, c, True), carry)
    for h in range(nh):
        l, acc = carry[h][-2:]
        o_ref[0, :, h * dv:(h + 1) * dv] = (acc / l).T.astype(o_ref.dtype)


def _attention(q, k, vt, tq, running_max):
    b, h, s, dq = q.shape
    _, _, nslab, dvv, tc = vt.shape
    nh = ATTN_HEADS_PER_STEP
    return pl.pallas_call(
        functools.partial(_attn_kernel, tq=tq, running_max=running_max),
        out_shape=jax.ShapeDtypeStruct((b, s, h * dvv), BF16),
        grid=(b, h // nh, s // tq),
        in_specs=[
            pl.BlockSpec((1, nh, tq, dq), lambda bi, hi, qi: (bi, hi, qi, 0)),
            pl.BlockSpec((1, nh, s, dq), lambda bi, hi, qi: (bi, hi, 0, 0)),
            pl.BlockSpec((1, nh, nslab, dvv, tc), lambda bi, hi, qi: (bi, hi, 0, 0, 0)),
        ],
        out_specs=pl.BlockSpec((1, tq, nh * dvv), lambda bi, hi, qi: (bi, qi, hi)),
        compiler_params=_params(("parallel", "parallel", "arbitrary"), 48),
        name="mla_attention" if running_max else "mla_attention_bounded",
    )(q, k, vt)


def _rope_tables(positions):
    b, s = positions.shape
    half = MLA_ROPE // 2
    inv_freq = ROPE_THETA ** (-jnp.arange(half, dtype=F32) / half)
    ang = positions.astype(F32)[..., None] * inv_freq
    cos, sin = _trig(ang.reshape(-1, LANE))
    cos = cos.reshape(b * s, half)
    sin = sin.reshape(b * s, half)
    zero = jnp.zeros((b * s, LANE - 2 * half), F32)
    cos4 = jnp.concatenate([cos, cos, zero], axis=-1)
    sin4 = jnp.concatenate([-sin, sin, zero], axis=-1)
    return cos4, sin4


def _split_rope_cols(w_rope):
    half = MLA_ROPE // 2
    x1, x2 = w_rope[..., :half], w_rope[..., half:]
    zero = jnp.zeros(w_rope.shape[:-1] + (LANE - MLA_ROPE,), w_rope.dtype)
    return jnp.concatenate([x1, x2, zero], axis=-1), jnp.concatenate([x2, x1, zero], axis=-1)


def _pad_rope_gain(gain):
    g_n = gain[:MLA_NOPE].reshape(1, MLA_NOPE)
    g_r = jnp.concatenate([gain[MLA_NOPE:], jnp.zeros((LANE - MLA_ROPE,), gain.dtype)]).reshape(1, LANE)
    return g_n, g_r


def kernel(x, c, positions, ada_w, ada_b, norm_mix, norm_ffn, gla_w_in, gla_w_alpha, gla_b_alpha, gla_onorm,
           gla_w_out, mla_w_dq, mla_q_lat_norm, mla_w_uq, mla_q_norm, mla_w_out, kv_ada_w, kv_ada_b, kv_norm,
           kv_w_dkv, kv_lat_norm, kv_w_ukv, kv_k_norm, ffn_w_gu, ffn_w_down):
    batch, seq, d = x.shape
    t = batch * seq
    depth = ada_w.shape[0]
    n_gla = gla_w_in.shape[0]
    xf = x.reshape(t, d)

    tm_big = min(seq, 1024)
    tm_mid = min(seq, 512)

    c_pad = jnp.pad(c, ((0, ADA_ROWS - batch), (0, 0)))
    cos4, sin4 = _rope_tables(positions)
    w_gu_bf = ffn_w_gu.astype(BF16)
    w_down_bf = ffn_w_down.astype(BF16)

    k_sh = v_sh = None
    for layer in range(depth):
        mod = _ada_mod(c_pad, ada_w, ada_b, layer)[:batch]
        shift_m, scale_m, gate_m, shift_f, scale_f, gate_f = jnp.split(mod, 6, axis=-1)
        if layer < n_gla:
            qkvg = gla_w_in.shape[2] - GLA_GATE_RANK
            w_a = jnp.pad(gla_w_in[layer, :, qkvg:], ((0, 0), (0, LANE - GLA_GATE_RANK))).astype(BF16)
            proj, a_lr = _mod_matmul(xf, norm_mix[layer], shift_m, scale_m, gla_w_in.astype(BF16), layer, qkvg,
                                     w_a, seq, tm_big, 2048)
            w_alpha_p = jnp.pad(gla_w_alpha[layer], ((0, LANE - GLA_GATE_RANK), (0, 0))).astype(BF16)
            mix_in = _gla(proj.reshape(batch, seq, qkvg), a_lr.reshape(batch, seq, LANE), w_alpha_p,
                          gla_b_alpha[layer], gla_onorm[layer], batch, seq).reshape(t, -1)
            w_out = gla_w_out[layer].astype(BF16)
        else:
            j = layer - n_gla
            if j == 0:
                kv_mod = _ada_mod(c_pad, kv_ada_w[None], kv_ada_b[None], 0)[:batch]
                kv_shift, kv_scale = jnp.split(kv_mod, 2, axis=-1)
                pe_p, pe_s = _split_rope_cols(kv_w_dkv[:, KV_LORA:])
                w_dkv_p = jnp.concatenate([kv_w_dkv[:, :KV_LORA], pe_p, pe_s], axis=1).astype(BF16)
                w_ukv = kv_w_ukv.reshape(KV_LORA, MLA_HEADS, MLA_NOPE + MLA_V)
                w_k = w_ukv[:, :, :MLA_NOPE].reshape(KV_LORA, MLA_HEADS * MLA_NOPE).astype(BF16)
                w_v = w_ukv[:, :, MLA_NOPE:].reshape(KV_LORA, MLA_HEADS * MLA_V).astype(BF16)
                kg_n, kg_r = _pad_rope_gain(kv_k_norm)
                k_sh, v_sh = _mla_kv(xf, kv_norm, kv_shift, kv_scale, w_dkv_p, kv_lat_norm, w_k, w_v,
                                     kg_n, kg_r, cos4, sin4, batch, seq, tm_mid)
            q_lora = mla_w_dq.shape[2]
            w_uq = mla_w_uq[j].reshape(q_lora, MLA_HEADS, MLA_NOPE + MLA_ROPE)
            w_qn = w_uq[:, :, :MLA_NOPE].reshape(q_lora, MLA_HEADS * MLA_NOPE).astype(BF16)
            r_p, r_s = _split_rope_cols(w_uq[:, :, MLA_NOPE:])
            w_qr = r_p.reshape(q_lora, MLA_HEADS * LANE).astype(BF16)
            w_qs = r_s.reshape(q_lora, MLA_HEADS * LANE).astype(BF16)
            qg_n, qg_r = _pad_rope_gain(mla_q_norm[j])
            q = _mla_q(xf, norm_mix[layer], shift_m, scale_m, mla_w_dq[j].astype(BF16), mla_q_lat_norm[j],
                       w_qn, w_qr, w_qs, qg_n, qg_r, cos4, sin4, batch, seq, tm_mid)
            score_bound = ((MLA_NOPE + MLA_ROPE) ** 0.5 * LOG2_E
                           * jnp.max(jnp.abs(mla_q_norm[j])) * jnp.max(jnp.abs(kv_k_norm)))
            mix_in = lax.cond(score_bound <= ATTN_MAX_SCORE_BOUND,
                              lambda: _attention(q, k_sh, v_sh, tm_mid, False),
                              lambda: _attention(q, k_sh, v_sh, tm_mid, True))
            mix_in = mix_in.reshape(t, MLA_HEADS * MLA_V)
            w_out = mla_w_out[j].astype(BF16)
        xf = _matmul_residual(mix_in, w_out, xf, gate_m, seq, tm_mid, d)
        xf = _ffn(xf, norm_ffn[layer], shift_f, scale_f, gate_f, w_gu_bf, w_down_bf, layer, seq, tm_big, 512)
    return xf.reshape(batch, seq, d)
```

```python
import functools

import jax
import jax.numpy as jnp
from jax import lax
from jax.experimental import pallas as pl
from jax.experimental.pallas import tpu as pltpu

GLA_HEADS = 4
GLA_GATE_RANK = 16
GLA_TAU = 16.0
GLA_CHUNK = 64
MLA_HEADS = 16
MLA_NOPE = 128
MLA_ROPE = 64
MLA_V = 128
KV_LORA = 512
ROPE_THETA = 10000.0
EPS = 1e-6
LOG2_E = 1.4426950408889634

LANE = 128
MIB = 1 << 20

F32 = jnp.float32
BF16 = jnp.bfloat16


def _params(semantics, vmem_mib):
    return pltpu.CompilerParams(dimension_semantics=semantics, vmem_limit_bytes=vmem_mib * MIB)


def _silu(x):
    return x * jax.nn.sigmoid(x)


def _rms_scale(x):
    return x * lax.rsqrt(jnp.mean(x * x, axis=-1, keepdims=True) + EPS)


ADA_ROWS = 16


def _split_bf16(x):
    hi = x.astype(BF16)
    return hi, (x - hi.astype(F32)).astype(BF16)


def _ada_kernel(c_ref, w_ref, b_ref, o_ref):
    s_hi, s_lo = _split_bf16(_silu(c_ref[...]))
    w_hi, w_lo = _split_bf16(w_ref[...])
    acc = jnp.dot(s_hi, w_lo, preferred_element_type=F32) + jnp.dot(s_lo, w_hi, preferred_element_type=F32)
    o_ref[...] = (acc + jnp.dot(s_hi, w_hi, preferred_element_type=F32)) + b_ref[...]


def _ada_mod(c_pad, w, b, layer, tn=1024):
    rows, d = c_pad.shape
    n = w.shape[2]
    return pl.pallas_call(
        _ada_kernel,
        out_shape=jax.ShapeDtypeStruct((rows, n), F32),
        grid=(n // tn,),
        in_specs=[
            pl.BlockSpec((rows, d), lambda j: (0, 0)),
            pl.BlockSpec((None, d, tn), lambda j: (layer, 0, j)),
            pl.BlockSpec((None, 1, tn), lambda j: (layer, 0, j)),
        ],
        out_specs=pl.BlockSpec((rows, tn), lambda j: (0, j)),
        compiler_params=_params(("parallel",), 48),
        name="ada_mod",
    )(c_pad, w, b[:, None, :])


def _trig_kernel(a_ref, c_ref, s_ref):
    a = a_ref[...]
    c_ref[...] = jnp.cos(a)
    s_ref[...] = jnp.sin(a)


def _trig(ang):
    rows = ang.shape[0]
    tr = min(rows, 512)
    spec = pl.BlockSpec((tr, LANE), lambda i: (i, 0))
    return pl.pallas_call(
        _trig_kernel,
        out_shape=(jax.ShapeDtypeStruct(ang.shape, F32),) * 2,
        grid=(rows // tr,),
        in_specs=[spec],
        out_specs=(spec, spec),
        compiler_params=_params(("parallel",), 16),
        name="rope_trig",
    )(ang)


MOD_ROWS = 16


def _modulate_into(h_ref, x_ref, g_ref, sh_ref, sc_ref):
    tm = x_ref.shape[0]
    step = min(tm, MOD_ROWS)
    gs = g_ref[...] * (1.0 + sc_ref[0])
    shift = sh_ref[0]
    for r in range(tm // step):
        rows = slice(r * step, (r + 1) * step)
        h_ref[rows, :] = (_rms_scale(x_ref[rows, :]) * gs + shift).astype(h_ref.dtype)


def _modmm_kernel(x_ref, g_ref, sh_ref, sc_ref, w_ref, wa_ref, o_ref, a_ref, h_ref):
    @pl.when(pl.program_id(1) == 0)
    def _():
        _modulate_into(h_ref, x_ref, g_ref, sh_ref, sc_ref)
        a_ref[...] = jnp.dot(h_ref[...], wa_ref[...], preferred_element_type=F32).astype(a_ref.dtype)

    o_ref[...] = jnp.dot(h_ref[...], w_ref[...], preferred_element_type=F32).astype(o_ref.dtype)


def _mod_matmul(x, gain, shift, scale, w, layer, n, w_a, seq, tm, tn):
    t, d = x.shape
    per_b = seq // tm
    vec = lambda i, j: (i // per_b, 0, 0)
    return pl.pallas_call(
        _modmm_kernel,
        out_shape=(jax.ShapeDtypeStruct((t, n), BF16), jax.ShapeDtypeStruct((t, LANE), BF16)),
        grid=(t // tm, n // tn),
        in_specs=[
            pl.BlockSpec((tm, d), lambda i, j: (i, 0)),
            pl.BlockSpec((1, d), lambda i, j: (0, 0)),
            pl.BlockSpec((1, 1, d), vec),
            pl.BlockSpec((1, 1, d), vec),
            pl.BlockSpec((None, d, tn), lambda i, j: (layer, 0, j)),
            pl.BlockSpec((d, LANE), lambda i, j: (0, 0)),
        ],
        out_specs=(pl.BlockSpec((tm, tn), lambda i, j: (i, j)),
                   pl.BlockSpec((tm, LANE), lambda i, j: (i, 0))),
        scratch_shapes=[pltpu.VMEM((tm, d), BF16)],
        compiler_params=_params(("parallel", "arbitrary"), 56),
        name="mod_matmul",
    )(x, gain.reshape(1, d), shift[:, None, :], scale[:, None, :], w, w_a)


def _mm_res_kernel(a_ref, w_ref, x_ref, gate_ref, o_ref):
    acc = jnp.dot(a_ref[...], w_ref[...], preferred_element_type=F32)
    o_ref[...] = x_ref[...] + gate_ref[0] * acc


def _matmul_residual(a, w, x, gate, seq, tm, tn):
    t, k = a.shape
    n = w.shape[1]
    per_b = seq // tm
    return pl.pallas_call(
        _mm_res_kernel,
        out_shape=jax.ShapeDtypeStruct((t, n), F32),
        grid=(t // tm, n // tn),
        in_specs=[
            pl.BlockSpec((tm, k), lambda i, j: (i, 0)),
            pl.BlockSpec((k, tn), lambda i, j: (0, j)),
            pl.BlockSpec((tm, tn), lambda i, j: (i, j)),
            pl.BlockSpec((1, 1, tn), lambda i, j: (i // per_b, 0, j)),
        ],
        out_specs=pl.BlockSpec((tm, tn), lambda i, j: (i, j)),
        compiler_params=_params(("parallel", "arbitrary"), 48),
        name="matmul_residual",
    )(a, w, x, gate[:, None, :])


def _ffn_kernel(x_ref, g_ref, sh_ref, sc_ref, gate_ref, wg_ref, wu_ref, wd_ref, o_ref, h_ref):
    f = pl.program_id(1)

    @pl.when(f == 0)
    def _():
        _modulate_into(h_ref, x_ref, g_ref, sh_ref, sc_ref)
        o_ref[...] = jnp.zeros_like(o_ref)

    h = h_ref[...]
    g = jnp.dot(h, wg_ref[...], preferred_element_type=F32)
    u = jnp.dot(h, wu_ref[...], preferred_element_type=F32)
    a = (_silu(g) * u).astype(BF16)
    o_ref[...] += jnp.dot(a, wd_ref[...], preferred_element_type=F32)

    @pl.when(f == pl.num_programs(1) - 1)
    def _():
        o_ref[...] = x_ref[...] + gate_ref[0] * o_ref[...]


def _ffn(x, gain, shift, scale, gate, w_gu, w_down, layer, seq, tm, tf):
    t, d = x.shape
    dff = w_down.shape[1]
    nf = dff // tf
    per_b = seq // tm
    vec = lambda i, f: (i // per_b, 0, 0)
    return pl.pallas_call(
        _ffn_kernel,
        out_shape=jax.ShapeDtypeStruct((t, d), F32),
        grid=(t // tm, nf),
        in_specs=[
            pl.BlockSpec((tm, d), lambda i, f: (i, 0)),
            pl.BlockSpec((1, d), lambda i, f: (0, 0)),
            pl.BlockSpec((1, 1, d), vec),
            pl.BlockSpec((1, 1, d), vec),
            pl.BlockSpec((1, 1, d), vec),
            pl.BlockSpec((None, d, tf), lambda i, f: (layer, 0, f)),
            pl.BlockSpec((None, d, tf), lambda i, f: (layer, 0, f + nf)),
            pl.BlockSpec((None, tf, d), lambda i, f: (layer, f, 0)),
        ],
        out_specs=pl.BlockSpec((tm, d), lambda i, f: (i, 0)),
        scratch_shapes=[pltpu.VMEM((tm, d), BF16)],
        compiler_params=_params(("parallel", "arbitrary"), 60),
        name="ffn",
    )(x, gain.reshape(1, d), shift[:, None, :], scale[:, None, :], gate[:, None, :], w_gu, w_gu, w_down)


GLA_PAIR = 2 * GLA_CHUNK


GLA_BATCH_PER_STEP = 4


def _gla_pair(q_ref, k_ref, v_ref, g_ref, a_ref, wal_ref, bal_ref, on_ref, o_ref, st_ref, bb, dk, dv):
    ch = GLA_CHUNK
    tp = q_ref.shape[1]
    qk = q_ref.shape[2]

    z = jnp.dot(a_ref[bb], wal_ref[...], preferred_element_type=F32) + bal_ref[...]
    la = (jnp.minimum(z, 0.0) - jnp.log(1.0 + jnp.exp(-jnp.abs(z)))) * (LOG2_E / GLA_TAU)

    row = lax.broadcasted_iota(jnp.int32, (tp, tp), 0)
    col = lax.broadcasted_iota(jnp.int32, (tp, tp), 1)
    causal = jnp.logical_and(col <= row, col >= (row // ch) * ch)
    cross = jnp.logical_and(row >= ch, col < ch)
    tri = jnp.where(causal, 1.0, 0.0).astype(BF16)
    first = lax.broadcasted_iota(jnp.int32, (tp, qk), 0) < ch
    first_h = lax.broadcasted_iota(jnp.int32, (tp, dk), 0) < ch

    la_hi = la.astype(BF16)
    la_lo = (la - la_hi.astype(F32)).astype(BF16)
    bcum = jnp.dot(tri, la_hi, preferred_element_type=F32) + jnp.dot(tri, la_lo, preferred_element_type=F32)
    b_last0 = bcum[ch - 1:ch]
    b_last1 = bcum[tp - 1:tp]
    e_q = jnp.exp2(bcum)
    e_ki = jnp.exp2(-bcum)
    e_ks = jnp.exp2(jnp.where(first, b_last0, b_last1) - bcum)
    dec0 = jnp.exp2(b_last0)
    dec1 = jnp.exp2(b_last1)
    dec01 = jnp.exp2(b_last0 + b_last1)
    on = on_ref[...]
    tn_dims = (((0,), (0,)), ((), ()))
    nt_dims = (((1,), (1,)), ((), ()))

    for h in range(GLA_HEADS):
        ks = slice(h * dk, (h + 1) * dk)
        vs = slice(h * dv, (h + 1) * dv)
        q = q_ref[bb, :, ks].astype(F32) * (dk ** -0.5)
        k = k_ref[bb, :, ks].astype(F32)
        v = v_ref[bb, :, vs]
        q_dec = q * e_q[:, ks]
        k_st = k * e_ks[:, ks]
        q_b = q_dec.astype(BF16)
        k_in = (k * e_ki[:, ks]).astype(BF16)
        k0 = jnp.where(first_h, k_st, 0.0).astype(BF16)
        att = lax.dot_general(q_b, k_in, nt_dims, preferred_element_type=F32)
        att_x = lax.dot_general(q_b, k0, nt_dims, preferred_element_type=F32)
        att = jnp.where(causal, att, jnp.where(cross, att_x, 0.0)).astype(BF16)
        s0 = st_ref[bb, h]
        q_s = jnp.where(first_h, q_dec, q_dec * dec0[:, ks]).astype(BF16)
        o = (jnp.dot(att, v, preferred_element_type=F32)
             + jnp.dot(q_s, s0.astype(BF16), preferred_element_type=F32))
        k_u = jnp.where(first_h, k_st * dec1[:, ks], k_st).astype(BF16)
        dec_cols = jnp.tile(jnp.broadcast_to(dec01[:, ks], (LANE, dk)).T, (1, dv // LANE))
        st_ref[bb, h] = dec_cols * s0 + lax.dot_general(k_u, v, tn_dims, preferred_element_type=F32)
        gg = g_ref[bb, :, vs].astype(F32)
        o_ref[bb, :, vs] = ((_rms_scale(o) * on) * _silu(gg)).astype(o_ref.dtype)


def _gla_kernel(q_ref, k_ref, v_ref, g_ref, a_ref, wal_ref, bal_ref, on_ref, o_ref, st_ref, *, dk, dv):
    @pl.when(pl.program_id(1) == 0)
    def _():
        st_ref[...] = jnp.zeros_like(st_ref)

    for bb in range(q_ref.shape[0]):
        _gla_pair(q_ref, k_ref, v_ref, g_ref, a_ref, wal_ref, bal_ref, on_ref, o_ref, st_ref, bb, dk, dv)


def _gla(proj, a_lr, w_alpha_p, b_alpha, onorm, batch, seq):
    dk = w_alpha_p.shape[1] // GLA_HEADS
    dv = onorm.shape[0]
    qk = GLA_HEADS * dk
    vv = GLA_HEADS * dv
    tp = GLA_PAIR
    nb = GLA_BATCH_PER_STEP if batch % GLA_BATCH_PER_STEP == 0 else 1
    return pl.pallas_call(
        functools.partial(_gla_kernel, dk=dk, dv=dv),
        out_shape=jax.ShapeDtypeStruct((batch, seq, vv), BF16),
        grid=(batch // nb, seq // tp),
        in_specs=[
            pl.BlockSpec((nb, tp, qk), lambda b, i: (b, i, 0)),
            pl.BlockSpec((nb, tp, qk), lambda b, i: (b, i, 1)),
            pl.BlockSpec((nb, tp, vv), lambda b, i: (b, i, 2 * qk // vv)),
            pl.BlockSpec((nb, tp, vv), lambda b, i: (b, i, 2 * qk // vv + 1)),
            pl.BlockSpec((nb, tp, LANE), lambda b, i: (b, i, 0)),
            pl.BlockSpec((LANE, qk), lambda b, i: (0, 0)),
            pl.BlockSpec((1, qk), lambda b, i: (0, 0)),
            pl.BlockSpec((1, dv), lambda b, i: (0, 0)),
        ],
        out_specs=pl.BlockSpec((nb, tp, vv), lambda b, i: (b, i, 0)),
        scratch_shapes=[pltpu.VMEM((nb, GLA_HEADS, dk, dv), F32)],
        compiler_params=_params(("parallel", "arbitrary"), 56),
        name="gla",
    )(proj, proj, proj, proj, a_lr, w_alpha_p, b_alpha.reshape(1, qk), onorm.reshape(1, dv))


def _rope_pair(x, x_swapped, cos4, sin4):
    return x * cos4 + x_swapped * sin4


def _head_norm_store(o_ref, h, rows, nope, rope_sq, rope_g, gain_n, dim, post_scale):
    ss = jnp.sum(nope * nope + rope_sq, axis=-1, keepdims=True)
    r = lax.rsqrt(ss / dim + EPS)
    if post_scale != 1.0:
        r = r * post_scale
    o_ref[0, h, rows, 0:LANE] = ((nope * r) * gain_n).astype(o_ref.dtype)
    o_ref[0, h, rows, LANE:2 * LANE] = (rope_g * r).astype(o_ref.dtype)


MLA_SUB = 256


def _sub_tiles(tm):
    step = min(tm, MLA_SUB)
    return [slice(r * step, (r + 1) * step) for r in range(tm // step)]


def _mla_kv_kernel(x_ref, g_ref, sh_ref, sc_ref, wd_ref, ln_ref, wk_ref, wv_ref, kn_ref, kr_ref,
                   cos_ref, sin_ref, k_ref, v_ref, h_ref):
    lora = ln_ref.shape[1]
    gn = kn_ref[...]
    gr = kr_ref[...]
    _modulate_into(h_ref, x_ref, g_ref, sh_ref, sc_ref)
    for rows in _sub_tiles(x_ref.shape[0]):
        ckv = jnp.dot(h_ref[rows, :], wd_ref[...], preferred_element_type=F32)
        c_lat = (_rms_scale(ckv[:, :lora]) * ln_ref[...]).astype(BF16)
        k_pe = _rope_pair(ckv[:, lora:lora + LANE], ckv[:, lora + LANE:lora + 2 * LANE],
                          cos_ref[rows, :], sin_ref[rows, :])
        kn = jnp.dot(c_lat, wk_ref[...], preferred_element_type=F32)
        vv = jnp.dot(c_lat, wv_ref[...], preferred_element_type=F32)
        pe_sq = k_pe * k_pe
        pe_g = k_pe * gr
        for h in range(MLA_HEADS):
            cols = slice(h * LANE, (h + 1) * LANE)
            _head_norm_store(k_ref, h, rows, kn[:, cols], pe_sq, pe_g, gn, MLA_NOPE + MLA_ROPE, 1.0)
            v_ref[0, h, 0, :, rows] = vv[:, cols].T.astype(v_ref.dtype)


def _mla_q_kernel(x_ref, g_ref, sh_ref, sc_ref, wd_ref, ln_ref, wn_ref, wr_ref, ws_ref, qn_ref, qr_ref,
                  cos_ref, sin_ref, q_ref, h_ref):
    gn = qn_ref[...]
    gr = qr_ref[...]
    sm_scale = (MLA_NOPE + MLA_ROPE) ** -0.5 * LOG2_E
    _modulate_into(h_ref, x_ref, g_ref, sh_ref, sc_ref)
    for rows in _sub_tiles(x_ref.shape[0]):
        cq = jnp.dot(h_ref[rows, :], wd_ref[...], preferred_element_type=F32)
        cq = (_rms_scale(cq) * ln_ref[...]).astype(BF16)
        qn = jnp.dot(cq, wn_ref[...], preferred_element_type=F32)
        qr = jnp.dot(cq, wr_ref[...], preferred_element_type=F32)
        qs = jnp.dot(cq, ws_ref[...], preferred_element_type=F32)
        cos4 = cos_ref[rows, :]
        sin4 = sin_ref[rows, :]
        for h in range(MLA_HEADS):
            cols = slice(h * LANE, (h + 1) * LANE)
            rope = _rope_pair(qr[:, cols], qs[:, cols], cos4, sin4)
            _head_norm_store(q_ref, h, rows, qn[:, cols], rope * rope, rope * gr, gn, MLA_NOPE + MLA_ROPE, sm_scale)


def _const_spec(arr):
    nd = arr.ndim
    return pl.BlockSpec(arr.shape, lambda b, i: (0,) * nd)


def _mla_kv(x, gain, shift, scale, w_dkv_p, lat_norm, w_k, w_v, kg_n, kg_r, cos4, sin4, batch, seq, tm):
    t, d = x.shape
    nt = seq // tm
    row = lambda b, i: (b * nt + i, 0)
    vec = lambda b, i: (b, 0, 0)
    g2 = gain.reshape(1, d)
    weights = [w_dkv_p, lat_norm.reshape(1, -1), w_k, w_v, kg_n, kg_r]
    hk = pl.BlockSpec((1, MLA_HEADS, tm, 2 * LANE), lambda b, i: (b, 0, i, 0))
    hv = pl.BlockSpec((1, MLA_HEADS, 1, MLA_V, tm), lambda b, i: (b, 0, i, 0, 0))
    return pl.pallas_call(
        _mla_kv_kernel,
        out_shape=(jax.ShapeDtypeStruct((batch, MLA_HEADS, seq, 2 * LANE), BF16),
                   jax.ShapeDtypeStruct((batch, MLA_HEADS, nt, MLA_V, tm), BF16)),
        grid=(batch, nt),
        in_specs=[pl.BlockSpec((tm, d), row), _const_spec(g2),
                  pl.BlockSpec((1, 1, d), vec), pl.BlockSpec((1, 1, d), vec)]
                 + [_const_spec(w) for w in weights]
                 + [pl.BlockSpec((tm, LANE), row), pl.BlockSpec((tm, LANE), row)],
        out_specs=(hk, hv),
        scratch_shapes=[pltpu.VMEM((tm, d), BF16)],
        compiler_params=_params(("parallel", "parallel"), 48),
        name="mla_kv",
    )(x, g2, shift[:, None, :], scale[:, None, :], *weights, cos4, sin4)


def _mla_q(x, gain, shift, scale, w_dq, q_lat_norm, w_qn, w_qr, w_qs, qg_n, qg_r, cos4, sin4, batch, seq, tm):
    t, d = x.shape
    nt = seq // tm
    row = lambda b, i: (b * nt + i, 0)
    vec = lambda b, i: (b, 0, 0)
    g2 = gain.reshape(1, d)
    weights = [w_dq, q_lat_norm.reshape(1, -1), w_qn, w_qr, w_qs, qg_n, qg_r]
    hq = pl.BlockSpec((1, MLA_HEADS, tm, 2 * LANE), lambda b, i: (b, 0, i, 0))
    return pl.pallas_call(
        _mla_q_kernel,
        out_shape=jax.ShapeDtypeStruct((batch, MLA_HEADS, seq, 2 * LANE), BF16),
        grid=(batch, nt),
        in_specs=[pl.BlockSpec((tm, d), row), _const_spec(g2),
                  pl.BlockSpec((1, 1, d), vec), pl.BlockSpec((1, 1, d), vec)]
                 + [_const_spec(w) for w in weights]
                 + [pl.BlockSpec((tm, LANE), row), pl.BlockSpec((tm, LANE), row)],
        out_specs=hq,
        scratch_shapes=[pltpu.VMEM((tm, d), BF16)],
        compiler_params=_params(("parallel", "parallel"), 48),
        name="mla_q",
    )(x, g2, shift[:, None, :], scale[:, None, :], *weights, cos4, sin4)


ATTN_HEADS_PER_STEP = 4
ATTN_Q_BLOCKS_PER_STEP = 2
ATTN_MAX_SCORE_BOUND = 60.0


def _attn_kernel(q_ref, k_ref, vt_ref, o_ref, *, tq, running_max):
    pair = pl.program_id(2)
    nh = q_ref.shape[1]
    tc = vt_ref.shape[-1]
    dv = vt_ref.shape[-2]
    nt_dims = (((1,), (1,)), ((), ()))

    def step(rows, blk0, nblk, carry, masked):
        tk = nblk * tq
        scores = []
        for h in range(nh):
            k = k_ref[0, h, pl.ds(pl.multiple_of(blk0 * tq, tq), tk), :]
            scores.append(lax.dot_general(k, q_ref[0, h, rows, :], nt_dims, preferred_element_type=F32))
        out = []
        for h in range(nh):
            st = scores[h]
            if masked:
                kv_i = lax.broadcasted_iota(jnp.int32, (tk, tq), 0)
                q_i = lax.broadcasted_iota(jnp.int32, (tk, tq), 1)
                st = jnp.where(kv_i - (nblk - 1) * tq <= q_i, st, -jnp.inf)
            if running_max:
                m, l, acc = carry[h]
                m_new = jnp.maximum(m, jnp.max(st, axis=0, keepdims=True))
                alpha = jnp.exp2(m - m_new)
                pt = jnp.exp2(st - m_new)
                l = alpha * l
                acc = alpha * acc
            else:
                l, acc = carry[h]
                pt = jnp.exp2(st)
            l = l + jnp.sum(pt, axis=0, keepdims=True)
            pt = pt.astype(BF16)
            slab0 = blk0 * (tq // tc)
            pv = jnp.dot(vt_ref[0, h, slab0], pt[0:tc], preferred_element_type=F32)
            for c in range(1, tk // tc):
                pv += jnp.dot(vt_ref[0, h, slab0 + c], pt[c * tc:(c + 1) * tc], preferred_element_type=F32)
            out.append((m_new, l, acc + pv) if running_max else (l, acc + pv))
        return tuple(out)

    stats = (jnp.zeros((1, tq), F32), jnp.zeros((dv, tq), F32))
    if running_max:
        stats = (jnp.full((1, tq), -jnp.inf, F32),) + stats
    init = tuple(stats for _ in range(nh))
    for sub in range(ATTN_Q_BLOCKS_PER_STEP):
        rows = slice(sub * tq, (sub + 1) * tq)
        carry = lax.fori_loop(0, pair, lambda j, c: step(rows, 2 * j, 2, c, False), init)
        carry = step(rows, 2 * pair, sub + 1, carry, True)
        for h in range(nh):
            l, acc = carry[h][-2:]
            o_ref[0, rows, h * dv:(h + 1) * dv] = (acc / l).T.astype(o_ref.dtype)


def _attention(q, k, vt, tq, running_max):
    b, h, s, dq = q.shape
    _, _, nslab, dvv, tc = vt.shape
    nh = ATTN_HEADS_PER_STEP
    return pl.pallas_call(
        functools.partial(_attn_kernel, tq=tq, running_max=running_max),
        out_shape=jax.ShapeDtypeStruct((b, s, h * dvv), BF16),
        grid=(b, h // nh, s // (ATTN_Q_BLOCKS_PER_STEP * tq)),
        in_specs=[
            pl.BlockSpec((1, nh, ATTN_Q_BLOCKS_PER_STEP * tq, dq), lambda bi, hi, qi: (bi, hi, qi, 0)),
            pl.BlockSpec((1, nh, s, dq), lambda bi, hi, qi: (bi, hi, 0, 0)),
            pl.BlockSpec((1, nh, nslab, dvv, tc), lambda bi, hi, qi: (bi, hi, 0, 0, 0)),
        ],
        out_specs=pl.BlockSpec((1, ATTN_Q_BLOCKS_PER_STEP * tq, nh * dvv), lambda bi, hi, qi: (bi, qi, hi)),
        compiler_params=_params(("parallel", "parallel", "arbitrary"), 48),
        name="mla_attention" if running_max else "mla_attention_bounded",
    )(q, k, vt)


def _rope_tables(positions):
    b, s = positions.shape
    half = MLA_ROPE // 2
    inv_freq = ROPE_THETA ** (-jnp.arange(half, dtype=F32) / half)
    ang = positions.astype(F32)[..., None] * inv_freq
    cos, sin = _trig(ang.reshape(-1, LANE))
    cos = cos.reshape(b * s, half)
    sin = sin.reshape(b * s, half)
    zero = jnp.zeros((b * s, LANE - 2 * half), F32)
    cos4 = jnp.concatenate([cos, cos, zero], axis=-1)
    sin4 = jnp.concatenate([-sin, sin, zero], axis=-1)
    return cos4, sin4


def _split_rope_cols(w_rope):
    half = MLA_ROPE // 2
    x1, x2 = w_rope[..., :half], w_rope[..., half:]
    zero = jnp.zeros(w_rope.shape[:-1] + (LANE - MLA_ROPE,), w_rope.dtype)
    return jnp.concatenate([x1, x2, zero], axis=-1), jnp.concatenate([x2, x1, zero], axis=-1)


def _pad_rope_gain(gain):
    g_n = gain[:MLA_NOPE].reshape(1, MLA_NOPE)
    g_r = jnp.concatenate([gain[MLA_NOPE:], jnp.zeros((LANE - MLA_ROPE,), gain.dtype)]).reshape(1, LANE)
    return g_n, g_r


def kernel(x, c, positions, ada_w, ada_b, norm_mix, norm_ffn, gla_w_in, gla_w_alpha, gla_b_alpha, gla_onorm,
           gla_w_out, mla_w_dq, mla_q_lat_norm, mla_w_uq, mla_q_norm, mla_w_out, kv_ada_w, kv_ada_b, kv_norm,
           kv_w_dkv, kv_lat_norm, kv_w_ukv, kv_k_norm, ffn_w_gu, ffn_w_down):
    batch, seq, d = x.shape
    t = batch * seq
    depth = ada_w.shape[0]
    n_gla = gla_w_in.shape[0]
    xf = x.reshape(t, d)

    tm_big = min(seq, 1024)
    tm_mid = min(seq, 512)
    t_attn = min(tm_mid, seq // ATTN_Q_BLOCKS_PER_STEP)

    c_pad = jnp.pad(c, ((0, ADA_ROWS - batch), (0, 0)))
    cos4, sin4 = _rope_tables(positions)
    w_gu_bf = ffn_w_gu.astype(BF16)
    w_down_bf = ffn_w_down.astype(BF16)

    k_sh = v_sh = None
    for layer in range(depth):
        mod = _ada_mod(c_pad, ada_w, ada_b, layer)[:batch]
        shift_m, scale_m, gate_m, shift_f, scale_f, gate_f = jnp.split(mod, 6, axis=-1)
        if layer < n_gla:
            qkvg = gla_w_in.shape[2] - GLA_GATE_RANK
            w_a = jnp.pad(gla_w_in[layer, :, qkvg:], ((0, 0), (0, LANE - GLA_GATE_RANK))).astype(BF16)
            proj, a_lr = _mod_matmul(xf, norm_mix[layer], shift_m, scale_m, gla_w_in.astype(BF16), layer, qkvg,
                                     w_a, seq, tm_big, 2048)
            w_alpha_p = jnp.pad(gla_w_alpha[layer], ((0, LANE - GLA_GATE_RANK), (0, 0))).astype(BF16)
            mix_in = _gla(proj.reshape(batch, seq, qkvg), a_lr.reshape(batch, seq, LANE), w_alpha_p,
                          gla_b_alpha[layer], gla_onorm[layer], batch, seq).reshape(t, -1)
            w_out = gla_w_out[layer].astype(BF16)
        else:
            j = layer - n_gla
            if j == 0:
                kv_mod = _ada_mod(c_pad, kv_ada_w[None], kv_ada_b[None], 0)[:batch]
                kv_shift, kv_scale = jnp.split(kv_mod, 2, axis=-1)
                pe_p, pe_s = _split_rope_cols(kv_w_dkv[:, KV_LORA:])
                w_dkv_p = jnp.concatenate([kv_w_dkv[:, :KV_LORA], pe_p, pe_s], axis=1).astype(BF16)
                w_ukv = kv_w_ukv.reshape(KV_LORA, MLA_HEADS, MLA_NOPE + MLA_V)
                w_k = w_ukv[:, :, :MLA_NOPE].reshape(KV_LORA, MLA_HEADS * MLA_NOPE).astype(BF16)
                w_v = w_ukv[:, :, MLA_NOPE:].reshape(KV_LORA, MLA_HEADS * MLA_V).astype(BF16)
                kg_n, kg_r = _pad_rope_gain(kv_k_norm)
                k_sh, v_sh = _mla_kv(xf, kv_norm, kv_shift, kv_scale, w_dkv_p, kv_lat_norm, w_k, w_v,
                                     kg_n, kg_r, cos4, sin4, batch, seq, t_attn)
            q_lora = mla_w_dq.shape[2]
            w_uq = mla_w_uq[j].reshape(q_lora, MLA_HEADS, MLA_NOPE + MLA_ROPE)
            w_qn = w_uq[:, :, :MLA_NOPE].reshape(q_lora, MLA_HEADS * MLA_NOPE).astype(BF16)
            r_p, r_s = _split_rope_cols(w_uq[:, :, MLA_NOPE:])
            w_qr = r_p.reshape(q_lora, MLA_HEADS * LANE).astype(BF16)
            w_qs = r_s.reshape(q_lora, MLA_HEADS * LANE).astype(BF16)
            qg_n, qg_r = _pad_rope_gain(mla_q_norm[j])
            q = _mla_q(xf, norm_mix[layer], shift_m, scale_m, mla_w_dq[j].astype(BF16), mla_q_lat_norm[j],
                       w_qn, w_qr, w_qs, qg_n, qg_r, cos4, sin4, batch, seq, tm_mid)
            score_bound = ((MLA_NOPE + MLA_ROPE) ** 0.5 * LOG2_E
                           * jnp.max(jnp.abs(mla_q_norm[j])) * jnp.max(jnp.abs(kv_k_norm)))
            mix_in = lax.cond(score_bound <= ATTN_MAX_SCORE_BOUND,
                              lambda: _attention(q, k_sh, v_sh, t_attn, False),
                              lambda: _attention(q, k_sh, v_sh, t_attn, True))
            mix_in = mix_in.reshape(t, MLA_HEADS * MLA_V)
            w_out = mla_w_out[j].astype(BF16)
        xf = _matmul_residual(mix_in, w_out, xf, gate_m, seq, tm_mid, d)
        xf = _ffn(xf, norm_ffn[layer], shift_f, scale_f, gate_f, w_gu_bf, w_down_bf, layer, seq, tm_big, 512)
    return xf.reshape(batch, seq, d)
```

```python
import functools

import jax
import jax.numpy as jnp
from jax import lax
from jax.experimental import pallas as pl
from jax.experimental.pallas import tpu as pltpu

GLA_HEADS = 4
GLA_GATE_RANK = 16
GLA_TAU = 16.0
GLA_CHUNK = 64
MLA_HEADS = 16
MLA_NOPE = 128
MLA_ROPE = 64
MLA_V = 128
KV_LORA = 512
ROPE_THETA = 10000.0
EPS = 1e-6
LOG2_E = 1.4426950408889634

LANE = 128
MIB = 1 << 20

F32 = jnp.float32
BF16 = jnp.bfloat16


def _params(semantics, vmem_mib):
    return pltpu.CompilerParams(dimension_semantics=semantics, vmem_limit_bytes=vmem_mib * MIB)


def _silu(x):
    return x * jax.nn.sigmoid(x)


def _rms_scale(x):
    return x * lax.rsqrt(jnp.mean(x * x, axis=-1, keepdims=True) + EPS)


ADA_ROWS = 16


def _split_bf16(x):
    hi = x.astype(BF16)
    return hi, (x - hi.astype(F32)).astype(BF16)


def _ada_kernel(c_ref, w_ref, b_ref, o_ref):
    s_hi, s_lo = _split_bf16(_silu(c_ref[...]))
    w_hi, w_lo = _split_bf16(w_ref[...])
    acc = jnp.dot(s_hi, w_lo, preferred_element_type=F32) + jnp.dot(s_lo, w_hi, preferred_element_type=F32)
    o_ref[...] = (acc + jnp.dot(s_hi, w_hi, preferred_element_type=F32)) + b_ref[...]


def _ada_mod(c_pad, w, b, layer, tn=1024):
    rows, d = c_pad.shape
    n = w.shape[2]
    return pl.pallas_call(
        _ada_kernel,
        out_shape=jax.ShapeDtypeStruct((rows, n), F32),
        grid=(n // tn,),
        in_specs=[
            pl.BlockSpec((rows, d), lambda j: (0, 0)),
            pl.BlockSpec((None, d, tn), lambda j: (layer, 0, j)),
            pl.BlockSpec((None, 1, tn), lambda j: (layer, 0, j)),
        ],
        out_specs=pl.BlockSpec((rows, tn), lambda j: (0, j)),
        compiler_params=_params(("parallel",), 48),
        name="ada_mod",
    )(c_pad, w, b[:, None, :])


def _trig_kernel(a_ref, c_ref, s_ref):
    a = a_ref[...]
    c_ref[...] = jnp.cos(a)
    s_ref[...] = jnp.sin(a)


def _trig(ang):
    rows = ang.shape[0]
    tr = min(rows, 512)
    spec = pl.BlockSpec((tr, LANE), lambda i: (i, 0))
    return pl.pallas_call(
        _trig_kernel,
        out_shape=(jax.ShapeDtypeStruct(ang.shape, F32),) * 2,
        grid=(rows // tr,),
        in_specs=[spec],
        out_specs=(spec, spec),
        compiler_params=_params(("parallel",), 16),
        name="rope_trig",
    )(ang)


MOD_ROWS = 16


def _modulate_into(h_ref, x_ref, g_ref, sh_ref, sc_ref):
    tm = x_ref.shape[0]
    step = min(tm, MOD_ROWS)
    gs = g_ref[...] * (1.0 + sc_ref[0])
    shift = sh_ref[0]
    for r in range(tm // step):
        rows = slice(r * step, (r + 1) * step)
        h_ref[rows, :] = (_rms_scale(x_ref[rows, :]) * gs + shift).astype(h_ref.dtype)


def _modmm_kernel(x_ref, g_ref, sh_ref, sc_ref, w_ref, wa_ref, o_ref, a_ref, h_ref):
    @pl.when(pl.program_id(1) == 0)
    def _():
        _modulate_into(h_ref, x_ref, g_ref, sh_ref, sc_ref)
        a_ref[...] = jnp.dot(h_ref[...], wa_ref[...], preferred_element_type=F32).astype(a_ref.dtype)

    o_ref[...] = jnp.dot(h_ref[...], w_ref[...], preferred_element_type=F32).astype(o_ref.dtype)


def _mod_matmul(x, gain, shift, scale, w, layer, n, w_a, seq, tm, tn):
    t, d = x.shape
    per_b = seq // tm
    vec = lambda i, j: (i // per_b, 0, 0)
    return pl.pallas_call(
        _modmm_kernel,
        out_shape=(jax.ShapeDtypeStruct((t, n), BF16), jax.ShapeDtypeStruct((t, LANE), BF16)),
        grid=(t // tm, n // tn),
        in_specs=[
            pl.BlockSpec((tm, d), lambda i, j: (i, 0)),
            pl.BlockSpec((1, d), lambda i, j: (0, 0)),
            pl.BlockSpec((1, 1, d), vec),
            pl.BlockSpec((1, 1, d), vec),
            pl.BlockSpec((None, d, tn), lambda i, j: (layer, 0, j)),
            pl.BlockSpec((d, LANE), lambda i, j: (0, 0)),
        ],
        out_specs=(pl.BlockSpec((tm, tn), lambda i, j: (i, j)),
                   pl.BlockSpec((tm, LANE), lambda i, j: (i, 0))),
        scratch_shapes=[pltpu.VMEM((tm, d), BF16)],
        compiler_params=_params(("parallel", "arbitrary"), 56),
        name="mod_matmul",
    )(x, gain.reshape(1, d), shift[:, None, :], scale[:, None, :], w, w_a)


def _mm_res_kernel(a_ref, w_ref, x_ref, gate_ref, o_ref):
    acc = jnp.dot(a_ref[...], w_ref[...], preferred_element_type=F32)
    o_ref[...] = x_ref[...] + gate_ref[0] * acc


def _matmul_residual(a, w, x, gate, seq, tm, tn):
    t, k = a.shape
    n = w.shape[1]
    per_b = seq // tm
    return pl.pallas_call(
        _mm_res_kernel,
        out_shape=jax.ShapeDtypeStruct((t, n), F32),
        grid=(t // tm, n // tn),
        in_specs=[
            pl.BlockSpec((tm, k), lambda i, j: (i, 0)),
            pl.BlockSpec((k, tn), lambda i, j: (0, j)),
            pl.BlockSpec((tm, tn), lambda i, j: (i, j)),
            pl.BlockSpec((1, 1, tn), lambda i, j: (i // per_b, 0, j)),
        ],
        out_specs=pl.BlockSpec((tm, tn), lambda i, j: (i, j)),
        compiler_params=_params(("parallel", "arbitrary"), 48),
        name="matmul_residual",
    )(a, w, x, gate[:, None, :])


FFN_BLOCKS_PER_STEP = 2
FFN_ROW_SPLIT = 2


def _ffn_kernel(x_hbm, g_ref, sh_ref, sc_ref, gate_ref, wga_ref, wua_ref, wda_ref, wgb_ref, wub_ref, wdb_ref,
                o_ref, xs_ref, h_ref, sem, *, nf):
    i, f = pl.program_id(0), pl.program_id(1)
    tm = xs_ref.shape[0]

    def x_copy(tile):
        return pltpu.make_async_copy(x_hbm.at[pl.ds(pl.multiple_of(tile * tm, tm), tm), :], xs_ref, sem)

    @pl.when(f == 0)
    def _():
        @pl.when(i == 0)
        def _():
            x_copy(0).start()

        x_copy(i).wait()
        _modulate_into(h_ref, xs_ref, g_ref, sh_ref, sc_ref)
        o_ref[...] = xs_ref[...]

    @pl.when(jnp.logical_and(f == 1, i + 1 < pl.num_programs(0)))
    def _():
        x_copy(i + 1).start()

    def block(wg_ref, wu_ref, wd_ref):
        for r in range(FFN_ROW_SPLIT):
            rows = slice(r * tm // FFN_ROW_SPLIT, (r + 1) * tm // FFN_ROW_SPLIT)
            h = h_ref[rows, :]
            g = jnp.dot(h, wg_ref[...], preferred_element_type=F32)
            u = jnp.dot(h, wu_ref[...], preferred_element_type=F32)
            a = (_silu(g) * u).astype(BF16)
            o_ref[rows, :] += gate_ref[0] * jnp.dot(a, wd_ref[...], preferred_element_type=F32)

    block(wga_ref, wua_ref, wda_ref)

    @pl.when(FFN_BLOCKS_PER_STEP * f + 1 < nf)
    def _():
        block(wgb_ref, wub_ref, wdb_ref)


def _ffn(x, gain, shift, scale, gate, w_gu, w_down, layer, seq, tm, tf):
    t, d = x.shape
    dff = w_down.shape[1]
    nf = dff // tf
    per_b = seq // tm
    vec = lambda i, f: (i // per_b, 0, 0)
    blk_a = lambda f: FFN_BLOCKS_PER_STEP * f
    blk_b = lambda f: jnp.minimum(FFN_BLOCKS_PER_STEP * f + 1, nf - 1)
    w_specs = []
    for blk in (blk_a, blk_b):
        w_specs += [
            pl.BlockSpec((None, d, tf), lambda i, f, blk=blk: (layer, 0, blk(f))),
            pl.BlockSpec((None, d, tf), lambda i, f, blk=blk: (layer, 0, blk(f) + nf)),
            pl.BlockSpec((None, tf, d), lambda i, f, blk=blk: (layer, blk(f), 0)),
        ]
    return pl.pallas_call(
        functools.partial(_ffn_kernel, nf=nf),
        out_shape=jax.ShapeDtypeStruct((t, d), F32),
        grid=(t // tm, pl.cdiv(nf, FFN_BLOCKS_PER_STEP)),
        in_specs=[
            pl.BlockSpec(memory_space=pl.ANY),
            pl.BlockSpec((1, d), lambda i, f: (0, 0)),
            pl.BlockSpec((1, 1, d), vec),
            pl.BlockSpec((1, 1, d), vec),
            pl.BlockSpec((1, 1, d), vec),
        ] + w_specs,
        out_specs=pl.BlockSpec((tm, d), lambda i, f: (i, 0)),
        scratch_shapes=[pltpu.VMEM((tm, d), F32), pltpu.VMEM((tm, d), BF16), pltpu.SemaphoreType.DMA(())],
        compiler_params=_params(("arbitrary", "arbitrary"), 60),
        name="ffn",
    )(x, gain.reshape(1, d), shift[:, None, :], scale[:, None, :], gate[:, None, :],
      w_gu, w_gu, w_down, w_gu, w_gu, w_down)


GLA_PAIR = 2 * GLA_CHUNK


GLA_BATCH_PER_STEP = 2


def _gla_pair(q_ref, k_ref, v_ref, g_ref, a_ref, wal_ref, bal_ref, on_ref, o_ref, st_ref, bb, dk, dv):
    ch = GLA_CHUNK
    tp = q_ref.shape[1]
    qk = q_ref.shape[2]

    z = jnp.dot(a_ref[bb], wal_ref[...], preferred_element_type=F32) + bal_ref[...]
    la = (jnp.minimum(z, 0.0) - jnp.log(1.0 + jnp.exp(-jnp.abs(z)))) * (LOG2_E / GLA_TAU)

    row = lax.broadcasted_iota(jnp.int32, (tp, tp), 0)
    col = lax.broadcasted_iota(jnp.int32, (tp, tp), 1)
    causal = jnp.logical_and(col <= row, col >= (row // ch) * ch)
    cross = jnp.logical_and(row >= ch, col < ch)
    tri = jnp.where(causal, 1.0, 0.0).astype(BF16)
    first = lax.broadcasted_iota(jnp.int32, (tp, qk), 0) < ch
    first_h = lax.broadcasted_iota(jnp.int32, (tp, dk), 0) < ch

    la_hi = la.astype(BF16)
    la_lo = (la - la_hi.astype(F32)).astype(BF16)
    bcum = jnp.dot(tri, la_hi, preferred_element_type=F32) + jnp.dot(tri, la_lo, preferred_element_type=F32)
    b_last0 = bcum[ch - 1:ch]
    b_last1 = bcum[tp - 1:tp]
    e_q = jnp.exp2(bcum)
    e_ki = jnp.exp2(-bcum)
    e_ks = jnp.exp2(jnp.where(first, b_last0, b_last1) - bcum)
    dec0 = jnp.exp2(b_last0)
    dec1 = jnp.exp2(b_last1)
    dec01 = jnp.exp2(b_last0 + b_last1)
    on = on_ref[...]
    tn_dims = (((0,), (0,)), ((), ()))
    nt_dims = (((1,), (1,)), ((), ()))

    for h in range(GLA_HEADS):
        ks = slice(h * dk, (h + 1) * dk)
        vs = slice(h * dv, (h + 1) * dv)
        q = q_ref[bb, :, ks].astype(F32) * (dk ** -0.5)
        k = k_ref[bb, :, ks].astype(F32)
        v = v_ref[bb, :, vs]
        q_dec = q * e_q[:, ks]
        k_st = k * e_ks[:, ks]
        q_b = q_dec.astype(BF16)
        k_in = (k * e_ki[:, ks]).astype(BF16)
        k0 = jnp.where(first_h, k_st, 0.0).astype(BF16)
        att = lax.dot_general(q_b, k_in, nt_dims, preferred_element_type=F32)
        att_x = lax.dot_general(q_b, k0, nt_dims, preferred_element_type=F32)
        att = jnp.where(causal, att, jnp.where(cross, att_x, 0.0)).astype(BF16)
        s0 = st_ref[bb, h]
        q_s = jnp.where(first_h, q_dec, q_dec * dec0[:, ks]).astype(BF16)
        o = (jnp.dot(att, v, preferred_element_type=F32)
             + jnp.dot(q_s, s0.astype(BF16), preferred_element_type=F32))
        k_u = jnp.where(first_h, k_st * dec1[:, ks], k_st).astype(BF16)
        dec_cols = jnp.tile(jnp.broadcast_to(dec01[:, ks], (LANE, dk)).T, (1, dv // LANE))
        st_ref[bb, h] = dec_cols * s0 + lax.dot_general(k_u, v, tn_dims, preferred_element_type=F32)
        gg = g_ref[bb, :, vs].astype(F32)
        o_ref[bb, :, vs] = ((_rms_scale(o) * on) * _silu(gg)).astype(o_ref.dtype)


def _gla_kernel(q_ref, k_ref, v_ref, g_ref, a_ref, wal_ref, bal_ref, on_ref, o_ref, st_ref, *, dk, dv):
    @pl.when(pl.program_id(1) == 0)
    def _():
        st_ref[...] = jnp.zeros_like(st_ref)

    for bb in range(q_ref.shape[0]):
        _gla_pair(q_ref, k_ref, v_ref, g_ref, a_ref, wal_ref, bal_ref, on_ref, o_ref, st_ref, bb, dk, dv)


def _gla(proj, a_lr, w_alpha_p, b_alpha, onorm, batch, seq):
    dk = w_alpha_p.shape[1] // GLA_HEADS
    dv = onorm.shape[0]
    qk = GLA_HEADS * dk
    vv = GLA_HEADS * dv
    tp = GLA_PAIR
    nb = GLA_BATCH_PER_STEP if batch % GLA_BATCH_PER_STEP == 0 else 1
    return pl.pallas_call(
        functools.partial(_gla_kernel, dk=dk, dv=dv),
        out_shape=jax.ShapeDtypeStruct((batch, seq, vv), BF16),
        grid=(batch // nb, seq // tp),
        in_specs=[
            pl.BlockSpec((nb, tp, qk), lambda b, i: (b, i, 0)),
            pl.BlockSpec((nb, tp, qk), lambda b, i: (b, i, 1)),
            pl.BlockSpec((nb, tp, vv), lambda b, i: (b, i, 2 * qk // vv)),
            pl.BlockSpec((nb, tp, vv), lambda b, i: (b, i, 2 * qk // vv + 1)),
            pl.BlockSpec((nb, tp, LANE), lambda b, i: (b, i, 0)),
            pl.BlockSpec((LANE, qk), lambda b, i: (0, 0)),
            pl.BlockSpec((1, qk), lambda b, i: (0, 0)),
            pl.BlockSpec((1, dv), lambda b, i: (0, 0)),
        ],
        out_specs=pl.BlockSpec((nb, tp, vv), lambda b, i: (b, i, 0)),
        scratch_shapes=[pltpu.VMEM((nb, GLA_HEADS, dk, dv), F32)],
        compiler_params=_params(("parallel", "arbitrary"), 56),
        name="gla",
    )(proj, proj, proj, proj, a_lr, w_alpha_p, b_alpha.reshape(1, qk), onorm.reshape(1, dv))


def _rope_pair(x, x_swapped, cos4, sin4):
    return x * cos4 + x_swapped * sin4


def _head_norm_store(o_ref, h, rows, nope, rope_sq, rope_g, gain_n, dim, post_scale):
    ss = jnp.sum(nope * nope + rope_sq, axis=-1, keepdims=True)
    r = lax.rsqrt(ss / dim + EPS)
    if post_scale != 1.0:
        r = r * post_scale
    o_ref[0, h, rows, 0:LANE] = ((nope * r) * gain_n).astype(o_ref.dtype)
    o_ref[0, h, rows, LANE:2 * LANE] = (rope_g * r).astype(o_ref.dtype)


MLA_SUB = 256


def _sub_tiles(tm):
    step = min(tm, MLA_SUB)
    return [slice(r * step, (r + 1) * step) for r in range(tm // step)]


def _mla_kv_kernel(x_ref, g_ref, sh_ref, sc_ref, wd_ref, ln_ref, wk_ref, wv_ref, kn_ref, kr_ref,
                   cos_ref, sin_ref, k_ref, v_ref, h_ref):
    lora = ln_ref.shape[1]
    gn = kn_ref[...]
    gr = kr_ref[...]
    _modulate_into(h_ref, x_ref, g_ref, sh_ref, sc_ref)
    for rows in _sub_tiles(x_ref.shape[0]):
        ckv = jnp.dot(h_ref[rows, :], wd_ref[...], preferred_element_type=F32)
        c_lat = (_rms_scale(ckv[:, :lora]) * ln_ref[...]).astype(BF16)
        k_pe = _rope_pair(ckv[:, lora:lora + LANE], ckv[:, lora + LANE:lora + 2 * LANE],
                          cos_ref[rows, :], sin_ref[rows, :])
        kn = jnp.dot(c_lat, wk_ref[...], preferred_element_type=F32)
        vv = jnp.dot(c_lat, wv_ref[...], preferred_element_type=F32)
        pe_sq = k_pe * k_pe
        pe_g = k_pe * gr
        for h in range(MLA_HEADS):
            cols = slice(h * LANE, (h + 1) * LANE)
            _head_norm_store(k_ref, h, rows, kn[:, cols], pe_sq, pe_g, gn, MLA_NOPE + MLA_ROPE, 1.0)
            v_ref[0, h, 0, :, rows] = vv[:, cols].T.astype(v_ref.dtype)


def _mla_q_kernel(x_ref, g_ref, sh_ref, sc_ref, wd_ref, ln_ref, wn_ref, wr_ref, ws_ref, qn_ref, qr_ref,
                  cos_ref, sin_ref, q_ref, h_ref):
    gn = qn_ref[...]
    gr = qr_ref[...]
    sm_scale = (MLA_NOPE + MLA_ROPE) ** -0.5 * LOG2_E
    _modulate_into(h_ref, x_ref, g_ref, sh_ref, sc_ref)
    for rows in _sub_tiles(x_ref.shape[0]):
        cq = jnp.dot(h_ref[rows, :], wd_ref[...], preferred_element_type=F32)
        cq = (_rms_scale(cq) * ln_ref[...]).astype(BF16)
        qn = jnp.dot(cq, wn_ref[...], preferred_element_type=F32)
        qr = jnp.dot(cq, wr_ref[...], preferred_element_type=F32)
        qs = jnp.dot(cq, ws_ref[...], preferred_element_type=F32)
        cos4 = cos_ref[rows, :]
        sin4 = sin_ref[rows, :]
        for h in range(MLA_HEADS):
            cols = slice(h * LANE, (h + 1) * LANE)
            rope = _rope_pair(qr[:, cols], qs[:, cols], cos4, sin4)
            _head_norm_store(q_ref, h, rows, qn[:, cols], rope * rope, rope * gr, gn, MLA_NOPE + MLA_ROPE, sm_scale)


def _const_spec(arr):
    nd = arr.ndim
    return pl.BlockSpec(arr.shape, lambda b, i: (0,) * nd)


def _mla_kv(x, gain, shift, scale, w_dkv_p, lat_norm, w_k, w_v, kg_n, kg_r, cos4, sin4, batch, seq, tm):
    t, d = x.shape
    nt = seq // tm
    row = lambda b, i: (b * nt + i, 0)
    vec = lambda b, i: (b, 0, 0)
    g2 = gain.reshape(1, d)
    weights = [w_dkv_p, lat_norm.reshape(1, -1), w_k, w_v, kg_n, kg_r]
    hk = pl.BlockSpec((1, MLA_HEADS, tm, 2 * LANE), lambda b, i: (b, 0, i, 0))
    hv = pl.BlockSpec((1, MLA_HEADS, 1, MLA_V, tm), lambda b, i: (b, 0, i, 0, 0))
    return pl.pallas_call(
        _mla_kv_kernel,
        out_shape=(jax.ShapeDtypeStruct((batch, MLA_HEADS, seq, 2 * LANE), BF16),
                   jax.ShapeDtypeStruct((batch, MLA_HEADS, nt, MLA_V, tm), BF16)),
        grid=(batch, nt),
        in_specs=[pl.BlockSpec((tm, d), row), _const_spec(g2),
                  pl.BlockSpec((1, 1, d), vec), pl.BlockSpec((1, 1, d), vec)]
                 + [_const_spec(w) for w in weights]
                 + [pl.BlockSpec((tm, LANE), row), pl.BlockSpec((tm, LANE), row)],
        out_specs=(hk, hv),
        scratch_shapes=[pltpu.VMEM((tm, d), BF16)],
        compiler_params=_params(("parallel", "parallel"), 48),
        name="mla_kv",
    )(x, g2, shift[:, None, :], scale[:, None, :], *weights, cos4, sin4)


def _mla_q(x, gain, shift, scale, w_dq, q_lat_norm, w_qn, w_qr, w_qs, qg_n, qg_r, cos4, sin4, batch, seq, tm):
    t, d = x.shape
    nt = seq // tm
    row = lambda b, i: (b * nt + i, 0)
    vec = lambda b, i: (b, 0, 0)
    g2 = gain.reshape(1, d)
    weights = [w_dq, q_lat_norm.reshape(1, -1), w_qn, w_qr, w_qs, qg_n, qg_r]
    hq = pl.BlockSpec((1, MLA_HEADS, tm, 2 * LANE), lambda b, i: (b, 0, i, 0))
    return pl.pallas_call(
        _mla_q_kernel,
        out_shape=jax.ShapeDtypeStruct((batch, MLA_HEADS, seq, 2 * LANE), BF16),
        grid=(batch, nt),
        in_specs=[pl.BlockSpec((tm, d), row), _const_spec(g2),
                  pl.BlockSpec((1, 1, d), vec), pl.BlockSpec((1, 1, d), vec)]
                 + [_const_spec(w) for w in weights]
                 + [pl.BlockSpec((tm, LANE), row), pl.BlockSpec((tm, LANE), row)],
        out_specs=hq,
        scratch_shapes=[pltpu.VMEM((tm, d), BF16)],
        compiler_params=_params(("parallel", "parallel"), 48),
        name="mla_q",
    )(x, g2, shift[:, None, :], scale[:, None, :], *weights, cos4, sin4)


ATTN_HEADS_PER_STEP = 4
ATTN_Q_BLOCKS_PER_STEP = 4
ATTN_MAX_SCORE_BOUND = 60.0


def _attn_kernel(q_ref, k_ref, vt_ref, o_ref, *, tq, running_max):
    first_pair = pl.program_id(2) * (ATTN_Q_BLOCKS_PER_STEP // 2)
    nh = q_ref.shape[1]
    tc = vt_ref.shape[-1]
    dv = vt_ref.shape[-2]
    nt_dims = (((1,), (1,)), ((), ()))

    def step(rows, blk0, nblk, carry, masked):
        tk = nblk * tq
        scores = []
        for h in range(nh):
            k = k_ref[0, h, pl.ds(pl.multiple_of(blk0 * tq, tq), tk), :]
            scores.append(lax.dot_general(k, q_ref[0, h, rows, :], nt_dims, preferred_element_type=F32))
        out = []
        for h in range(nh):
            st = scores[h]
            if masked:
                kv_i = lax.broadcasted_iota(jnp.int32, (tk, tq), 0)
                q_i = lax.broadcasted_iota(jnp.int32, (tk, tq), 1)
                st = jnp.where(kv_i - (nblk - 1) * tq <= q_i, st, -jnp.inf)
            if running_max:
                m, l, acc = carry[h]
                m_new = jnp.maximum(m, jnp.max(st, axis=0, keepdims=True))
                alpha = jnp.exp2(m - m_new)
                pt = jnp.exp2(st - m_new)
                l = alpha * l
                acc = alpha * acc
            else:
                l, acc = carry[h]
                pt = jnp.exp2(st)
            l = l + jnp.sum(pt, axis=0, keepdims=True)
            pt = pt.astype(BF16)
            slab0 = blk0 * (tq // tc)
            pv = jnp.dot(vt_ref[0, h, slab0], pt[0:tc], preferred_element_type=F32)
            for c in range(1, tk // tc):
                pv += jnp.dot(vt_ref[0, h, slab0 + c], pt[c * tc:(c + 1) * tc], preferred_element_type=F32)
            out.append((m_new, l, acc + pv) if running_max else (l, acc + pv))
        return tuple(out)

    stats = (jnp.zeros((1, tq), F32), jnp.zeros((dv, tq), F32))
    if running_max:
        stats = (jnp.full((1, tq), -jnp.inf, F32),) + stats
    init = tuple(stats for _ in range(nh))
    for sub in range(ATTN_Q_BLOCKS_PER_STEP):
        rows = slice(sub * tq, (sub + 1) * tq)
        pairs = first_pair + sub // 2
        carry = lax.fori_loop(0, pairs, lambda j, c: step(rows, 2 * j, 2, c, False), init)
        carry = step(rows, 2 * pairs, sub % 2 + 1, carry, True)
        for h in range(nh):
            l, acc = carry[h][-2:]
            o_ref[0, rows, h * dv:(h + 1) * dv] = (acc / l).T.astype(o_ref.dtype)


def _attention(q, k, vt, tq, running_max):
    b, h, s, dq = q.shape
    _, _, nslab, dvv, tc = vt.shape
    nh = ATTN_HEADS_PER_STEP
    return pl.pallas_call(
        functools.partial(_attn_kernel, tq=tq, running_max=running_max),
        out_shape=jax.ShapeDtypeStruct((b, s, h * dvv), BF16),
        grid=(b, h // nh, s // (ATTN_Q_BLOCKS_PER_STEP * tq)),
        in_specs=[
            pl.BlockSpec((1, nh, ATTN_Q_BLOCKS_PER_STEP * tq, dq), lambda bi, hi, qi: (bi, hi, qi, 0)),
            pl.BlockSpec((1, nh, s, dq), lambda bi, hi, qi: (bi, hi, 0, 0)),
            pl.BlockSpec((1, nh, nslab, dvv, tc), lambda bi, hi, qi: (bi, hi, 0, 0, 0)),
        ],
        out_specs=pl.BlockSpec((1, ATTN_Q_BLOCKS_PER_STEP * tq, nh * dvv), lambda bi, hi, qi: (bi, qi, hi)),
        compiler_params=_params(("parallel", "parallel", "arbitrary"), 58),
        name="mla_attention" if running_max else "mla_attention_bounded",
    )(q, k, vt)


def _rope_tables(positions):
    b, s = positions.shape
    half = MLA_ROPE // 2
    inv_freq = ROPE_THETA ** (-jnp.arange(half, dtype=F32) / half)
    ang = positions.astype(F32)[..., None] * inv_freq
    cos, sin = _trig(ang.reshape(-1, LANE))
    cos = cos.reshape(b * s, half)
    sin = sin.reshape(b * s, half)
    zero = jnp.zeros((b * s, LANE - 2 * half), F32)
    cos4 = jnp.concatenate([cos, cos, zero], axis=-1)
    sin4 = jnp.concatenate([-sin, sin, zero], axis=-1)
    return cos4, sin4


def _split_rope_cols(w_rope):
    half = MLA_ROPE // 2
    x1, x2 = w_rope[..., :half], w_rope[..., half:]
    zero = jnp.zeros(w_rope.shape[:-1] + (LANE - MLA_ROPE,), w_rope.dtype)
    return jnp.concatenate([x1, x2, zero], axis=-1), jnp.concatenate([x2, x1, zero], axis=-1)


def _pad_rope_gain(gain):
    g_n = gain[:MLA_NOPE].reshape(1, MLA_NOPE)
    g_r = jnp.concatenate([gain[MLA_NOPE:], jnp.zeros((LANE - MLA_ROPE,), gain.dtype)]).reshape(1, LANE)
    return g_n, g_r


def kernel(x, c, positions, ada_w, ada_b, norm_mix, norm_ffn, gla_w_in, gla_w_alpha, gla_b_alpha, gla_onorm,
           gla_w_out, mla_w_dq, mla_q_lat_norm, mla_w_uq, mla_q_norm, mla_w_out, kv_ada_w, kv_ada_b, kv_norm,
           kv_w_dkv, kv_lat_norm, kv_w_ukv, kv_k_norm, ffn_w_gu, ffn_w_down):
    batch, seq, d = x.shape
    t = batch * seq
    depth = ada_w.shape[0]
    n_gla = gla_w_in.shape[0]
    xf = x.reshape(t, d)

    tm_big = min(seq, 1024)
    tm_mid = min(seq, 512)
    t_attn = min(tm_mid, seq // ATTN_Q_BLOCKS_PER_STEP)

    c_pad = jnp.pad(c, ((0, ADA_ROWS - batch), (0, 0)))
    cos4, sin4 = _rope_tables(positions)
    w_gu_bf = ffn_w_gu.astype(BF16)
    w_down_bf = ffn_w_down.astype(BF16)

    k_sh = v_sh = None
    for layer in range(depth):
        mod = _ada_mod(c_pad, ada_w, ada_b, layer)[:batch]
        shift_m, scale_m, gate_m, shift_f, scale_f, gate_f = jnp.split(mod, 6, axis=-1)
        if layer < n_gla:
            qkvg = gla_w_in.shape[2] - GLA_GATE_RANK
            w_a = jnp.pad(gla_w_in[layer, :, qkvg:], ((0, 0), (0, LANE - GLA_GATE_RANK))).astype(BF16)
            proj, a_lr = _mod_matmul(xf, norm_mix[layer], shift_m, scale_m, gla_w_in.astype(BF16), layer, qkvg,
                                     w_a, seq, tm_big, 2048)
            w_alpha_p = jnp.pad(gla_w_alpha[layer], ((0, LANE - GLA_GATE_RANK), (0, 0))).astype(BF16)
            mix_in = _gla(proj.reshape(batch, seq, qkvg), a_lr.reshape(batch, seq, LANE), w_alpha_p,
                          gla_b_alpha[layer], gla_onorm[layer], batch, seq).reshape(t, -1)
            w_out = gla_w_out[layer].astype(BF16)
        else:
            j = layer - n_gla
            if j == 0:
                kv_mod = _ada_mod(c_pad, kv_ada_w[None], kv_ada_b[None], 0)[:batch]
                kv_shift, kv_scale = jnp.split(kv_mod, 2, axis=-1)
                pe_p, pe_s = _split_rope_cols(kv_w_dkv[:, KV_LORA:])
                w_dkv_p = jnp.concatenate([kv_w_dkv[:, :KV_LORA], pe_p, pe_s], axis=1).astype(BF16)
                w_ukv = kv_w_ukv.reshape(KV_LORA, MLA_HEADS, MLA_NOPE + MLA_V)
                w_k = w_ukv[:, :, :MLA_NOPE].reshape(KV_LORA, MLA_HEADS * MLA_NOPE).astype(BF16)
                w_v = w_ukv[:, :, MLA_NOPE:].reshape(KV_LORA, MLA_HEADS * MLA_V).astype(BF16)
                kg_n, kg_r = _pad_rope_gain(kv_k_norm)
                k_sh, v_sh = _mla_kv(xf, kv_norm, kv_shift, kv_scale, w_dkv_p, kv_lat_norm, w_k, w_v,
                                     kg_n, kg_r, cos4, sin4, batch, seq, t_attn)
            q_lora = mla_w_dq.shape[2]
            w_uq = mla_w_uq[j].reshape(q_lora, MLA_HEADS, MLA_NOPE + MLA_ROPE)
            w_qn = w_uq[:, :, :MLA_NOPE].reshape(q_lora, MLA_HEADS * MLA_NOPE).astype(BF16)
            r_p, r_s = _split_rope_cols(w_uq[:, :, MLA_NOPE:])
            w_qr = r_p.reshape(q_lora, MLA_HEADS * LANE).astype(BF16)
            w_qs = r_s.reshape(q_lora, MLA_HEADS * LANE).astype(BF16)
            qg_n, qg_r = _pad_rope_gain(mla_q_norm[j])
            q = _mla_q(xf, norm_mix[layer], shift_m, scale_m, mla_w_dq[j].astype(BF16), mla_q_lat_norm[j],
                       w_qn, w_qr, w_qs, qg_n, qg_r, cos4, sin4, batch, seq, tm_mid)
            score_bound = ((MLA_NOPE + MLA_ROPE) ** 0.5 * LOG2_E
                           * jnp.max(jnp.abs(mla_q_norm[j])) * jnp.max(jnp.abs(kv_k_norm)))
            mix_in = lax.cond(score_bound <= ATTN_MAX_SCORE_BOUND,
                              lambda: _attention(q, k_sh, v_sh, t_attn, False),
                              lambda: _attention(q, k_sh, v_sh, t_attn, True))
            mix_in = mix_in.reshape(t, MLA_HEADS * MLA_V)
            w_out = mla_w_out[j].astype(BF16)
        xf = _matmul_residual(mix_in, w_out, xf, gate_m, seq, tm_mid, d)
        xf = _ffn(xf, norm_ffn[layer], shift_f, scale_f, gate_f, w_gu_bf, w_down_bf, layer, seq, tm_big, 512)
    return xf.reshape(batch, seq, d)
```

```python
import functools

import jax
import jax.numpy as jnp
from jax import lax
from jax.experimental import pallas as pl
from jax.experimental.pallas import tpu as pltpu

GLA_HEADS = 4
GLA_GATE_RANK = 16
GLA_TAU = 16.0
GLA_CHUNK = 64
MLA_HEADS = 16
MLA_NOPE = 128
MLA_ROPE = 64
MLA_V = 128
KV_LORA = 512
ROPE_THETA = 10000.0
EPS = 1e-6
LOG2_E = 1.4426950408889634

LANE = 128
MIB = 1 << 20

F32 = jnp.float32
BF16 = jnp.bfloat16


def _params(semantics, vmem_mib):
    return pltpu.CompilerParams(dimension_semantics=semantics, vmem_limit_bytes=vmem_mib * MIB)


def _silu(x):
    return x * jax.nn.sigmoid(x)


def _rms_scale(x):
    return x * lax.rsqrt(jnp.mean(x * x, axis=-1, keepdims=True) + EPS)


ADA_ROWS = 16


def _split_bf16(x):
    hi = x.astype(BF16)
    return hi, (x - hi.astype(F32)).astype(BF16)


def _ada_kernel(c_ref, w_ref, b_ref, o_ref):
    s_hi, s_lo = _split_bf16(_silu(c_ref[...]))
    w_hi, w_lo = _split_bf16(w_ref[...])
    acc = jnp.dot(s_hi, w_lo, preferred_element_type=F32) + jnp.dot(s_lo, w_hi, preferred_element_type=F32)
    o_ref[...] = (acc + jnp.dot(s_hi, w_hi, preferred_element_type=F32)) + b_ref[...]


def _ada_mod(c_pad, w, b, layer, tn=1024):
    rows, d = c_pad.shape
    n = w.shape[2]
    return pl.pallas_call(
        _ada_kernel,
        out_shape=jax.ShapeDtypeStruct((rows, n), F32),
        grid=(n // tn,),
        in_specs=[
            pl.BlockSpec((rows, d), lambda j: (0, 0)),
            pl.BlockSpec((None, d, tn), lambda j: (layer, 0, j)),
            pl.BlockSpec((None, 1, tn), lambda j: (layer, 0, j)),
        ],
        out_specs=pl.BlockSpec((rows, tn), lambda j: (0, j)),
        compiler_params=_params(("parallel",), 48),
        name="ada_mod",
    )(c_pad, w, b[:, None, :])


def _trig_kernel(a_ref, c_ref, s_ref):
    a = a_ref[...]
    c_ref[...] = jnp.cos(a)
    s_ref[...] = jnp.sin(a)


def _trig(ang):
    rows = ang.shape[0]
    tr = min(rows, 512)
    spec = pl.BlockSpec((tr, LANE), lambda i: (i, 0))
    return pl.pallas_call(
        _trig_kernel,
        out_shape=(jax.ShapeDtypeStruct(ang.shape, F32),) * 2,
        grid=(rows // tr,),
        in_specs=[spec],
        out_specs=(spec, spec),
        compiler_params=_params(("parallel",), 16),
        name="rope_trig",
    )(ang)


MOD_ROWS = 16


def _modulate_into(h_ref, x_ref, g_ref, sh_ref, sc_ref):
    tm = x_ref.shape[0]
    step = min(tm, MOD_ROWS)
    gs = g_ref[...] * (1.0 + sc_ref[0])
    shift = sh_ref[0]
    for r in range(tm // step):
        rows = slice(r * step, (r + 1) * step)
        h_ref[rows, :] = (_rms_scale(x_ref[rows, :]) * gs + shift).astype(h_ref.dtype)


def _modmm_kernel(x_ref, g_ref, sh_ref, sc_ref, w_ref, wa_ref, o_ref, a_ref, h_ref):
    @pl.when(pl.program_id(1) == 0)
    def _():
        _modulate_into(h_ref, x_ref, g_ref, sh_ref, sc_ref)
        a_ref[...] = jnp.dot(h_ref[...], wa_ref[...], preferred_element_type=F32).astype(a_ref.dtype)

    o_ref[...] = jnp.dot(h_ref[...], w_ref[...], preferred_element_type=F32).astype(o_ref.dtype)


def _mod_matmul(x, gain, shift, scale, w, layer, n, w_a, seq, tm, tn):
    t, d = x.shape
    per_b = seq // tm
    vec = lambda i, j: (i // per_b, 0, 0)
    return pl.pallas_call(
        _modmm_kernel,
        out_shape=(jax.ShapeDtypeStruct((t, n), BF16), jax.ShapeDtypeStruct((t, LANE), BF16)),
        grid=(t // tm, n // tn),
        in_specs=[
            pl.BlockSpec((tm, d), lambda i, j: (i, 0)),
            pl.BlockSpec((1, d), lambda i, j: (0, 0)),
            pl.BlockSpec((1, 1, d), vec),
            pl.BlockSpec((1, 1, d), vec),
            pl.BlockSpec((None, d, tn), lambda i, j: (layer, 0, j)),
            pl.BlockSpec((d, LANE), lambda i, j: (0, 0)),
        ],
        out_specs=(pl.BlockSpec((tm, tn), lambda i, j: (i, j)),
                   pl.BlockSpec((tm, LANE), lambda i, j: (i, 0))),
        scratch_shapes=[pltpu.VMEM((tm, d), BF16)],
        compiler_params=_params(("parallel", "arbitrary"), 56),
        name="mod_matmul",
    )(x, gain.reshape(1, d), shift[:, None, :], scale[:, None, :], w, w_a)


def _mm_res_kernel(a_ref, w_ref, x_ref, gate_ref, o_ref):
    acc = jnp.dot(a_ref[...], w_ref[...], preferred_element_type=F32)
    o_ref[...] = x_ref[...] + gate_ref[0] * acc


def _matmul_residual(a, w, x, gate, seq, tm, tn):
    t, k = a.shape
    n = w.shape[1]
    per_b = seq // tm
    return pl.pallas_call(
        _mm_res_kernel,
        out_shape=jax.ShapeDtypeStruct((t, n), F32),
        grid=(t // tm, n // tn),
        in_specs=[
            pl.BlockSpec((tm, k), lambda i, j: (i, 0)),
            pl.BlockSpec((k, tn), lambda i, j: (0, j), pipeline_mode=pl.Buffered(1 if tn == n else 2)),
            pl.BlockSpec((tm, tn), lambda i, j: (i, j)),
            pl.BlockSpec((1, 1, tn), lambda i, j: (i // per_b, 0, j)),
        ],
        out_specs=pl.BlockSpec((tm, tn), lambda i, j: (i, j)),
        compiler_params=_params(("parallel", "arbitrary"), 56),
        name="matmul_residual",
    )(a, w, x, gate[:, None, :])


FFN_BLOCKS_PER_STEP = 2
FFN_ROW_SPLIT = 2


def _ffn_kernel(x_hbm, g_ref, sh_ref, sc_ref, gate_ref, wga_ref, wua_ref, wda_ref, wgb_ref, wub_ref, wdb_ref,
                o_ref, xs_ref, h_ref, sem, *, nf):
    i, f = pl.program_id(0), pl.program_id(1)
    tm = xs_ref.shape[0]

    def x_copy(tile):
        return pltpu.make_async_copy(x_hbm.at[pl.ds(pl.multiple_of(tile * tm, tm), tm), :], xs_ref, sem)

    @pl.when(f == 0)
    def _():
        @pl.when(i == 0)
        def _():
            x_copy(0).start()

        x_copy(i).wait()
        _modulate_into(h_ref, xs_ref, g_ref, sh_ref, sc_ref)
        o_ref[...] = xs_ref[...]

    @pl.when(jnp.logical_and(f == 1, i + 1 < pl.num_programs(0)))
    def _():
        x_copy(i + 1).start()

    def block(wg_ref, wu_ref, wd_ref):
        for r in range(FFN_ROW_SPLIT):
            rows = slice(r * tm // FFN_ROW_SPLIT, (r + 1) * tm // FFN_ROW_SPLIT)
            h = h_ref[rows, :]
            g = jnp.dot(h, wg_ref[...], preferred_element_type=F32)
            u = jnp.dot(h, wu_ref[...], preferred_element_type=F32)
            a = (_silu(g) * u).astype(BF16)
            o_ref[rows, :] += gate_ref[0] * jnp.dot(a, wd_ref[...], preferred_element_type=F32)

    block(wga_ref, wua_ref, wda_ref)

    @pl.when(FFN_BLOCKS_PER_STEP * f + 1 < nf)
    def _():
        block(wgb_ref, wub_ref, wdb_ref)


def _ffn(x, gain, shift, scale, gate, w_gu, w_down, layer, seq, tm, tf):
    t, d = x.shape
    dff = w_down.shape[1]
    nf = dff // tf
    per_b = seq // tm
    vec = lambda i, f: (i // per_b, 0, 0)
    blk_a = lambda f: FFN_BLOCKS_PER_STEP * f
    blk_b = lambda f: jnp.minimum(FFN_BLOCKS_PER_STEP * f + 1, nf - 1)
    w_specs = []
    for blk in (blk_a, blk_b):
        w_specs += [
            pl.BlockSpec((None, d, tf), lambda i, f, blk=blk: (layer, 0, blk(f))),
            pl.BlockSpec((None, d, tf), lambda i, f, blk=blk: (layer, 0, blk(f) + nf)),
            pl.BlockSpec((None, tf, d), lambda i, f, blk=blk: (layer, blk(f), 0)),
        ]
    return pl.pallas_call(
        functools.partial(_ffn_kernel, nf=nf),
        out_shape=jax.ShapeDtypeStruct((t, d), F32),
        grid=(t // tm, pl.cdiv(nf, FFN_BLOCKS_PER_STEP)),
        in_specs=[
            pl.BlockSpec(memory_space=pl.ANY),
            pl.BlockSpec((1, d), lambda i, f: (0, 0)),
            pl.BlockSpec((1, 1, d), vec),
            pl.BlockSpec((1, 1, d), vec),
            pl.BlockSpec((1, 1, d), vec),
        ] + w_specs,
        out_specs=pl.BlockSpec((tm, d), lambda i, f: (i, 0)),
        scratch_shapes=[pltpu.VMEM((tm, d), F32), pltpu.VMEM((tm, d), BF16), pltpu.SemaphoreType.DMA(())],
        compiler_params=_params(("arbitrary", "arbitrary"), 60),
        name="ffn",
    )(x, gain.reshape(1, d), shift[:, None, :], scale[:, None, :], gate[:, None, :],
      w_gu, w_gu, w_down, w_gu, w_gu, w_down)


GLA_PAIR = 2 * GLA_CHUNK


GLA_BATCH_PER_STEP = 2


def _gla_pair(q_ref, k_ref, v_ref, g_ref, a_ref, wal_ref, bal_ref, on_ref, o_ref, st_ref, bb, dk, dv):
    ch = GLA_CHUNK
    tp = q_ref.shape[1]
    qk = q_ref.shape[2]

    z = jnp.dot(a_ref[bb], wal_ref[...], preferred_element_type=F32) + bal_ref[...]
    la = (jnp.minimum(z, 0.0) - jnp.log(1.0 + jnp.exp(-jnp.abs(z)))) * (LOG2_E / GLA_TAU)

    row = lax.broadcasted_iota(jnp.int32, (tp, tp), 0)
    col = lax.broadcasted_iota(jnp.int32, (tp, tp), 1)
    causal = jnp.logical_and(col <= row, col >= (row // ch) * ch)
    cross = jnp.logical_and(row >= ch, col < ch)
    tri = jnp.where(causal, 1.0, 0.0).astype(BF16)
    first = lax.broadcasted_iota(jnp.int32, (tp, qk), 0) < ch
    first_h = lax.broadcasted_iota(jnp.int32, (tp, dk), 0) < ch

    la_hi = la.astype(BF16)
    la_lo = (la - la_hi.astype(F32)).astype(BF16)
    bcum = jnp.dot(tri, la_hi, preferred_element_type=F32) + jnp.dot(tri, la_lo, preferred_element_type=F32)
    b_last0 = bcum[ch - 1:ch]
    b_last1 = bcum[tp - 1:tp]
    e_q = jnp.exp2(bcum)
    e_ki = jnp.exp2(-bcum)
    e_ks = jnp.exp2(jnp.where(first, b_last0, b_last1) - bcum)
    dec0 = jnp.exp2(b_last0)
    dec1 = jnp.exp2(b_last1)
    dec01 = jnp.exp2(b_last0 + b_last1)
    on = on_ref[...]
    tn_dims = (((0,), (0,)), ((), ()))
    nt_dims = (((1,), (1,)), ((), ()))

    for h in range(GLA_HEADS):
        ks = slice(h * dk, (h + 1) * dk)
        vs = slice(h * dv, (h + 1) * dv)
        q = q_ref[bb, :, ks].astype(F32) * (dk ** -0.5)
        k = k_ref[bb, :, ks].astype(F32)
        v = v_ref[bb, :, vs]
        q_dec = q * e_q[:, ks]
        k_st = k * e_ks[:, ks]
        q_b = q_dec.astype(BF16)
        k_in = (k * e_ki[:, ks]).astype(BF16)
        k0 = jnp.where(first_h, k_st, 0.0).astype(BF16)
        att = lax.dot_general(q_b, k_in, nt_dims, preferred_element_type=F32)
        att_x = lax.dot_general(q_b, k0, nt_dims, preferred_element_type=F32)
        att = jnp.where(causal, att, jnp.where(cross, att_x, 0.0)).astype(BF16)
        s0 = st_ref[bb, h]
        q_s = jnp.where(first_h, q_dec, q_dec * dec0[:, ks]).astype(BF16)
        o = (jnp.dot(att, v, preferred_element_type=F32)
             + jnp.dot(q_s, s0.astype(BF16), preferred_element_type=F32))
        k_u = jnp.where(first_h, k_st * dec1[:, ks], k_st).astype(BF16)
        dec_cols = jnp.tile(jnp.broadcast_to(dec01[:, ks], (LANE, dk)).T, (1, dv // LANE))
        st_ref[bb, h] = dec_cols * s0 + lax.dot_general(k_u, v, tn_dims, preferred_element_type=F32)
        gg = g_ref[bb, :, vs].astype(F32)
        o_ref[bb, :, vs] = ((_rms_scale(o) * on) * _silu(gg)).astype(o_ref.dtype)


def _gla_kernel(q_ref, k_ref, v_ref, g_ref, a_ref, wal_ref, bal_ref, on_ref, o_ref, st_ref, *, dk, dv):
    @pl.when(pl.program_id(1) == 0)
    def _():
        st_ref[...] = jnp.zeros_like(st_ref)

    for bb in range(q_ref.shape[0]):
        _gla_pair(q_ref, k_ref, v_ref, g_ref, a_ref, wal_ref, bal_ref, on_ref, o_ref, st_ref, bb, dk, dv)


def _gla(proj, a_lr, w_alpha_p, b_alpha, onorm, batch, seq):
    dk = w_alpha_p.shape[1] // GLA_HEADS
    dv = onorm.shape[0]
    qk = GLA_HEADS * dk
    vv = GLA_HEADS * dv
    tp = GLA_PAIR
    nb = GLA_BATCH_PER_STEP if batch % GLA_BATCH_PER_STEP == 0 else 1
    return pl.pallas_call(
        functools.partial(_gla_kernel, dk=dk, dv=dv),
        out_shape=jax.ShapeDtypeStruct((batch, seq, vv), BF16),
        grid=(batch // nb, seq // tp),
        in_specs=[
            pl.BlockSpec((nb, tp, qk), lambda b, i: (b, i, 0)),
            pl.BlockSpec((nb, tp, qk), lambda b, i: (b, i, 1)),
            pl.BlockSpec((nb, tp, vv), lambda b, i: (b, i, 2 * qk // vv)),
            pl.BlockSpec((nb, tp, vv), lambda b, i: (b, i, 2 * qk // vv + 1)),
            pl.BlockSpec((nb, tp, LANE), lambda b, i: (b, i, 0)),
            pl.BlockSpec((LANE, qk), lambda b, i: (0, 0)),
            pl.BlockSpec((1, qk), lambda b, i: (0, 0)),
            pl.BlockSpec((1, dv), lambda b, i: (0, 0)),
        ],
        out_specs=pl.BlockSpec((nb, tp, vv), lambda b, i: (b, i, 0)),
        scratch_shapes=[pltpu.VMEM((nb, GLA_HEADS, dk, dv), F32)],
        compiler_params=_params(("parallel", "arbitrary"), 56),
        name="gla",
    )(proj, proj, proj, proj, a_lr, w_alpha_p, b_alpha.reshape(1, qk), onorm.reshape(1, dv))


def _rope_pair(x, x_swapped, cos4, sin4):
    return x * cos4 + x_swapped * sin4


def _head_norm_store(o_ref, h, rows, nope, rope_sq, rope_g, gain_n, dim, post_scale):
    ss = jnp.sum(nope * nope + rope_sq, axis=-1, keepdims=True)
    r = lax.rsqrt(ss / dim + EPS)
    if post_scale != 1.0:
        r = r * post_scale
    o_ref[0, h, rows, 0:LANE] = ((nope * r) * gain_n).astype(o_ref.dtype)
    o_ref[0, h, rows, LANE:2 * LANE] = (rope_g * r).astype(o_ref.dtype)


MLA_SUB = 256


def _sub_tiles(tm):
    step = min(tm, MLA_SUB)
    return [slice(r * step, (r + 1) * step) for r in range(tm // step)]


def _mla_kv_kernel(x_ref, g_ref, sh_ref, sc_ref, wd_ref, ln_ref, wk_ref, wv_ref, kn_ref, kr_ref,
                   cos_ref, sin_ref, k_ref, v_ref, h_ref):
    lora = ln_ref.shape[1]
    gn = kn_ref[...]
    gr = kr_ref[...]
    _modulate_into(h_ref, x_ref, g_ref, sh_ref, sc_ref)
    for rows in _sub_tiles(x_ref.shape[0]):
        ckv = jnp.dot(h_ref[rows, :], wd_ref[...], preferred_element_type=F32)
        c_lat = (_rms_scale(ckv[:, :lora]) * ln_ref[...]).astype(BF16)
        k_pe = _rope_pair(ckv[:, lora:lora + LANE], ckv[:, lora + LANE:lora + 2 * LANE],
                          cos_ref[rows, :], sin_ref[rows, :])
        kn = jnp.dot(c_lat, wk_ref[...], preferred_element_type=F32)
        vv = jnp.dot(c_lat, wv_ref[...], preferred_element_type=F32)
        pe_sq = k_pe * k_pe
        pe_g = k_pe * gr
        for h in range(MLA_HEADS):
            cols = slice(h * LANE, (h + 1) * LANE)
            _head_norm_store(k_ref, h, rows, kn[:, cols], pe_sq, pe_g, gn, MLA_NOPE + MLA_ROPE, 1.0)
            v_ref[0, h, 0, :, rows] = vv[:, cols].T.astype(v_ref.dtype)


def _mla_q_kernel(x_ref, g_ref, sh_ref, sc_ref, wd_ref, ln_ref, wn_ref, wr_ref, ws_ref, qn_ref, qr_ref,
                  cos_ref, sin_ref, q_ref, h_ref):
    gn = qn_ref[...]
    gr = qr_ref[...]
    sm_scale = (MLA_NOPE + MLA_ROPE) ** -0.5 * LOG2_E
    _modulate_into(h_ref, x_ref, g_ref, sh_ref, sc_ref)
    for rows in _sub_tiles(x_ref.shape[0]):
        cq = jnp.dot(h_ref[rows, :], wd_ref[...], preferred_element_type=F32)
        cq = (_rms_scale(cq) * ln_ref[...]).astype(BF16)
        qn = jnp.dot(cq, wn_ref[...], preferred_element_type=F32)
        qr = jnp.dot(cq, wr_ref[...], preferred_element_type=F32)
        qs = jnp.dot(cq, ws_ref[...], preferred_element_type=F32)
        cos4 = cos_ref[rows, :]
        sin4 = sin_ref[rows, :]
        for h in range(MLA_HEADS):
            cols = slice(h * LANE, (h + 1) * LANE)
            rope = _rope_pair(qr[:, cols], qs[:, cols], cos4, sin4)
            _head_norm_store(q_ref, h, rows, qn[:, cols], rope * rope, rope * gr, gn, MLA_NOPE + MLA_ROPE, sm_scale)


def _const_spec(arr):
    nd = arr.ndim
    return pl.BlockSpec(arr.shape, lambda b, i: (0,) * nd)


def _mla_kv(x, gain, shift, scale, w_dkv_p, lat_norm, w_k, w_v, kg_n, kg_r, cos4, sin4, batch, seq, tm):
    t, d = x.shape
    nt = seq // tm
    row = lambda b, i: (b * nt + i, 0)
    vec = lambda b, i: (b, 0, 0)
    g2 = gain.reshape(1, d)
    weights = [w_dkv_p, lat_norm.reshape(1, -1), w_k, w_v, kg_n, kg_r]
    hk = pl.BlockSpec((1, MLA_HEADS, tm, 2 * LANE), lambda b, i: (b, 0, i, 0))
    hv = pl.BlockSpec((1, MLA_HEADS, 1, MLA_V, tm), lambda b, i: (b, 0, i, 0, 0))
    return pl.pallas_call(
        _mla_kv_kernel,
        out_shape=(jax.ShapeDtypeStruct((batch, MLA_HEADS, seq, 2 * LANE), BF16),
                   jax.ShapeDtypeStruct((batch, MLA_HEADS, nt, MLA_V, tm), BF16)),
        grid=(batch, nt),
        in_specs=[pl.BlockSpec((tm, d), row), _const_spec(g2),
                  pl.BlockSpec((1, 1, d), vec), pl.BlockSpec((1, 1, d), vec)]
                 + [_const_spec(w) for w in weights]
                 + [pl.BlockSpec((tm, LANE), row), pl.BlockSpec((tm, LANE), row)],
        out_specs=(hk, hv),
        scratch_shapes=[pltpu.VMEM((tm, d), BF16)],
        compiler_params=_params(("parallel", "parallel"), 48),
        name="mla_kv",
    )(x, g2, shift[:, None, :], scale[:, None, :], *weights, cos4, sin4)


def _mla_q(x, gain, shift, scale, w_dq, q_lat_norm, w_qn, w_qr, w_qs, qg_n, qg_r, cos4, sin4, batch, seq, tm):
    t, d = x.shape
    nt = seq // tm
    row = lambda b, i: (b * nt + i, 0)
    vec = lambda b, i: (b, 0, 0)
    g2 = gain.reshape(1, d)
    weights = [w_dq, q_lat_norm.reshape(1, -1), w_qn, w_qr, w_qs, qg_n, qg_r]
    hq = pl.BlockSpec((1, MLA_HEADS, tm, 2 * LANE), lambda b, i: (b, 0, i, 0))
    return pl.pallas_call(
        _mla_q_kernel,
        out_shape=jax.ShapeDtypeStruct((batch, MLA_HEADS, seq, 2 * LANE), BF16),
        grid=(batch, nt),
        in_specs=[pl.BlockSpec((tm, d), row), _const_spec(g2),
                  pl.BlockSpec((1, 1, d), vec), pl.BlockSpec((1, 1, d), vec)]
                 + [_const_spec(w) for w in weights]
                 + [pl.BlockSpec((tm, LANE), row), pl.BlockSpec((tm, LANE), row)],
        out_specs=hq,
        scratch_shapes=[pltpu.VMEM((tm, d), BF16)],
        compiler_params=_params(("parallel", "parallel"), 48),
        name="mla_q",
    )(x, g2, shift[:, None, :], scale[:, None, :], *weights, cos4, sin4)


ATTN_HEADS_PER_STEP = 4
ATTN_Q_BLOCKS_PER_STEP = 2
ATTN_MAX_SCORE_BOUND = 60.0


def _attn_kernel(q_ref, k_ref, vt_ref, o_ref, *, tq, running_max):
    first_pair = pl.program_id(2) * (ATTN_Q_BLOCKS_PER_STEP // 2)
    nh = q_ref.shape[1]
    tc = vt_ref.shape[-1]
    dv = vt_ref.shape[-2]
    nt_dims = (((1,), (1,)), ((), ()))

    def step(rows, blk0, nblk, carry, masked):
        tk = nblk * tq
        scores = []
        for h in range(nh):
            k = k_ref[0, h, pl.ds(pl.multiple_of(blk0 * tq, tq), tk), :]
            scores.append(lax.dot_general(k, q_ref[0, h, rows, :], nt_dims, preferred_element_type=F32))
        out = []
        for h in range(nh):
            st = scores[h]
            if masked:
                kv_i = lax.broadcasted_iota(jnp.int32, (tk, tq), 0)
                q_i = lax.broadcasted_iota(jnp.int32, (tk, tq), 1)
                st = jnp.where(kv_i - (nblk - 1) * tq <= q_i, st, -jnp.inf)
            if running_max:
                m, l, acc = carry[h]
                m_new = jnp.maximum(m, jnp.max(st, axis=0, keepdims=True))
                alpha = jnp.exp2(m - m_new)
                pt = jnp.exp2(st - m_new)
                l = alpha * l
                acc = alpha * acc
            else:
                l, acc = carry[h]
                pt = jnp.exp2(st)
            l = l + jnp.sum(pt, axis=0, keepdims=True)
            pt = pt.astype(BF16)
            slab0 = blk0 * (tq // tc)
            pv = jnp.dot(vt_ref[0, h, slab0], pt[0:tc], preferred_element_type=F32)
            for c in range(1, tk // tc):
                pv += jnp.dot(vt_ref[0, h, slab0 + c], pt[c * tc:(c + 1) * tc], preferred_element_type=F32)
            out.append((m_new, l, acc + pv) if running_max else (l, acc + pv))
        return tuple(out)

    stats = (jnp.zeros((1, tq), F32), jnp.zeros((dv, tq), F32))
    if running_max:
        stats = (jnp.full((1, tq), -jnp.inf, F32),) + stats
    init = tuple(stats for _ in range(nh))
    for sub in range(ATTN_Q_BLOCKS_PER_STEP):
        rows = slice(sub * tq, (sub + 1) * tq)
        pairs = first_pair + sub // 2
        carry = lax.fori_loop(0, pairs, lambda j, c: step(rows, 2 * j, 2, c, False), init)
        carry = step(rows, 2 * pairs, sub % 2 + 1, carry, True)
        for h in range(nh):
            l, acc = carry[h][-2:]
            o_ref[0, rows, h * dv:(h + 1) * dv] = (acc / l).T.astype(o_ref.dtype)


def _attention(q, k, vt, tq, running_max):
    b, h, s, dq = q.shape
    _, _, nslab, dvv, tc = vt.shape
    nh = ATTN_HEADS_PER_STEP
    return pl.pallas_call(
        functools.partial(_attn_kernel, tq=tq, running_max=running_max),
        out_shape=jax.ShapeDtypeStruct((b, s, h * dvv), BF16),
        grid=(b, h // nh, s // (ATTN_Q_BLOCKS_PER_STEP * tq)),
        in_specs=[
            pl.BlockSpec((1, nh, ATTN_Q_BLOCKS_PER_STEP * tq, dq), lambda bi, hi, qi: (bi, hi, qi, 0)),
            pl.BlockSpec((1, nh, s, dq), lambda bi, hi, qi: (bi, hi, 0, 0)),
            pl.BlockSpec((1, nh, nslab, dvv, tc), lambda bi, hi, qi: (bi, hi, 0, 0, 0)),
        ],
        out_specs=pl.BlockSpec((1, ATTN_Q_BLOCKS_PER_STEP * tq, nh * dvv), lambda bi, hi, qi: (bi, qi, hi)),
        compiler_params=_params(("parallel", "parallel", "arbitrary"), 48),
        name="mla_attention" if running_max else "mla_attention_bounded",
    )(q, k, vt)


def _rope_tables(positions):
    b, s = positions.shape
    half = MLA_ROPE // 2
    inv_freq = ROPE_THETA ** (-jnp.arange(half, dtype=F32) / half)
    ang = positions.astype(F32)[..., None] * inv_freq
    cos, sin = _trig(ang.reshape(-1, LANE))
    cos = cos.reshape(b * s, half)
    sin = sin.reshape(b * s, half)
    zero = jnp.zeros((b * s, LANE - 2 * half), F32)
    cos4 = jnp.concatenate([cos, cos, zero], axis=-1)
    sin4 = jnp.concatenate([-sin, sin, zero], axis=-1)
    return cos4, sin4


def _split_rope_cols(w_rope):
    half = MLA_ROPE // 2
    x1, x2 = w_rope[..., :half], w_rope[..., half:]
    zero = jnp.zeros(w_rope.shape[:-1] + (LANE - MLA_ROPE,), w_rope.dtype)
    return jnp.concatenate([x1, x2, zero], axis=-1), jnp.concatenate([x2, x1, zero], axis=-1)


def _pad_rope_gain(gain):
    g_n = gain[:MLA_NOPE].reshape(1, MLA_NOPE)
    g_r = jnp.concatenate([gain[MLA_NOPE:], jnp.zeros((LANE - MLA_ROPE,), gain.dtype)]).reshape(1, LANE)
    return g_n, g_r


def kernel(x, c, positions, ada_w, ada_b, norm_mix, norm_ffn, gla_w_in, gla_w_alpha, gla_b_alpha, gla_onorm,
           gla_w_out, mla_w_dq, mla_q_lat_norm, mla_w_uq, mla_q_norm, mla_w_out, kv_ada_w, kv_ada_b, kv_norm,
           kv_w_dkv, kv_lat_norm, kv_w_ukv, kv_k_norm, ffn_w_gu, ffn_w_down):
    batch, seq, d = x.shape
    t = batch * seq
    depth = ada_w.shape[0]
    n_gla = gla_w_in.shape[0]
    xf = x.reshape(t, d)

    tm_big = min(seq, 1024)
    tm_mid = min(seq, 512)
    t_attn = min(tm_mid, seq // ATTN_Q_BLOCKS_PER_STEP)

    c_pad = jnp.pad(c, ((0, ADA_ROWS - batch), (0, 0)))
    cos4, sin4 = _rope_tables(positions)
    w_gu_bf = ffn_w_gu.astype(BF16)
    w_down_bf = ffn_w_down.astype(BF16)

    k_sh = v_sh = None
    for layer in range(depth):
        mod = _ada_mod(c_pad, ada_w, ada_b, layer)[:batch]
        shift_m, scale_m, gate_m, shift_f, scale_f, gate_f = jnp.split(mod, 6, axis=-1)
        if layer < n_gla:
            qkvg = gla_w_in.shape[2] - GLA_GATE_RANK
            w_a = jnp.pad(gla_w_in[layer, :, qkvg:], ((0, 0), (0, LANE - GLA_GATE_RANK))).astype(BF16)
            proj, a_lr = _mod_matmul(xf, norm_mix[layer], shift_m, scale_m, gla_w_in.astype(BF16), layer, qkvg,
                                     w_a, seq, tm_big, 2048)
            w_alpha_p = jnp.pad(gla_w_alpha[layer], ((0, LANE - GLA_GATE_RANK), (0, 0))).astype(BF16)
            mix_in = _gla(proj.reshape(batch, seq, qkvg), a_lr.reshape(batch, seq, LANE), w_alpha_p,
                          gla_b_alpha[layer], gla_onorm[layer], batch, seq).reshape(t, -1)
            w_out = gla_w_out[layer].astype(BF16)
        else:
            j = layer - n_gla
            if j == 0:
                kv_mod = _ada_mod(c_pad, kv_ada_w[None], kv_ada_b[None], 0)[:batch]
                kv_shift, kv_scale = jnp.split(kv_mod, 2, axis=-1)
                pe_p, pe_s = _split_rope_cols(kv_w_dkv[:, KV_LORA:])
                w_dkv_p = jnp.concatenate([kv_w_dkv[:, :KV_LORA], pe_p, pe_s], axis=1).astype(BF16)
                w_ukv = kv_w_ukv.reshape(KV_LORA, MLA_HEADS, MLA_NOPE + MLA_V)
                w_k = w_ukv[:, :, :MLA_NOPE].reshape(KV_LORA, MLA_HEADS * MLA_NOPE).astype(BF16)
                w_v = w_ukv[:, :, MLA_NOPE:].reshape(KV_LORA, MLA_HEADS * MLA_V).astype(BF16)
                kg_n, kg_r = _pad_rope_gain(kv_k_norm)
                k_sh, v_sh = _mla_kv(xf, kv_norm, kv_shift, kv_scale, w_dkv_p, kv_lat_norm, w_k, w_v,
                                     kg_n, kg_r, cos4, sin4, batch, seq, t_attn)
            q_lora = mla_w_dq.shape[2]
            w_uq = mla_w_uq[j].reshape(q_lora, MLA_HEADS, MLA_NOPE + MLA_ROPE)
            w_qn = w_uq[:, :, :MLA_NOPE].reshape(q_lora, MLA_HEADS * MLA_NOPE).astype(BF16)
            r_p, r_s = _split_rope_cols(w_uq[:, :, MLA_NOPE:])
            w_qr = r_p.reshape(q_lora, MLA_HEADS * LANE).astype(BF16)
            w_qs = r_s.reshape(q_lora, MLA_HEADS * LANE).astype(BF16)
            qg_n, qg_r = _pad_rope_gain(mla_q_norm[j])
            q = _mla_q(xf, norm_mix[layer], shift_m, scale_m, mla_w_dq[j].astype(BF16), mla_q_lat_norm[j],
                       w_qn, w_qr, w_qs, qg_n, qg_r, cos4, sin4, batch, seq, tm_mid)
            score_bound = ((MLA_NOPE + MLA_ROPE) ** 0.5 * LOG2_E
                           * jnp.max(jnp.abs(mla_q_norm[j])) * jnp.max(jnp.abs(kv_k_norm)))
            mix_in = lax.cond(score_bound <= ATTN_MAX_SCORE_BOUND,
                              lambda: _attention(q, k_sh, v_sh, t_attn, False),
                              lambda: _attention(q, k_sh, v_sh, t_attn, True))
            mix_in = mix_in.reshape(t, MLA_HEADS * MLA_V)
            w_out = mla_w_out[j].astype(BF16)
        xf = _matmul_residual(mix_in, w_out, xf, gate_m, seq, tm_big, d)
        xf = _ffn(xf, norm_ffn[layer], shift_f, scale_f, gate_f, w_gu_bf, w_down_bf, layer, seq, tm_big, 512)
    return xf.reshape(batch, seq, d)
```

```python
import functools

import jax
import jax.numpy as jnp
from jax import lax
from jax.experimental import pallas as pl
from jax.experimental.pallas import tpu as pltpu

GLA_HEADS = 4
GLA_GATE_RANK = 16
GLA_TAU = 16.0
GLA_CHUNK = 64
MLA_HEADS = 16
MLA_NOPE = 128
MLA_ROPE = 64
MLA_V = 128
KV_LORA = 512
ROPE_THETA = 10000.0
EPS = 1e-6
LOG2_E = 1.4426950408889634

LANE = 128
MIB = 1 << 20

F32 = jnp.float32
BF16 = jnp.bfloat16


def _params(semantics, vmem_mib):
    return pltpu.CompilerParams(dimension_semantics=semantics, vmem_limit_bytes=vmem_mib * MIB)


def _silu(x):
    return x * jax.nn.sigmoid(x)


def _rms_scale(x):
    return x * lax.rsqrt(jnp.mean(x * x, axis=-1, keepdims=True) + EPS)


ADA_ROWS = 16


def _split_bf16(x):
    hi = x.astype(BF16)
    return hi, (x - hi.astype(F32)).astype(BF16)


def _ada_kernel(c_ref, w_ref, b_ref, o_ref):
    s_hi, s_lo = _split_bf16(_silu(c_ref[...]))
    w_hi, w_lo = _split_bf16(w_ref[...])
    acc = jnp.dot(s_hi, w_lo, preferred_element_type=F32) + jnp.dot(s_lo, w_hi, preferred_element_type=F32)
    o_ref[...] = (acc + jnp.dot(s_hi, w_hi, preferred_element_type=F32)) + b_ref[...]


def _ada_mod(c_pad, w, b, layer, tn=1024):
    rows, d = c_pad.shape
    n = w.shape[2]
    return pl.pallas_call(
        _ada_kernel,
        out_shape=jax.ShapeDtypeStruct((rows, n), F32),
        grid=(n // tn,),
        in_specs=[
            pl.BlockSpec((rows, d), lambda j: (0, 0)),
            pl.BlockSpec((None, d, tn), lambda j: (layer, 0, j)),
            pl.BlockSpec((None, 1, tn), lambda j: (layer, 0, j)),
        ],
        out_specs=pl.BlockSpec((rows, tn), lambda j: (0, j)),
        compiler_params=_params(("parallel",), 48),
        name="ada_mod",
    )(c_pad, w, b[:, None, :])


def _trig_kernel(a_ref, c_ref, s_ref):
    a = a_ref[...]
    c_ref[...] = jnp.cos(a)
    s_ref[...] = jnp.sin(a)


def _trig(ang):
    rows = ang.shape[0]
    tr = min(rows, 512)
    spec = pl.BlockSpec((tr, LANE), lambda i: (i, 0))
    return pl.pallas_call(
        _trig_kernel,
        out_shape=(jax.ShapeDtypeStruct(ang.shape, F32),) * 2,
        grid=(rows // tr,),
        in_specs=[spec],
        out_specs=(spec, spec),
        compiler_params=_params(("parallel",), 16),
        name="rope_trig",
    )(ang)


MOD_ROWS = 16


def _modulate_into(h_ref, x_ref, g_ref, sh_ref, sc_ref):
    tm = x_ref.shape[0]
    step = min(tm, MOD_ROWS)
    gs = g_ref[...] * (1.0 + sc_ref[0])
    shift = sh_ref[0]
    for r in range(tm // step):
        rows = slice(r * step, (r + 1) * step)
        h_ref[rows, :] = (_rms_scale(x_ref[rows, :]) * gs + shift).astype(h_ref.dtype)


def _modmm_kernel(x_ref, g_ref, sh_ref, sc_ref, w_ref, wa_ref, o_ref, a_ref, h_ref):
    @pl.when(pl.program_id(1) == 0)
    def _():
        _modulate_into(h_ref, x_ref, g_ref, sh_ref, sc_ref)
        a_ref[...] = jnp.dot(h_ref[...], wa_ref[...], preferred_element_type=F32).astype(a_ref.dtype)

    o_ref[...] = jnp.dot(h_ref[...], w_ref[...], preferred_element_type=F32).astype(o_ref.dtype)


def _mod_matmul(x, gain, shift, scale, w, layer, n, w_a, seq, tm, tn):
    t, d = x.shape
    per_b = seq // tm
    vec = lambda i, j: (i // per_b, 0, 0)
    return pl.pallas_call(
        _modmm_kernel,
        out_shape=(jax.ShapeDtypeStruct((t, n), BF16), jax.ShapeDtypeStruct((t, LANE), BF16)),
        grid=(t // tm, n // tn),
        in_specs=[
            pl.BlockSpec((tm, d), lambda i, j: (i, 0)),
            pl.BlockSpec((1, d), lambda i, j: (0, 0)),
            pl.BlockSpec((1, 1, d), vec),
            pl.BlockSpec((1, 1, d), vec),
            pl.BlockSpec((None, d, tn), lambda i, j: (layer, 0, j)),
            pl.BlockSpec((d, LANE), lambda i, j: (0, 0)),
        ],
        out_specs=(pl.BlockSpec((tm, tn), lambda i, j: (i, j)),
                   pl.BlockSpec((tm, LANE), lambda i, j: (i, 0))),
        scratch_shapes=[pltpu.VMEM((tm, d), BF16)],
        compiler_params=_params(("parallel", "arbitrary"), 56),
        name="mod_matmul",
    )(x, gain.reshape(1, d), shift[:, None, :], scale[:, None, :], w, w_a)


def _mm_res_kernel(a_ref, w_ref, x_ref, gate_ref, o_ref):
    acc = jnp.dot(a_ref[...], w_ref[...], preferred_element_type=F32)
    o_ref[...] = x_ref[...] + gate_ref[0] * acc


def _matmul_residual(a, w, x, gate, seq, tm, tn):
    t, k = a.shape
    n = w.shape[1]
    per_b = seq // tm
    return pl.pallas_call(
        _mm_res_kernel,
        out_shape=jax.ShapeDtypeStruct((t, n), F32),
        grid=(t // tm, n // tn),
        in_specs=[
            pl.BlockSpec((tm, k), lambda i, j: (i, 0)),
            pl.BlockSpec((k, tn), lambda i, j: (0, j)),
            pl.BlockSpec((tm, tn), lambda i, j: (i, j)),
            pl.BlockSpec((1, 1, tn), lambda i, j: (i // per_b, 0, j)),
        ],
        out_specs=pl.BlockSpec((tm, tn), lambda i, j: (i, j)),
        compiler_params=_params(("parallel", "arbitrary"), 48),
        name="matmul_residual",
    )(a, w, x, gate[:, None, :])


FFN_BLOCKS_PER_STEP = 2
FFN_ROW_SPLIT = 2


def _ffn_kernel(x_hbm, g_ref, sh_ref, sc_ref, gate_ref, wga_ref, wua_ref, wda_ref, wgb_ref, wub_ref, wdb_ref,
                o_ref, xs_ref, h_ref, sem, *, nf):
    i, f = pl.program_id(0), pl.program_id(1)
    tm = xs_ref.shape[0]

    def x_copy(tile):
        return pltpu.make_async_copy(x_hbm.at[pl.ds(pl.multiple_of(tile * tm, tm), tm), :], xs_ref, sem)

    @pl.when(f == 0)
    def _():
        @pl.when(i == 0)
        def _():
            x_copy(0).start()

        x_copy(i).wait()
        _modulate_into(h_ref, xs_ref, g_ref, sh_ref, sc_ref)
        o_ref[...] = xs_ref[...]

    @pl.when(jnp.logical_and(f == 1, i + 1 < pl.num_programs(0)))
    def _():
        x_copy(i + 1).start()

    def block(wg_ref, wu_ref, wd_ref):
        for r in range(FFN_ROW_SPLIT):
            rows = slice(r * tm // FFN_ROW_SPLIT, (r + 1) * tm // FFN_ROW_SPLIT)
            h = h_ref[rows, :]
            g = jnp.dot(h, wg_ref[...], preferred_element_type=F32)
            u = jnp.dot(h, wu_ref[...], preferred_element_type=F32)
            a = (_silu(g) * u).astype(BF16)
            o_ref[rows, :] += gate_ref[0] * jnp.dot(a, wd_ref[...], preferred_element_type=F32)

    block(wga_ref, wua_ref, wda_ref)

    @pl.when(FFN_BLOCKS_PER_STEP * f + 1 < nf)
    def _():
        block(wgb_ref, wub_ref, wdb_ref)


def _ffn(x, gain, shift, scale, gate, w_gu, w_down, layer, seq, tm, tf):
    t, d = x.shape
    dff = w_down.shape[1]
    nf = dff // tf
    per_b = seq // tm
    vec = lambda i, f: (i // per_b, 0, 0)
    blk_a = lambda f: FFN_BLOCKS_PER_STEP * f
    blk_b = lambda f: jnp.minimum(FFN_BLOCKS_PER_STEP * f + 1, nf - 1)
    w_specs = []
    for blk in (blk_a, blk_b):
        w_specs += [
            pl.BlockSpec((None, d, tf), lambda i, f, blk=blk: (layer, 0, blk(f))),
            pl.BlockSpec((None, d, tf), lambda i, f, blk=blk: (layer, 0, blk(f) + nf)),
            pl.BlockSpec((None, tf, d), lambda i, f, blk=blk: (layer, blk(f), 0)),
        ]
    return pl.pallas_call(
        functools.partial(_ffn_kernel, nf=nf),
        out_shape=jax.ShapeDtypeStruct((t, d), F32),
        grid=(t // tm, pl.cdiv(nf, FFN_BLOCKS_PER_STEP)),
        in_specs=[
            pl.BlockSpec(memory_space=pl.ANY),
            pl.BlockSpec((1, d), lambda i, f: (0, 0)),
            pl.BlockSpec((1, 1, d), vec),
            pl.BlockSpec((1, 1, d), vec),
            pl.BlockSpec((1, 1, d), vec),
        ] + w_specs,
        out_specs=pl.BlockSpec((tm, d), lambda i, f: (i, 0)),
        scratch_shapes=[pltpu.VMEM((tm, d), F32), pltpu.VMEM((tm, d), BF16), pltpu.SemaphoreType.DMA(())],
        compiler_params=_params(("arbitrary", "arbitrary"), 60),
        name="ffn",
    )(x, gain.reshape(1, d), shift[:, None, :], scale[:, None, :], gate[:, None, :],
      w_gu, w_gu, w_down, w_gu, w_gu, w_down)


GLA_PAIR = 2 * GLA_CHUNK


GLA_BATCH_PER_STEP = 2


def _gla_pair(q_ref, k_ref, v_ref, g_ref, a_ref, wal_ref, bal_ref, on_ref, o_ref, st_ref, bb, dk, dv):
    ch = GLA_CHUNK
    tp = q_ref.shape[1]
    qk = q_ref.shape[2]

    z = jnp.dot(a_ref[bb], wal_ref[...], preferred_element_type=F32) + bal_ref[...]
    la = (jnp.minimum(z, 0.0) - jnp.log(1.0 + jnp.exp(-jnp.abs(z)))) * (LOG2_E / GLA_TAU)

    row = lax.broadcasted_iota(jnp.int32, (tp, tp), 0)
    col = lax.broadcasted_iota(jnp.int32, (tp, tp), 1)
    causal = jnp.logical_and(col <= row, col >= (row // ch) * ch)
    cross = jnp.logical_and(row >= ch, col < ch)
    tri = jnp.where(causal, 1.0, 0.0).astype(BF16)
    first = lax.broadcasted_iota(jnp.int32, (tp, qk), 0) < ch
    first_h = lax.broadcasted_iota(jnp.int32, (tp, dk), 0) < ch

    la_hi = la.astype(BF16)
    la_lo = (la - la_hi.astype(F32)).astype(BF16)
    bcum = jnp.dot(tri, la_hi, preferred_element_type=F32) + jnp.dot(tri, la_lo, preferred_element_type=F32)
    b_last0 = bcum[ch - 1:ch]
    b_last1 = bcum[tp - 1:tp]
    e_q = jnp.exp2(bcum)
    e_ki = jnp.exp2(-bcum)
    e_ks = jnp.exp2(jnp.where(first, b_last0, b_last1) - bcum)
    dec0 = jnp.exp2(b_last0)
    dec1 = jnp.exp2(b_last1)
    dec01 = jnp.exp2(b_last0 + b_last1)
    on = on_ref[...]
    tn_dims = (((0,), (0,)), ((), ()))
    nt_dims = (((1,), (1,)), ((), ()))

    for h in range(GLA_HEADS):
        ks = slice(h * dk, (h + 1) * dk)
        vs = slice(h * dv, (h + 1) * dv)
        q = q_ref[bb, :, ks].astype(F32) * (dk ** -0.5)
        k = k_ref[bb, :, ks].astype(F32)
        v = v_ref[bb, :, vs]
        q_dec = q * e_q[:, ks]
        k_st = k * e_ks[:, ks]
        q_b = q_dec.astype(BF16)
        k_in = (k * e_ki[:, ks]).astype(BF16)
        k0 = jnp.where(first_h, k_st, 0.0).astype(BF16)
        att = lax.dot_general(q_b, k_in, nt_dims, preferred_element_type=F32)
        att_x = lax.dot_general(q_b, k0, nt_dims, preferred_element_type=F32)
        att = jnp.where(causal, att, jnp.where(cross, att_x, 0.0)).astype(BF16)
        s0 = st_ref[bb, h]
        q_s = jnp.where(first_h, q_dec, q_dec * dec0[:, ks]).astype(BF16)
        o = (jnp.dot(att, v, preferred_element_type=F32)
             + jnp.dot(q_s, s0.astype(BF16), preferred_element_type=F32))
        k_u = jnp.where(first_h, k_st * dec1[:, ks], k_st).astype(BF16)
        dec_cols = jnp.tile(jnp.broadcast_to(dec01[:, ks], (LANE, dk)).T, (1, dv // LANE))
        st_ref[bb, h] = dec_cols * s0 + lax.dot_general(k_u, v, tn_dims, preferred_element_type=F32)
        gg = g_ref[bb, :, vs].astype(F32)
        o_ref[bb, :, vs] = ((_rms_scale(o) * on) * _silu(gg)).astype(o_ref.dtype)


def _gla_kernel(q_ref, k_ref, v_ref, g_ref, a_ref, wal_ref, bal_ref, on_ref, o_ref, st_ref, *, dk, dv):
    @pl.when(pl.program_id(1) == 0)
    def _():
        st_ref[...] = jnp.zeros_like(st_ref)

    for bb in range(q_ref.shape[0]):
        _gla_pair(q_ref, k_ref, v_ref, g_ref, a_ref, wal_ref, bal_ref, on_ref, o_ref, st_ref, bb, dk, dv)


def _gla(proj, a_lr, w_alpha_p, b_alpha, onorm, batch, seq):
    dk = w_alpha_p.shape[1] // GLA_HEADS
    dv = onorm.shape[0]
    qk = GLA_HEADS * dk
    vv = GLA_HEADS * dv
    tp = GLA_PAIR
    nb = GLA_BATCH_PER_STEP if batch % GLA_BATCH_PER_STEP == 0 else 1
    return pl.pallas_call(
        functools.partial(_gla_kernel, dk=dk, dv=dv),
        out_shape=jax.ShapeDtypeStruct((batch, seq, vv), BF16),
        grid=(batch // nb, seq // tp),
        in_specs=[
            pl.BlockSpec((nb, tp, qk), lambda b, i: (b, i, 0)),
            pl.BlockSpec((nb, tp, qk), lambda b, i: (b, i, 1)),
            pl.BlockSpec((nb, tp, vv), lambda b, i: (b, i, 2 * qk // vv)),
            pl.BlockSpec((nb, tp, vv), lambda b, i: (b, i, 2 * qk // vv + 1)),
            pl.BlockSpec((nb, tp, LANE), lambda b, i: (b, i, 0)),
            pl.BlockSpec((LANE, qk), lambda b, i: (0, 0)),
            pl.BlockSpec((1, qk), lambda b, i: (0, 0)),
            pl.BlockSpec((1, dv), lambda b, i: (0, 0)),
        ],
        out_specs=pl.BlockSpec((nb, tp, vv), lambda b, i: (b, i, 0)),
        scratch_shapes=[pltpu.VMEM((nb, GLA_HEADS, dk, dv), F32)],
        compiler_params=_params(("parallel", "arbitrary"), 56),
        name="gla",
    )(proj, proj, proj, proj, a_lr, w_alpha_p, b_alpha.reshape(1, qk), onorm.reshape(1, dv))


def _rope_pair(x, x_swapped, cos4, sin4):
    return x * cos4 + x_swapped * sin4


def _head_norm_store(o_ref, h, rows, nope, rope_sq, rope_g, gain_n, dim, post_scale):
    ss = jnp.sum(nope * nope + rope_sq, axis=-1, keepdims=True)
    r = lax.rsqrt(ss / dim + EPS)
    if post_scale != 1.0:
        r = r * post_scale
    o_ref[0, h, rows, 0:LANE] = ((nope * r) * gain_n).astype(o_ref.dtype)
    o_ref[0, h, rows, LANE:2 * LANE] = (rope_g * r).astype(o_ref.dtype)


MLA_SUB = 256


def _sub_tiles(tm):
    step = min(tm, MLA_SUB)
    return [slice(r * step, (r + 1) * step) for r in range(tm // step)]


def _mla_kv_kernel(x_ref, g_ref, sh_ref, sc_ref, wd_ref, ln_ref, wk_ref, wv_ref, kn_ref, kr_ref,
                   cos_ref, sin_ref, k_ref, v_ref, h_ref):
    lora = ln_ref.shape[1]
    gn = kn_ref[...]
    gr = kr_ref[...]
    _modulate_into(h_ref, x_ref, g_ref, sh_ref, sc_ref)
    for rows in _sub_tiles(x_ref.shape[0]):
        ckv = jnp.dot(h_ref[rows, :], wd_ref[...], preferred_element_type=F32)
        c_lat = (_rms_scale(ckv[:, :lora]) * ln_ref[...]).astype(BF16)
        k_pe = _rope_pair(ckv[:, lora:lora + LANE], ckv[:, lora + LANE:lora + 2 * LANE],
                          cos_ref[rows, :], sin_ref[rows, :])
        kn = jnp.dot(c_lat, wk_ref[...], preferred_element_type=F32)
        vv = jnp.dot(c_lat, wv_ref[...], preferred_element_type=F32)
        pe_sq = k_pe * k_pe
        pe_g = k_pe * gr
        for h in range(MLA_HEADS):
            cols = slice(h * LANE, (h + 1) * LANE)
            _head_norm_store(k_ref, h, rows, kn[:, cols], pe_sq, pe_g, gn, MLA_NOPE + MLA_ROPE, 1.0)
            v_ref[0, h, 0, :, rows] = vv[:, cols].T.astype(v_ref.dtype)


def _mla_q_kernel(x_ref, g_ref, sh_ref, sc_ref, wd_ref, ln_ref, wn_ref, wr_ref, ws_ref, qn_ref, qr_ref,
                  cos_ref, sin_ref, q_ref, h_ref):
    gn = qn_ref[...]
    gr = qr_ref[...]
    sm_scale = (MLA_NOPE + MLA_ROPE) ** -0.5 * LOG2_E
    _modulate_into(h_ref, x_ref, g_ref, sh_ref, sc_ref)
    for rows in _sub_tiles(x_ref.shape[0]):
        cq = jnp.dot(h_ref[rows, :], wd_ref[...], preferred_element_type=F32)
        cq = (_rms_scale(cq) * ln_ref[...]).astype(BF16)
        qn = jnp.dot(cq, wn_ref[...], preferred_element_type=F32)
        qr = jnp.dot(cq, wr_ref[...], preferred_element_type=F32)
        qs = jnp.dot(cq, ws_ref[...], preferred_element_type=F32)
        cos4 = cos_ref[rows, :]
        sin4 = sin_ref[rows, :]
        for h in range(MLA_HEADS):
            cols = slice(h * LANE, (h + 1) * LANE)
            rope = _rope_pair(qr[:, cols], qs[:, cols], cos4, sin4)
            _head_norm_store(q_ref, h, rows, qn[:, cols], rope * rope, rope * gr, gn, MLA_NOPE + MLA_ROPE, sm_scale)


def _const_spec(arr):
    nd = arr.ndim
    return pl.BlockSpec(arr.shape, lambda b, i: (0,) * nd)


def _mla_kv(x, gain, shift, scale, w_dkv_p, lat_norm, w_k, w_v, kg_n, kg_r, cos4, sin4, batch, seq, tm):
    t, d = x.shape
    nt = seq // tm
    row = lambda b, i: (b * nt + i, 0)
    vec = lambda b, i: (b, 0, 0)
    g2 = gain.reshape(1, d)
    weights = [w_dkv_p, lat_norm.reshape(1, -1), w_k, w_v, kg_n, kg_r]
    hk = pl.BlockSpec((1, MLA_HEADS, tm, 2 * LANE), lambda b, i: (b, 0, i, 0))
    hv = pl.BlockSpec((1, MLA_HEADS, 1, MLA_V, tm), lambda b, i: (b, 0, i, 0, 0))
    return pl.pallas_call(
        _mla_kv_kernel,
        out_shape=(jax.ShapeDtypeStruct((batch, MLA_HEADS, seq, 2 * LANE), BF16),
                   jax.ShapeDtypeStruct((batch, MLA_HEADS, nt, MLA_V, tm), BF16)),
        grid=(batch, nt),
        in_specs=[pl.BlockSpec((tm, d), row), _const_spec(g2),
                  pl.BlockSpec((1, 1, d), vec), pl.BlockSpec((1, 1, d), vec)]
                 + [_const_spec(w) for w in weights]
                 + [pl.BlockSpec((tm, LANE), row), pl.BlockSpec((tm, LANE), row)],
        out_specs=(hk, hv),
        scratch_shapes=[pltpu.VMEM((tm, d), BF16)],
        compiler_params=_params(("parallel", "parallel"), 48),
        name="mla_kv",
    )(x, g2, shift[:, None, :], scale[:, None, :], *weights, cos4, sin4)


def _mla_q(x, gain, shift, scale, w_dq, q_lat_norm, w_qn, w_qr, w_qs, qg_n, qg_r, cos4, sin4, batch, seq, tm):
    t, d = x.shape
    nt = seq // tm
    row = lambda b, i: (b * nt + i, 0)
    vec = lambda b, i: (b, 0, 0)
    g2 = gain.reshape(1, d)
    weights = [w_dq, q_lat_norm.reshape(1, -1), w_qn, w_qr, w_qs, qg_n, qg_r]
    hq = pl.BlockSpec((1, MLA_HEADS, tm, 2 * LANE), lambda b, i: (b, 0, i, 0))
    return pl.pallas_call(
        _mla_q_kernel,
        out_shape=jax.ShapeDtypeStruct((batch, MLA_HEADS, seq, 2 * LANE), BF16),
        grid=(batch, nt),
        in_specs=[pl.BlockSpec((tm, d), row), _const_spec(g2),
                  pl.BlockSpec((1, 1, d), vec), pl.BlockSpec((1, 1, d), vec)]
                 + [_const_spec(w) for w in weights]
                 + [pl.BlockSpec((tm, LANE), row), pl.BlockSpec((tm, LANE), row)],
        out_specs=hq,
        scratch_shapes=[pltpu.VMEM((tm, d), BF16)],
        compiler_params=_params(("parallel", "parallel"), 48),
        name="mla_q",
    )(x, g2, shift[:, None, :], scale[:, None, :], *weights, cos4, sin4)


ATTN_HEADS_PER_STEP = 4
ATTN_Q_BLOCKS_PER_STEP = 2
ATTN_MAX_SCORE_BOUND = 60.0


def _attn_kernel(q_ref, k_ref, vt_ref, o_ref, *, tq, running_max):
    first_pair = pl.program_id(2) * (ATTN_Q_BLOCKS_PER_STEP // 2)
    nh = q_ref.shape[1]
    tc = vt_ref.shape[-1]
    dv = vt_ref.shape[-2]
    nt_dims = (((1,), (1,)), ((), ()))

    def step(rows, blk0, nblk, carry, masked):
        tk = nblk * tq
        scores = []
        for h in range(nh):
            k = k_ref[0, h, pl.ds(pl.multiple_of(blk0 * tq, tq), tk), :]
            scores.append(lax.dot_general(k, q_ref[0, h, rows, :], nt_dims, preferred_element_type=F32))
        out = []
        for h in range(nh):
            st = scores[h]
            if masked:
                kv_i = lax.broadcasted_iota(jnp.int32, (tk, tq), 0)
                q_i = lax.broadcasted_iota(jnp.int32, (tk, tq), 1)
                st = jnp.where(kv_i - (nblk - 1) * tq <= q_i, st, -jnp.inf)
            if running_max:
                m, l, acc = carry[h]
                m_new = jnp.maximum(m, jnp.max(st, axis=0, keepdims=True))
                alpha = jnp.exp2(m - m_new)
                pt = jnp.exp2(st - m_new)
                l = alpha * l
                acc = alpha * acc
            else:
                l, acc = carry[h]
                pt = jnp.exp2(st)
            l = l + jnp.sum(pt, axis=0, keepdims=True)
            pt = pt.astype(BF16)
            slab0 = blk0 * (tq // tc)
            pv = jnp.dot(vt_ref[0, h, slab0], pt[0:tc], preferred_element_type=F32)
            for c in range(1, tk // tc):
                pv += jnp.dot(vt_ref[0, h, slab0 + c], pt[c * tc:(c + 1) * tc], preferred_element_type=F32)
            out.append((m_new, l, acc + pv) if running_max else (l, acc + pv))
        return tuple(out)

    stats = (jnp.zeros((1, tq), F32), jnp.zeros((dv, tq), F32))
    if running_max:
        stats = (jnp.full((1, tq), -jnp.inf, F32),) + stats
    init = tuple(stats for _ in range(nh))
    for sub in range(ATTN_Q_BLOCKS_PER_STEP):
        rows = slice(sub * tq, (sub + 1) * tq)
        pairs = first_pair + sub // 2
        carry = lax.fori_loop(0, pairs, lambda j, c: step(rows, 2 * j, 2, c, False), init)
        carry = step(rows, 2 * pairs, sub % 2 + 1, carry, True)
        for h in range(nh):
            l, acc = carry[h][-2:]
            o_ref[0, rows, h * dv:(h + 1) * dv] = (acc / l).T.astype(o_ref.dtype)


def _attention(q, k, vt, tq, running_max):
    b, h, s, dq = q.shape
    _, _, nslab, dvv, tc = vt.shape
    nh = ATTN_HEADS_PER_STEP
    return pl.pallas_call(
        functools.partial(_attn_kernel, tq=tq, running_max=running_max),
        out_shape=jax.ShapeDtypeStruct((b, s, h * dvv), BF16),
        grid=(b, h // nh, s // (ATTN_Q_BLOCKS_PER_STEP * tq)),
        in_specs=[
            pl.BlockSpec((1, nh, ATTN_Q_BLOCKS_PER_STEP * tq, dq), lambda bi, hi, qi: (bi, hi, qi, 0)),
            pl.BlockSpec((1, nh, s, dq), lambda bi, hi, qi: (bi, hi, 0, 0)),
            pl.BlockSpec((1, nh, nslab, dvv, tc), lambda bi, hi, qi: (bi, hi, 0, 0, 0)),
        ],
        out_specs=pl.BlockSpec((1, ATTN_Q_BLOCKS_PER_STEP * tq, nh * dvv), lambda bi, hi, qi: (bi, qi, hi)),
        compiler_params=_params(("parallel", "parallel", "arbitrary"), 48),
        name="mla_attention" if running_max else "mla_attention_bounded",
    )(q, k, vt)


def _rope_tables(positions):
    b, s = positions.shape
    half = MLA_ROPE // 2
    inv_freq = ROPE_THETA ** (-jnp.arange(half, dtype=F32) / half)
    ang = positions.astype(F32)[..., None] * inv_freq
    cos, sin = _trig(ang.reshape(-1, LANE))
    cos = cos.reshape(b * s, half)
    sin = sin.reshape(b * s, half)
    zero = jnp.zeros((b * s, LANE - 2 * half), F32)
    cos4 = jnp.concatenate([cos, cos, zero], axis=-1)
    sin4 = jnp.concatenate([-sin, sin, zero], axis=-1)
    return cos4, sin4


def _split_rope_cols(w_rope):
    half = MLA_ROPE // 2
    x1, x2 = w_rope[..., :half], w_rope[..., half:]
    zero = jnp.zeros(w_rope.shape[:-1] + (LANE - MLA_ROPE,), w_rope.dtype)
    return jnp.concatenate([x1, x2, zero], axis=-1), jnp.concatenate([x2, x1, zero], axis=-1)


def _pad_rope_gain(gain):
    g_n = gain[:MLA_NOPE].reshape(1, MLA_NOPE)
    g_r = jnp.concatenate([gain[MLA_NOPE:], jnp.zeros((LANE - MLA_ROPE,), gain.dtype)]).reshape(1, LANE)
    return g_n, g_r


def kernel(x, c, positions, ada_w, ada_b, norm_mix, norm_ffn, gla_w_in, gla_w_alpha, gla_b_alpha, gla_onorm,
           gla_w_out, mla_w_dq, mla_q_lat_norm, mla_w_uq, mla_q_norm, mla_w_out, kv_ada_w, kv_ada_b, kv_norm,
           kv_w_dkv, kv_lat_norm, kv_w_ukv, kv_k_norm, ffn_w_gu, ffn_w_down):
    batch, seq, d = x.shape
    t = batch * seq
    depth = ada_w.shape[0]
    n_gla = gla_w_in.shape[0]
    xf = x.reshape(t, d)

    tm_big = min(seq, 1024)
    tm_mid = min(seq, 512)
    t_attn = min(tm_mid, seq // ATTN_Q_BLOCKS_PER_STEP)

    c_pad = jnp.pad(c, ((0, ADA_ROWS - batch), (0, 0)))
    cos4, sin4 = _rope_tables(positions)
    w_gu_bf = ffn_w_gu.astype(BF16)
    w_down_bf = ffn_w_down.astype(BF16)

    k_sh = v_sh = None
    for layer in range(depth):
        mod = _ada_mod(c_pad, ada_w, ada_b, layer)[:batch]
        shift_m, scale_m, gate_m, shift_f, scale_f, gate_f = jnp.split(mod, 6, axis=-1)
        if layer < n_gla:
            qkvg = gla_w_in.shape[2] - GLA_GATE_RANK
            w_a = jnp.pad(gla_w_in[layer, :, qkvg:], ((0, 0), (0, LANE - GLA_GATE_RANK))).astype(BF16)
            proj, a_lr = _mod_matmul(xf, norm_mix[layer], shift_m, scale_m, gla_w_in.astype(BF16), layer, qkvg,
                                     w_a, seq, tm_big, 2048)
            w_alpha_p = jnp.pad(gla_w_alpha[layer], ((0, LANE - GLA_GATE_RANK), (0, 0))).astype(BF16)
            mix_in = _gla(proj.reshape(batch, seq, qkvg), a_lr.reshape(batch, seq, LANE), w_alpha_p,
                          gla_b_alpha[layer], gla_onorm[layer], batch, seq).reshape(t, -1)
            w_out = gla_w_out[layer].astype(BF16)
        else:
            j = layer - n_gla
            if j == 0:
                kv_mod = _ada_mod(c_pad, kv_ada_w[None], kv_ada_b[None], 0)[:batch]
                kv_shift, kv_scale = jnp.split(kv_mod, 2, axis=-1)
                pe_p, pe_s = _split_rope_cols(kv_w_dkv[:, KV_LORA:])
                w_dkv_p = jnp.concatenate([kv_w_dkv[:, :KV_LORA], pe_p, pe_s], axis=1).astype(BF16)
                w_ukv = kv_w_ukv.reshape(KV_LORA, MLA_HEADS, MLA_NOPE + MLA_V)
                w_k = w_ukv[:, :, :MLA_NOPE].reshape(KV_LORA, MLA_HEADS * MLA_NOPE).astype(BF16)
                w_v = w_ukv[:, :, MLA_NOPE:].reshape(KV_LORA, MLA_HEADS * MLA_V).astype(BF16)
                kg_n, kg_r = _pad_rope_gain(kv_k_norm)
                k_sh, v_sh = _mla_kv(xf, kv_norm, kv_shift, kv_scale, w_dkv_p, kv_lat_norm, w_k, w_v,
                                     kg_n, kg_r, cos4, sin4, batch, seq, t_attn)
            q_lora = mla_w_dq.shape[2]
            w_uq = mla_w_uq[j].reshape(q_lora, MLA_HEADS, MLA_NOPE + MLA_ROPE)
            w_qn = w_uq[:, :, :MLA_NOPE].reshape(q_lora, MLA_HEADS * MLA_NOPE).astype(BF16)
            r_p, r_s = _split_rope_cols(w_uq[:, :, MLA_NOPE:])
            w_qr = r_p.reshape(q_lora, MLA_HEADS * LANE).astype(BF16)
            w_qs = r_s.reshape(q_lora, MLA_HEADS * LANE).astype(BF16)
            qg_n, qg_r = _pad_rope_gain(mla_q_norm[j])
            q = _mla_q(xf, norm_mix[layer], shift_m, scale_m, mla_w_dq[j].astype(BF16), mla_q_lat_norm[j],
                       w_qn, w_qr, w_qs, qg_n, qg_r, cos4, sin4, batch, seq, tm_mid)
            score_bound = ((MLA_NOPE + MLA_ROPE) ** 0.5 * LOG2_E
                           * jnp.max(jnp.abs(mla_q_norm[j])) * jnp.max(jnp.abs(kv_k_norm)))
            mix_in = lax.cond(score_bound <= ATTN_MAX_SCORE_BOUND,
                              lambda: _attention(q, k_sh, v_sh, t_attn, False),
                              lambda: _attention(q, k_sh, v_sh, t_attn, True))
            mix_in = mix_in.reshape(t, MLA_HEADS * MLA_V)
            w_out = mla_w_out[j].astype(BF16)
        xf = _matmul_residual(mix_in, w_out, xf, gate_m, seq, tm_mid, d)
        xf = _ffn(xf, norm_ffn[layer], shift_f, scale_f, gate_f, w_gu_bf, w_down_bf, layer, seq, tm_big, 512)
    return xf.reshape(batch, seq, d)
```

```python
import functools

import jax
import jax.numpy as jnp
from jax import lax
from jax.experimental import pallas as pl
from jax.experimental.pallas import tpu as pltpu

GLA_HEADS = 4
GLA_GATE_RANK = 16
GLA_TAU = 16.0
GLA_CHUNK = 64
MLA_HEADS = 16
MLA_NOPE = 128
MLA_ROPE = 64
MLA_V = 128
KV_LORA = 512
ROPE_THETA = 10000.0
EPS = 1e-6
LOG2_E = 1.4426950408889634

LANE = 128
MIB = 1 << 20

F32 = jnp.float32
BF16 = jnp.bfloat16


def _params(semantics, vmem_mib):
    return pltpu.CompilerParams(dimension_semantics=semantics, vmem_limit_bytes=vmem_mib * MIB)


def _silu(x):
    return x * jax.nn.sigmoid(x)


def _rms_scale(x):
    return x * lax.rsqrt(jnp.mean(x * x, axis=-1, keepdims=True) + EPS)


ADA_ROWS = 16


def _split_bf16(x):
    hi = x.astype(BF16)
    return hi, (x - hi.astype(F32)).astype(BF16)


def _ada_kernel(c_ref, w_ref, b_ref, o_ref):
    s_hi, s_lo = _split_bf16(_silu(c_ref[...]))
    w_hi, w_lo = _split_bf16(w_ref[...])
    acc = jnp.dot(s_hi, w_lo, preferred_element_type=F32) + jnp.dot(s_lo, w_hi, preferred_element_type=F32)
    o_ref[...] = (acc + jnp.dot(s_hi, w_hi, preferred_element_type=F32)) + b_ref[...]


def _ada_mod(c_pad, w, b, layer, tn=1024):
    rows, d = c_pad.shape
    n = w.shape[2]
    return pl.pallas_call(
        _ada_kernel,
        out_shape=jax.ShapeDtypeStruct((rows, n), F32),
        grid=(n // tn,),
        in_specs=[
            pl.BlockSpec((rows, d), lambda j: (0, 0)),
            pl.BlockSpec((None, d, tn), lambda j: (layer, 0, j)),
            pl.BlockSpec((None, 1, tn), lambda j: (layer, 0, j)),
        ],
        out_specs=pl.BlockSpec((rows, tn), lambda j: (0, j)),
        compiler_params=_params(("parallel",), 48),
        name="ada_mod",
    )(c_pad, w, b[:, None, :])


def _trig_kernel(a_ref, c_ref, s_ref):
    a = a_ref[...]
    c_ref[...] = jnp.cos(a)
    s_ref[...] = jnp.sin(a)


def _trig(ang):
    rows = ang.shape[0]
    tr = min(rows, 512)
    spec = pl.BlockSpec((tr, LANE), lambda i: (i, 0))
    return pl.pallas_call(
        _trig_kernel,
        out_shape=(jax.ShapeDtypeStruct(ang.shape, F32),) * 2,
        grid=(rows // tr,),
        in_specs=[spec],
        out_specs=(spec, spec),
        compiler_params=_params(("parallel",), 16),
        name="rope_trig",
    )(ang)


MOD_ROWS = 16


def _modulate_into(h_ref, x_ref, g_ref, sh_ref, sc_ref):
    tm = x_ref.shape[0]
    step = min(tm, MOD_ROWS)
    gs = g_ref[...] * (1.0 + sc_ref[0])
    shift = sh_ref[0]
    for r in range(tm // step):
        rows = slice(r * step, (r + 1) * step)
        h_ref[rows, :] = (_rms_scale(x_ref[rows, :]) * gs + shift).astype(h_ref.dtype)


def _modmm_kernel(x_ref, g_ref, sh_ref, sc_ref, w_ref, wa_ref, o_ref, a_ref, h_ref):
    @pl.when(pl.program_id(1) == 0)
    def _():
        _modulate_into(h_ref, x_ref, g_ref, sh_ref, sc_ref)
        a_ref[...] = jnp.dot(h_ref[...], wa_ref[...], preferred_element_type=F32).astype(a_ref.dtype)

    o_ref[...] = jnp.dot(h_ref[...], w_ref[...], preferred_element_type=F32).astype(o_ref.dtype)


def _mod_matmul(x, gain, shift, scale, w, layer, n, w_a, seq, tm, tn):
    t, d = x.shape
    per_b = seq // tm
    vec = lambda i, j: (i // per_b, 0, 0)
    return pl.pallas_call(
        _modmm_kernel,
        out_shape=(jax.ShapeDtypeStruct((t, n), BF16), jax.ShapeDtypeStruct((t, LANE), BF16)),
        grid=(t // tm, n // tn),
        in_specs=[
            pl.BlockSpec((tm, d), lambda i, j: (i, 0)),
            pl.BlockSpec((1, d), lambda i, j: (0, 0)),
            pl.BlockSpec((1, 1, d), vec),
            pl.BlockSpec((1, 1, d), vec),
            pl.BlockSpec((None, d, tn), lambda i, j: (layer, 0, j)),
            pl.BlockSpec((d, LANE), lambda i, j: (0, 0)),
        ],
        out_specs=(pl.BlockSpec((tm, tn), lambda i, j: (i, j)),
                   pl.BlockSpec((tm, LANE), lambda i, j: (i, 0))),
        scratch_shapes=[pltpu.VMEM((tm, d), BF16)],
        compiler_params=_params(("parallel", "arbitrary"), 56),
        name="mod_matmul",
    )(x, gain.reshape(1, d), shift[:, None, :], scale[:, None, :], w, w_a)


def _mm_res_kernel(a_ref, w_ref, x_ref, gate_ref, o_ref):
    acc = jnp.dot(a_ref[...], w_ref[...], preferred_element_type=F32)
    o_ref[...] = x_ref[...] + gate_ref[0] * acc


def _matmul_residual(a, w, x, gate, seq, tm, tn):
    t, k = a.shape
    n = w.shape[1]
    per_b = seq // tm
    return pl.pallas_call(
        _mm_res_kernel,
        out_shape=jax.ShapeDtypeStruct((t, n), F32),
        grid=(t // tm, n // tn),
        in_specs=[
            pl.BlockSpec((tm, k), lambda i, j: (i, 0)),
            pl.BlockSpec((k, tn), lambda i, j: (0, j)),
            pl.BlockSpec((tm, tn), lambda i, j: (i, j)),
            pl.BlockSpec((1, 1, tn), lambda i, j: (i // per_b, 0, j)),
        ],
        out_specs=pl.BlockSpec((tm, tn), lambda i, j: (i, j)),
        compiler_params=_params(("parallel", "arbitrary"), 48),
        name="matmul_residual",
    )(a, w, x, gate[:, None, :])


FFN_BLOCKS_PER_STEP = 2
FFN_ROW_SPLIT = 2


def _ffn_kernel(x_hbm, g_ref, sh_ref, sc_ref, gate_ref, wga_ref, wua_ref, wda_ref, wgb_ref, wub_ref, wdb_ref,
                o_ref, xs_ref, h_ref, sem, *, nf):
    i, f = pl.program_id(0), pl.program_id(1)
    tm = xs_ref.shape[0]

    def x_copy(tile):
        return pltpu.make_async_copy(x_hbm.at[pl.ds(pl.multiple_of(tile * tm, tm), tm), :], xs_ref, sem)

    @pl.when(f == 0)
    def _():
        @pl.when(i == 0)
        def _():
            x_copy(0).start()

        x_copy(i).wait()
        _modulate_into(h_ref, xs_ref, g_ref, sh_ref, sc_ref)
        o_ref[...] = xs_ref[...]

    @pl.when(jnp.logical_and(f == 1, i + 1 < pl.num_programs(0)))
    def _():
        x_copy(i + 1).start()

    def block(wg_ref, wu_ref, wd_ref):
        for r in range(FFN_ROW_SPLIT):
            rows = slice(r * tm // FFN_ROW_SPLIT, (r + 1) * tm // FFN_ROW_SPLIT)
            h = h_ref[rows, :]
            g = jnp.dot(h, wg_ref[...], preferred_element_type=F32)
            u = jnp.dot(h, wu_ref[...], preferred_element_type=F32)
            a = (_silu(g) * u).astype(BF16)
            o_ref[rows, :] += gate_ref[0] * jnp.dot(a, wd_ref[...], preferred_element_type=F32)

    has_second = FFN_BLOCKS_PER_STEP * f + 1 < nf

    @pl.when(has_second)
    def _():
        block(wga_ref, wua_ref, wda_ref)
        block(wgb_ref, wub_ref, wdb_ref)

    @pl.when(jnp.logical_not(has_second))
    def _():
        block(wga_ref, wua_ref, wda_ref)


def _ffn(x, gain, shift, scale, gate, w_gu, w_down, layer, seq, tm, tf):
    t, d = x.shape
    dff = w_down.shape[1]
    nf = dff // tf
    per_b = seq // tm
    vec = lambda i, f: (i // per_b, 0, 0)
    blk_a = lambda f: FFN_BLOCKS_PER_STEP * f
    blk_b = lambda f: jnp.minimum(FFN_BLOCKS_PER_STEP * f + 1, nf - 1)
    w_specs = []
    for blk in (blk_a, blk_b):
        w_specs += [
            pl.BlockSpec((None, d, tf), lambda i, f, blk=blk: (layer, 0, blk(f))),
            pl.BlockSpec((None, d, tf), lambda i, f, blk=blk: (layer, 0, blk(f) + nf)),
            pl.BlockSpec((None, tf, d), lambda i, f, blk=blk: (layer, blk(f), 0)),
        ]
    return pl.pallas_call(
        functools.partial(_ffn_kernel, nf=nf),
        out_shape=jax.ShapeDtypeStruct((t, d), F32),
        grid=(t // tm, pl.cdiv(nf, FFN_BLOCKS_PER_STEP)),
        in_specs=[
            pl.BlockSpec(memory_space=pl.ANY),
            pl.BlockSpec((1, d), lambda i, f: (0, 0)),
            pl.BlockSpec((1, 1, d), vec),
            pl.BlockSpec((1, 1, d), vec),
            pl.BlockSpec((1, 1, d), vec),
        ] + w_specs,
        out_specs=pl.BlockSpec((tm, d), lambda i, f: (i, 0)),
        scratch_shapes=[pltpu.VMEM((tm, d), F32), pltpu.VMEM((tm, d), BF16), pltpu.SemaphoreType.DMA(())],
        compiler_params=_params(("arbitrary", "arbitrary"), 60),
        name="ffn",
    )(x, gain.reshape(1, d), shift[:, None, :], scale[:, None, :], gate[:, None, :],
      w_gu, w_gu, w_down, w_gu, w_gu, w_down)


GLA_PAIR = 2 * GLA_CHUNK


GLA_BATCH_PER_STEP = 2


def _gla_pair(q_ref, k_ref, v_ref, g_ref, a_ref, wal_ref, bal_ref, on_ref, o_ref, st_ref, bb, dk, dv):
    ch = GLA_CHUNK
    tp = q_ref.shape[1]
    qk = q_ref.shape[2]

    z = jnp.dot(a_ref[bb], wal_ref[...], preferred_element_type=F32) + bal_ref[...]
    la = (jnp.minimum(z, 0.0) - jnp.log(1.0 + jnp.exp(-jnp.abs(z)))) * (LOG2_E / GLA_TAU)

    row = lax.broadcasted_iota(jnp.int32, (tp, tp), 0)
    col = lax.broadcasted_iota(jnp.int32, (tp, tp), 1)
    causal = jnp.logical_and(col <= row, col >= (row // ch) * ch)
    cross = jnp.logical_and(row >= ch, col < ch)
    tri = jnp.where(causal, 1.0, 0.0).astype(BF16)
    first = lax.broadcasted_iota(jnp.int32, (tp, qk), 0) < ch
    first_h = lax.broadcasted_iota(jnp.int32, (tp, dk), 0) < ch

    la_hi = la.astype(BF16)
    la_lo = (la - la_hi.astype(F32)).astype(BF16)
    bcum = jnp.dot(tri, la_hi, preferred_element_type=F32) + jnp.dot(tri, la_lo, preferred_element_type=F32)
    b_last0 = bcum[ch - 1:ch]
    b_last1 = bcum[tp - 1:tp]
    e_q = jnp.exp2(bcum)
    e_ki = jnp.exp2(-bcum)
    e_ks = jnp.exp2(jnp.where(first, b_last0, b_last1) - bcum)
    dec0 = jnp.exp2(b_last0)
    dec1 = jnp.exp2(b_last1)
    dec01 = jnp.exp2(b_last0 + b_last1)
    on = on_ref[...]
    tn_dims = (((0,), (0,)), ((), ()))
    nt_dims = (((1,), (1,)), ((), ()))

    for h in range(GLA_HEADS):
        ks = slice(h * dk, (h + 1) * dk)
        vs = slice(h * dv, (h + 1) * dv)
        q = q_ref[bb, :, ks].astype(F32) * (dk ** -0.5)
        k = k_ref[bb, :, ks].astype(F32)
        v = v_ref[bb, :, vs]
        q_dec = q * e_q[:, ks]
        k_st = k * e_ks[:, ks]
        q_b = q_dec.astype(BF16)
        k_in = (k * e_ki[:, ks]).astype(BF16)
        k0 = jnp.where(first_h, k_st, 0.0).astype(BF16)
        att = lax.dot_general(q_b, k_in, nt_dims, preferred_element_type=F32)
        att_x = lax.dot_general(q_b, k0, nt_dims, preferred_element_type=F32)
        att = jnp.where(causal, att, jnp.where(cross, att_x, 0.0)).astype(BF16)
        s0 = st_ref[bb, h]
        q_s = jnp.where(first_h, q_dec, q_dec * dec0[:, ks]).astype(BF16)
        o = (jnp.dot(att, v, preferred_element_type=F32)
             + jnp.dot(q_s, s0.astype(BF16), preferred_element_type=F32))
        k_u = jnp.where(first_h, k_st * dec1[:, ks], k_st).astype(BF16)
        dec_cols = jnp.tile(jnp.broadcast_to(dec01[:, ks], (LANE, dk)).T, (1, dv // LANE))
        st_ref[bb, h] = dec_cols * s0 + lax.dot_general(k_u, v, tn_dims, preferred_element_type=F32)
        gg = g_ref[bb, :, vs].astype(F32)
        o_ref[bb, :, vs] = ((_rms_scale(o) * on) * _silu(gg)).astype(o_ref.dtype)


def _gla_kernel(q_ref, k_ref, v_ref, g_ref, a_ref, wal_ref, bal_ref, on_ref, o_ref, st_ref, *, dk, dv):
    @pl.when(pl.program_id(1) == 0)
    def _():
        st_ref[...] = jnp.zeros_like(st_ref)

    for bb in range(q_ref.shape[0]):
        _gla_pair(q_ref, k_ref, v_ref, g_ref, a_ref, wal_ref, bal_ref, on_ref, o_ref, st_ref, bb, dk, dv)


def _gla(proj, a_lr, w_alpha_p, b_alpha, onorm, batch, seq):
    dk = w_alpha_p.shape[1] // GLA_HEADS
    dv = onorm.shape[0]
    qk = GLA_HEADS * dk
    vv = GLA_HEADS * dv
    tp = GLA_PAIR
    nb = GLA_BATCH_PER_STEP if batch % GLA_BATCH_PER_STEP == 0 else 1
    return pl.pallas_call(
        functools.partial(_gla_kernel, dk=dk, dv=dv),
        out_shape=jax.ShapeDtypeStruct((batch, seq, vv), BF16),
        grid=(batch // nb, seq // tp),
        in_specs=[
            pl.BlockSpec((nb, tp, qk), lambda b, i: (b, i, 0)),
            pl.BlockSpec((nb, tp, qk), lambda b, i: (b, i, 1)),
            pl.BlockSpec((nb, tp, vv), lambda b, i: (b, i, 2 * qk // vv)),
            pl.BlockSpec((nb, tp, vv), lambda b, i: (b, i, 2 * qk // vv + 1)),
            pl.BlockSpec((nb, tp, LANE), lambda b, i: (b, i, 0)),
            pl.BlockSpec((LANE, qk), lambda b, i: (0, 0)),
            pl.BlockSpec((1, qk), lambda b, i: (0, 0)),
            pl.BlockSpec((1, dv), lambda b, i: (0, 0)),
        ],
        out_specs=pl.BlockSpec((nb, tp, vv), lambda b, i: (b, i, 0)),
        scratch_shapes=[pltpu.VMEM((nb, GLA_HEADS, dk, dv), F32)],
        compiler_params=_params(("parallel", "arbitrary"), 56),
        name="gla",
    )(proj, proj, proj, proj, a_lr, w_alpha_p, b_alpha.reshape(1, qk), onorm.reshape(1, dv))


def _rope_pair(x, x_swapped, cos4, sin4):
    return x * cos4 + x_swapped * sin4


def _head_norm_store(o_ref, h, rows, nope, rope_sq, rope_g, gain_n, dim, post_scale):
    ss = jnp.sum(nope * nope + rope_sq, axis=-1, keepdims=True)
    r = lax.rsqrt(ss / dim + EPS)
    if post_scale != 1.0:
        r = r * post_scale
    o_ref[0, h, rows, 0:LANE] = ((nope * r) * gain_n).astype(o_ref.dtype)
    o_ref[0, h, rows, LANE:2 * LANE] = (rope_g * r).astype(o_ref.dtype)


MLA_SUB = 256


def _sub_tiles(tm):
    step = min(tm, MLA_SUB)
    return [slice(r * step, (r + 1) * step) for r in range(tm // step)]


def _mla_kv_kernel(x_ref, g_ref, sh_ref, sc_ref, wd_ref, ln_ref, wk_ref, wv_ref, kn_ref, kr_ref,
                   cos_ref, sin_ref, k_ref, v_ref, h_ref):
    lora = ln_ref.shape[1]
    gn = kn_ref[...]
    gr = kr_ref[...]
    _modulate_into(h_ref, x_ref, g_ref, sh_ref, sc_ref)
    for rows in _sub_tiles(x_ref.shape[0]):
        ckv = jnp.dot(h_ref[rows, :], wd_ref[...], preferred_element_type=F32)
        c_lat = (_rms_scale(ckv[:, :lora]) * ln_ref[...]).astype(BF16)
        k_pe = _rope_pair(ckv[:, lora:lora + LANE], ckv[:, lora + LANE:lora + 2 * LANE],
                          cos_ref[rows, :], sin_ref[rows, :])
        kn = jnp.dot(c_lat, wk_ref[...], preferred_element_type=F32)
        vv = jnp.dot(c_lat, wv_ref[...], preferred_element_type=F32)
        pe_sq = k_pe * k_pe
        pe_g = k_pe * gr
        for h in range(MLA_HEADS):
            cols = slice(h * LANE, (h + 1) * LANE)
            _head_norm_store(k_ref, h, rows, kn[:, cols], pe_sq, pe_g, gn, MLA_NOPE + MLA_ROPE, 1.0)
            v_ref[0, h, 0, :, rows] = vv[:, cols].T.astype(v_ref.dtype)


def _mla_q_kernel(x_ref, g_ref, sh_ref, sc_ref, wd_ref, ln_ref, wn_ref, wr_ref, ws_ref, qn_ref, qr_ref,
                  cos_ref, sin_ref, q_ref, h_ref):
    gn = qn_ref[...]
    gr = qr_ref[...]
    sm_scale = (MLA_NOPE + MLA_ROPE) ** -0.5 * LOG2_E
    _modulate_into(h_ref, x_ref, g_ref, sh_ref, sc_ref)
    for rows in _sub_tiles(x_ref.shape[0]):
        cq = jnp.dot(h_ref[rows, :], wd_ref[...], preferred_element_type=F32)
        cq = (_rms_scale(cq) * ln_ref[...]).astype(BF16)
        qn = jnp.dot(cq, wn_ref[...], preferred_element_type=F32)
        qr = jnp.dot(cq, wr_ref[...], preferred_element_type=F32)
        qs = jnp.dot(cq, ws_ref[...], preferred_element_type=F32)
        cos4 = cos_ref[rows, :]
        sin4 = sin_ref[rows, :]
        for h in range(MLA_HEADS):
            cols = slice(h * LANE, (h + 1) * LANE)
            rope = _rope_pair(qr[:, cols], qs[:, cols], cos4, sin4)
            _head_norm_store(q_ref, h, rows, qn[:, cols], rope * rope, rope * gr, gn, MLA_NOPE + MLA_ROPE, sm_scale)


def _const_spec(arr):
    nd = arr.ndim
    return pl.BlockSpec(arr.shape, lambda b, i: (0,) * nd)


def _mla_kv(x, gain, shift, scale, w_dkv_p, lat_norm, w_k, w_v, kg_n, kg_r, cos4, sin4, batch, seq, tm):
    t, d = x.shape
    nt = seq // tm
    row = lambda b, i: (b * nt + i, 0)
    vec = lambda b, i: (b, 0, 0)
    g2 = gain.reshape(1, d)
    weights = [w_dkv_p, lat_norm.reshape(1, -1), w_k, w_v, kg_n, kg_r]
    hk = pl.BlockSpec((1, MLA_HEADS, tm, 2 * LANE), lambda b, i: (b, 0, i, 0))
    hv = pl.BlockSpec((1, MLA_HEADS, 1, MLA_V, tm), lambda b, i: (b, 0, i, 0, 0))
    return pl.pallas_call(
        _mla_kv_kernel,
        out_shape=(jax.ShapeDtypeStruct((batch, MLA_HEADS, seq, 2 * LANE), BF16),
                   jax.ShapeDtypeStruct((batch, MLA_HEADS, nt, MLA_V, tm), BF16)),
        grid=(batch, nt),
        in_specs=[pl.BlockSpec((tm, d), row), _const_spec(g2),
                  pl.BlockSpec((1, 1, d), vec), pl.BlockSpec((1, 1, d), vec)]
                 + [_const_spec(w) for w in weights]
                 + [pl.BlockSpec((tm, LANE), row), pl.BlockSpec((tm, LANE), row)],
        out_specs=(hk, hv),
        scratch_shapes=[pltpu.VMEM((tm, d), BF16)],
        compiler_params=_params(("parallel", "parallel"), 48),
        name="mla_kv",
    )(x, g2, shift[:, None, :], scale[:, None, :], *weights, cos4, sin4)


def _mla_q(x, gain, shift, scale, w_dq, q_lat_norm, w_qn, w_qr, w_qs, qg_n, qg_r, cos4, sin4, batch, seq, tm):
    t, d = x.shape
    nt = seq // tm
    row = lambda b, i: (b * nt + i, 0)
    vec = lambda b, i: (b, 0, 0)
    g2 = gain.reshape(1, d)
    weights = [w_dq, q_lat_norm.reshape(1, -1), w_qn, w_qr, w_qs, qg_n, qg_r]
    hq = pl.BlockSpec((1, MLA_HEADS, tm, 2 * LANE), lambda b, i: (b, 0, i, 0))
    return pl.pallas_call(
        _mla_q_kernel,
        out_shape=jax.ShapeDtypeStruct((batch, MLA_HEADS, seq, 2 * LANE), BF16),
        grid=(batch, nt),
        in_specs=[pl.BlockSpec((tm, d), row), _const_spec(g2),
                  pl.BlockSpec((1, 1, d), vec), pl.BlockSpec((1, 1, d), vec)]
                 + [_const_spec(w) for w in weights]
                 + [pl.BlockSpec((tm, LANE), row), pl.BlockSpec((tm, LANE), row)],
        out_specs=hq,
        scratch_shapes=[pltpu.VMEM((tm, d), BF16)],
        compiler_params=_params(("parallel", "parallel"), 48),
        name="mla_q",
    )(x, g2, shift[:, None, :], scale[:, None, :], *weights, cos4, sin4)


ATTN_HEADS_PER_STEP = 4
ATTN_Q_BLOCKS_PER_STEP = 2
ATTN_MAX_SCORE_BOUND = 60.0


def _attn_kernel(q_ref, k_ref, vt_ref, o_ref, *, tq, running_max):
    first_pair = pl.program_id(2) * (ATTN_Q_BLOCKS_PER_STEP // 2)
    nh = q_ref.shape[1]
    tc = vt_ref.shape[-1]
    dv = vt_ref.shape[-2]
    nt_dims = (((1,), (1,)), ((), ()))

    def step(rows, blk0, nblk, carry, masked):
        tk = nblk * tq
        scores = []
        for h in range(nh):
            k = k_ref[0, h, pl.ds(pl.multiple_of(blk0 * tq, tq), tk), :]
            scores.append(lax.dot_general(k, q_ref[0, h, rows, :], nt_dims, preferred_element_type=F32))
        out = []
        for h in range(nh):
            st = scores[h]
            if masked:
                kv_i = lax.broadcasted_iota(jnp.int32, (tk, tq), 0)
                q_i = lax.broadcasted_iota(jnp.int32, (tk, tq), 1)
                st = jnp.where(kv_i - (nblk - 1) * tq <= q_i, st, -jnp.inf)
            if running_max:
                m, l, acc = carry[h]
                m_new = jnp.maximum(m, jnp.max(st, axis=0, keepdims=True))
                alpha = jnp.exp2(m - m_new)
                pt = jnp.exp2(st - m_new)
                l = alpha * l
                acc = alpha * acc
            else:
                l, acc = carry[h]
                pt = jnp.exp2(st)
            l = l + jnp.sum(pt, axis=0, keepdims=True)
            pt = pt.astype(BF16)
            slab0 = blk0 * (tq // tc)
            pv = jnp.dot(vt_ref[0, h, slab0], pt[0:tc], preferred_element_type=F32)
            for c in range(1, tk // tc):
                pv += jnp.dot(vt_ref[0, h, slab0 + c], pt[c * tc:(c + 1) * tc], preferred_element_type=F32)
            out.append((m_new, l, acc + pv) if running_max else (l, acc + pv))
        return tuple(out)

    stats = (jnp.zeros((1, tq), F32), jnp.zeros((dv, tq), F32))
    if running_max:
        stats = (jnp.full((1, tq), -jnp.inf, F32),) + stats
    init = tuple(stats for _ in range(nh))
    for sub in range(ATTN_Q_BLOCKS_PER_STEP):
        rows = slice(sub * tq, (sub + 1) * tq)
        pairs = first_pair + sub // 2
        carry = lax.fori_loop(0, pairs, lambda j, c: step(rows, 2 * j, 2, c, False), init)
        carry = step(rows, 2 * pairs, sub % 2 + 1, carry, True)
        for h in range(nh):
            l, acc = carry[h][-2:]
            o_ref[0, rows, h * dv:(h + 1) * dv] = (acc / l).T.astype(o_ref.dtype)


def _attention(q, k, vt, tq, running_max):
    b, h, s, dq = q.shape
    _, _, nslab, dvv, tc = vt.shape
    nh = ATTN_HEADS_PER_STEP
    return pl.pallas_call(
        functools.partial(_attn_kernel, tq=tq, running_max=running_max),
        out_shape=jax.ShapeDtypeStruct((b, s, h * dvv), BF16),
        grid=(b, h // nh, s // (ATTN_Q_BLOCKS_PER_STEP * tq)),
        in_specs=[
            pl.BlockSpec((1, nh, ATTN_Q_BLOCKS_PER_STEP * tq, dq), lambda bi, hi, qi: (bi, hi, qi, 0)),
            pl.BlockSpec((1, nh, s, dq), lambda bi, hi, qi: (bi, hi, 0, 0)),
            pl.BlockSpec((1, nh, nslab, dvv, tc), lambda bi, hi, qi: (bi, hi, 0, 0, 0)),
        ],
        out_specs=pl.BlockSpec((1, ATTN_Q_BLOCKS_PER_STEP * tq, nh * dvv), lambda bi, hi, qi: (bi, qi, hi)),
        compiler_params=_params(("parallel", "parallel", "arbitrary"), 48),
        name="mla_attention" if running_max else "mla_attention_bounded",
    )(q, k, vt)


def _rope_tables(positions):
    b, s = positions.shape
    half = MLA_ROPE // 2
    inv_freq = ROPE_THETA ** (-jnp.arange(half, dtype=F32) / half)
    ang = positions.astype(F32)[..., None] * inv_freq
    cos, sin = _trig(ang.reshape(-1, LANE))
    cos = cos.reshape(b * s, half)
    sin = sin.reshape(b * s, half)
    zero = jnp.zeros((b * s, LANE - 2 * half), F32)
    cos4 = jnp.concatenate([cos, cos, zero], axis=-1)
    sin4 = jnp.concatenate([-sin, sin, zero], axis=-1)
    return cos4, sin4


def _split_rope_cols(w_rope):
    half = MLA_ROPE // 2
    x1, x2 = w_rope[..., :half], w_rope[..., half:]
    zero = jnp.zeros(w_rope.shape[:-1] + (LANE - MLA_ROPE,), w_rope.dtype)
    return jnp.concatenate([x1, x2, zero], axis=-1), jnp.concatenate([x2, x1, zero], axis=-1)


def _pad_rope_gain(gain):
    g_n = gain[:MLA_NOPE].reshape(1, MLA_NOPE)
    g_r = jnp.concatenate([gain[MLA_NOPE:], jnp.zeros((LANE - MLA_ROPE,), gain.dtype)]).reshape(1, LANE)
    return g_n, g_r


def kernel(x, c, positions, ada_w, ada_b, norm_mix, norm_ffn, gla_w_in, gla_w_alpha, gla_b_alpha, gla_onorm,
           gla_w_out, mla_w_dq, mla_q_lat_norm, mla_w_uq, mla_q_norm, mla_w_out, kv_ada_w, kv_ada_b, kv_norm,
           kv_w_dkv, kv_lat_norm, kv_w_ukv, kv_k_norm, ffn_w_gu, ffn_w_down):
    batch, seq, d = x.shape
    t = batch * seq
    depth = ada_w.shape[0]
    n_gla = gla_w_in.shape[0]
    xf = x.reshape(t, d)

    tm_big = min(seq, 1024)
    tm_mid = min(seq, 512)
    t_attn = min(tm_mid, seq // ATTN_Q_BLOCKS_PER_STEP)

    c_pad = jnp.pad(c, ((0, ADA_ROWS - batch), (0, 0)))
    cos4, sin4 = _rope_tables(positions)
    w_gu_bf = ffn_w_gu.astype(BF16)
    w_down_bf = ffn_w_down.astype(BF16)

    k_sh = v_sh = None
    for layer in range(depth):
        mod = _ada_mod(c_pad, ada_w, ada_b, layer)[:batch]
        shift_m, scale_m, gate_m, shift_f, scale_f, gate_f = jnp.split(mod, 6, axis=-1)
        if layer < n_gla:
            qkvg = gla_w_in.shape[2] - GLA_GATE_RANK
            w_a = jnp.pad(gla_w_in[layer, :, qkvg:], ((0, 0), (0, LANE - GLA_GATE_RANK))).astype(BF16)
            proj, a_lr = _mod_matmul(xf, norm_mix[layer], shift_m, scale_m, gla_w_in.astype(BF16), layer, qkvg,
                                     w_a, seq, tm_big, 2048)
            w_alpha_p = jnp.pad(gla_w_alpha[layer], ((0, LANE - GLA_GATE_RANK), (0, 0))).astype(BF16)
            mix_in = _gla(proj.reshape(batch, seq, qkvg), a_lr.reshape(batch, seq, LANE), w_alpha_p,
                          gla_b_alpha[layer], gla_onorm[layer], batch, seq).reshape(t, -1)
            w_out = gla_w_out[layer].astype(BF16)
        else:
            j = layer - n_gla
            if j == 0:
                kv_mod = _ada_mod(c_pad, kv_ada_w[None], kv_ada_b[None], 0)[:batch]
                kv_shift, kv_scale = jnp.split(kv_mod, 2, axis=-1)
                pe_p, pe_s = _split_rope_cols(kv_w_dkv[:, KV_LORA:])
                w_dkv_p = jnp.concatenate([kv_w_dkv[:, :KV_LORA], pe_p, pe_s], axis=1).astype(BF16)
                w_ukv = kv_w_ukv.reshape(KV_LORA, MLA_HEADS, MLA_NOPE + MLA_V)
                w_k = w_ukv[:, :, :MLA_NOPE].reshape(KV_LORA, MLA_HEADS * MLA_NOPE).astype(BF16)
                w_v = w_ukv[:, :, MLA_NOPE:].reshape(KV_LORA, MLA_HEADS * MLA_V).astype(BF16)
                kg_n, kg_r = _pad_rope_gain(kv_k_norm)
                k_sh, v_sh = _mla_kv(xf, kv_norm, kv_shift, kv_scale, w_dkv_p, kv_lat_norm, w_k, w_v,
                                     kg_n, kg_r, cos4, sin4, batch, seq, t_attn)
            q_lora = mla_w_dq.shape[2]
            w_uq = mla_w_uq[j].reshape(q_lora, MLA_HEADS, MLA_NOPE + MLA_ROPE)
            w_qn = w_uq[:, :, :MLA_NOPE].reshape(q_lora, MLA_HEADS * MLA_NOPE).astype(BF16)
            r_p, r_s = _split_rope_cols(w_uq[:, :, MLA_NOPE:])
            w_qr = r_p.reshape(q_lora, MLA_HEADS * LANE).astype(BF16)
            w_qs = r_s.reshape(q_lora, MLA_HEADS * LANE).astype(BF16)
            qg_n, qg_r = _pad_rope_gain(mla_q_norm[j])
            q = _mla_q(xf, norm_mix[layer], shift_m, scale_m, mla_w_dq[j].astype(BF16), mla_q_lat_norm[j],
                       w_qn, w_qr, w_qs, qg_n, qg_r, cos4, sin4, batch, seq, tm_mid)
            score_bound = ((MLA_NOPE + MLA_ROPE) ** 0.5 * LOG2_E
                           * jnp.max(jnp.abs(mla_q_norm[j])) * jnp.max(jnp.abs(kv_k_norm)))
            mix_in = lax.cond(score_bound <= ATTN_MAX_SCORE_BOUND,
                              lambda: _attention(q, k_sh, v_sh, t_attn, False),
                              lambda: _attention(q, k_sh, v_sh, t_attn, True))
            mix_in = mix_in.reshape(t, MLA_HEADS * MLA_V)
            w_out = mla_w_out[j].astype(BF16)
        xf = _matmul_residual(mix_in, w_out, xf, gate_m, seq, tm_mid, d)
        xf = _ffn(xf, norm_ffn[layer], shift_f, scale_f, gate_f, w_gu_bf, w_down_bf, layer, seq, tm_big, 512)
    return xf.reshape(batch, seq, d)
```

```python
import functools

import jax
import jax.numpy as jnp
from jax import lax
from jax.experimental import pallas as pl
from jax.experimental.pallas import tpu as pltpu

GLA_HEADS = 4
GLA_GATE_RANK = 16
GLA_TAU = 16.0
GLA_CHUNK = 64
MLA_HEADS = 16
MLA_NOPE = 128
MLA_ROPE = 64
MLA_V = 128
KV_LORA = 512
ROPE_THETA = 10000.0
EPS = 1e-6
LOG2_E = 1.4426950408889634

LANE = 128
MIB = 1 << 20

F32 = jnp.float32
BF16 = jnp.bfloat16


def _params(semantics, vmem_mib):
    return pltpu.CompilerParams(dimension_semantics=semantics, vmem_limit_bytes=vmem_mib * MIB)


def _silu(x):
    return x * jax.nn.sigmoid(x)


def _rms_scale(x):
    return x * lax.rsqrt(jnp.mean(x * x, axis=-1, keepdims=True) + EPS)


ADA_ROWS = 16


def _split_bf16(x):
    hi = x.astype(BF16)
    return hi, (x - hi.astype(F32)).astype(BF16)


def _ada_kernel(c_ref, w_ref, b_ref, o_ref):
    s_hi, s_lo = _split_bf16(_silu(c_ref[...]))
    w_hi, w_lo = _split_bf16(w_ref[...])
    acc = jnp.dot(s_hi, w_lo, preferred_element_type=F32) + jnp.dot(s_lo, w_hi, preferred_element_type=F32)
    o_ref[...] = (acc + jnp.dot(s_hi, w_hi, preferred_element_type=F32)) + b_ref[...]


def _ada_mod(c_pad, w, b, layer, tn=1024):
    rows, d = c_pad.shape
    n = w.shape[2]
    return pl.pallas_call(
        _ada_kernel,
        out_shape=jax.ShapeDtypeStruct((rows, n), F32),
        grid=(n // tn,),
        in_specs=[
            pl.BlockSpec((rows, d), lambda j: (0, 0)),
            pl.BlockSpec((None, d, tn), lambda j: (layer, 0, j)),
            pl.BlockSpec((None, 1, tn), lambda j: (layer, 0, j)),
        ],
        out_specs=pl.BlockSpec((rows, tn), lambda j: (0, j)),
        compiler_params=_params(("parallel",), 48),
        name="ada_mod",
    )(c_pad, w, b[:, None, :])


def _trig_kernel(a_ref, c_ref, s_ref):
    a = a_ref[...]
    c_ref[...] = jnp.cos(a)
    s_ref[...] = jnp.sin(a)


def _trig(ang):
    rows = ang.shape[0]
    tr = min(rows, 512)
    spec = pl.BlockSpec((tr, LANE), lambda i: (i, 0))
    return pl.pallas_call(
        _trig_kernel,
        out_shape=(jax.ShapeDtypeStruct(ang.shape, F32),) * 2,
        grid=(rows // tr,),
        in_specs=[spec],
        out_specs=(spec, spec),
        compiler_params=_params(("parallel",), 16),
        name="rope_trig",
    )(ang)


MOD_ROWS = 16


def _modulate_into(h_ref, x_ref, g_ref, sh_ref, sc_ref):
    tm = x_ref.shape[0]
    step = min(tm, MOD_ROWS)
    gs = g_ref[...] * (1.0 + sc_ref[0])
    shift = sh_ref[0]
    for r in range(tm // step):
        rows = slice(r * step, (r + 1) * step)
        h_ref[rows, :] = (_rms_scale(x_ref[rows, :]) * gs + shift).astype(h_ref.dtype)


def _modmm_kernel(x_ref, g_ref, sh_ref, sc_ref, w_ref, wa_ref, o_ref, a_ref, h_ref):
    @pl.when(pl.program_id(1) == 0)
    def _():
        _modulate_into(h_ref, x_ref, g_ref, sh_ref, sc_ref)
        a_ref[...] = jnp.dot(h_ref[...], wa_ref[...], preferred_element_type=F32).astype(a_ref.dtype)

    o_ref[...] = jnp.dot(h_ref[...], w_ref[...], preferred_element_type=F32).astype(o_ref.dtype)


def _mod_matmul(x, gain, shift, scale, w, layer, n, w_a, seq, tm, tn):
    t, d = x.shape
    per_b = seq // tm
    vec = lambda i, j: (i // per_b, 0, 0)
    return pl.pallas_call(
        _modmm_kernel,
        out_shape=(jax.ShapeDtypeStruct((t, n), BF16), jax.ShapeDtypeStruct((t, LANE), BF16)),
        grid=(t // tm, n // tn),
        in_specs=[
            pl.BlockSpec((tm, d), lambda i, j: (i, 0)),
            pl.BlockSpec((1, d), lambda i, j: (0, 0)),
            pl.BlockSpec((1, 1, d), vec),
            pl.BlockSpec((1, 1, d), vec),
            pl.BlockSpec((None, d, tn), lambda i, j: (layer, 0, j)),
            pl.BlockSpec((d, LANE), lambda i, j: (0, 0)),
        ],
        out_specs=(pl.BlockSpec((tm, tn), lambda i, j: (i, j)),
                   pl.BlockSpec((tm, LANE), lambda i, j: (i, 0))),
        scratch_shapes=[pltpu.VMEM((tm, d), BF16)],
        compiler_params=_params(("parallel", "arbitrary"), 56),
        name="mod_matmul",
    )(x, gain.reshape(1, d), shift[:, None, :], scale[:, None, :], w, w_a)


def _mm_res_kernel(a_ref, w_ref, x_ref, gate_ref, o_ref):
    acc = jnp.dot(a_ref[...], w_ref[...], preferred_element_type=F32)
    o_ref[...] = x_ref[...] + gate_ref[0] * acc


def _matmul_residual(a, w, x, gate, seq, tm, tn):
    t, k = a.shape
    n = w.shape[1]
    per_b = seq // tm
    return pl.pallas_call(
        _mm_res_kernel,
        out_shape=jax.ShapeDtypeStruct((t, n), F32),
        grid=(t // tm, n // tn),
        in_specs=[
            pl.BlockSpec((tm, k), lambda i, j: (i, 0)),
            pl.BlockSpec((k, tn), lambda i, j: (0, j)),
            pl.BlockSpec((tm, tn), lambda i, j: (i, j)),
            pl.BlockSpec((1, 1, tn), lambda i, j: (i // per_b, 0, j)),
        ],
        out_specs=pl.BlockSpec((tm, tn), lambda i, j: (i, j)),
        compiler_params=_params(("parallel", "arbitrary"), 48),
        name="matmul_residual",
    )(a, w, x, gate[:, None, :])


FFN_BLOCKS_PER_STEP = 2
FFN_ROW_SPLIT = 2


def _ffn_kernel(x_hbm, g_ref, sh_ref, sc_ref, gate_ref, wga_ref, wua_ref, wda_ref, wgb_ref, wub_ref, wdb_ref,
                o_ref, xs_ref, h_ref, sem, *, nf):
    i, f = pl.program_id(0), pl.program_id(1)
    tm = xs_ref.shape[0]

    def x_copy(tile):
        return pltpu.make_async_copy(x_hbm.at[pl.ds(pl.multiple_of(tile * tm, tm), tm), :], xs_ref, sem)

    @pl.when(jnp.logical_and(f == 1, i + 1 < pl.num_programs(0)))
    def _():
        x_copy(i + 1).start()

    def block(wg_ref, wu_ref, wd_ref):
        for r in range(FFN_ROW_SPLIT):
            rows = slice(r * tm // FFN_ROW_SPLIT, (r + 1) * tm // FFN_ROW_SPLIT)
            h = h_ref[rows, :]
            g = jnp.dot(h, wg_ref[...], preferred_element_type=F32)
            u = jnp.dot(h, wu_ref[...], preferred_element_type=F32)
            a = (_silu(g) * u).astype(BF16)
            o_ref[rows, :] += gate_ref[0] * jnp.dot(a, wd_ref[...], preferred_element_type=F32)

    has_second = FFN_BLOCKS_PER_STEP * f + 1 < nf

    def blocks(second):
        block(wga_ref, wua_ref, wda_ref)
        if second:
            block(wgb_ref, wub_ref, wdb_ref)

    @pl.when(f == 0)
    def _():
        @pl.when(i == 0)
        def _():
            x_copy(0).start()

        x_copy(i).wait()
        _modulate_into(h_ref, xs_ref, g_ref, sh_ref, sc_ref)
        o_ref[...] = xs_ref[...]
        blocks(nf > 1)

    @pl.when(jnp.logical_and(f > 0, has_second))
    def _():
        blocks(True)

    @pl.when(jnp.logical_and(f > 0, jnp.logical_not(has_second)))
    def _():
        blocks(False)


def _ffn(x, gain, shift, scale, gate, w_gu, w_down, layer, seq, tm, tf):
    t, d = x.shape
    dff = w_down.shape[1]
    nf = dff // tf
    per_b = seq // tm
    vec = lambda i, f: (i // per_b, 0, 0)
    blk_a = lambda f: FFN_BLOCKS_PER_STEP * f
    blk_b = lambda f: jnp.minimum(FFN_BLOCKS_PER_STEP * f + 1, nf - 1)
    w_specs = []
    for blk in (blk_a, blk_b):
        w_specs += [
            pl.BlockSpec((None, d, tf), lambda i, f, blk=blk: (layer, 0, blk(f))),
            pl.BlockSpec((None, d, tf), lambda i, f, blk=blk: (layer, 0, blk(f) + nf)),
            pl.BlockSpec((None, tf, d), lambda i, f, blk=blk: (layer, blk(f), 0)),
        ]
    return pl.pallas_call(
        functools.partial(_ffn_kernel, nf=nf),
        out_shape=jax.ShapeDtypeStruct((t, d), F32),
        grid=(t // tm, pl.cdiv(nf, FFN_BLOCKS_PER_STEP)),
        in_specs=[
            pl.BlockSpec(memory_space=pl.ANY),
            pl.BlockSpec((1, d), lambda i, f: (0, 0)),
            pl.BlockSpec((1, 1, d), vec),
            pl.BlockSpec((1, 1, d), vec),
            pl.BlockSpec((1, 1, d), vec),
        ] + w_specs,
        out_specs=pl.BlockSpec((tm, d), lambda i, f: (i, 0)),
        scratch_shapes=[pltpu.VMEM((tm, d), F32), pltpu.VMEM((tm, d), BF16), pltpu.SemaphoreType.DMA(())],
        compiler_params=_params(("arbitrary", "arbitrary"), 60),
        name="ffn",
    )(x, gain.reshape(1, d), shift[:, None, :], scale[:, None, :], gate[:, None, :],
      w_gu, w_gu, w_down, w_gu, w_gu, w_down)


GLA_PAIR = 2 * GLA_CHUNK


GLA_BATCH_PER_STEP = 2


def _gla_pair(q_ref, k_ref, v_ref, g_ref, a_ref, wal_ref, bal_ref, on_ref, o_ref, st_ref, bb, dk, dv):
    ch = GLA_CHUNK
    tp = q_ref.shape[1]
    qk = q_ref.shape[2]

    z = jnp.dot(a_ref[bb], wal_ref[...], preferred_element_type=F32) + bal_ref[...]
    la = (jnp.minimum(z, 0.0) - jnp.log(1.0 + jnp.exp(-jnp.abs(z)))) * (LOG2_E / GLA_TAU)

    row = lax.broadcasted_iota(jnp.int32, (tp, tp), 0)
    col = lax.broadcasted_iota(jnp.int32, (tp, tp), 1)
    causal = jnp.logical_and(col <= row, col >= (row // ch) * ch)
    cross = jnp.logical_and(row >= ch, col < ch)
    tri = jnp.where(causal, 1.0, 0.0).astype(BF16)
    first = lax.broadcasted_iota(jnp.int32, (tp, qk), 0) < ch
    first_h = lax.broadcasted_iota(jnp.int32, (tp, dk), 0) < ch

    la_hi = la.astype(BF16)
    la_lo = (la - la_hi.astype(F32)).astype(BF16)
    bcum = jnp.dot(tri, la_hi, preferred_element_type=F32) + jnp.dot(tri, la_lo, preferred_element_type=F32)
    b_last0 = bcum[ch - 1:ch]
    b_last1 = bcum[tp - 1:tp]
    e_q = jnp.exp2(bcum)
    e_ki = jnp.exp2(-bcum)
    e_ks = jnp.exp2(jnp.where(first, b_last0, b_last1) - bcum)
    dec0 = jnp.exp2(b_last0)
    dec1 = jnp.exp2(b_last1)
    dec01 = jnp.exp2(b_last0 + b_last1)
    on = on_ref[...]
    tn_dims = (((0,), (0,)), ((), ()))
    nt_dims = (((1,), (1,)), ((), ()))

    for h in range(GLA_HEADS):
        ks = slice(h * dk, (h + 1) * dk)
        vs = slice(h * dv, (h + 1) * dv)
        q = q_ref[bb, :, ks].astype(F32) * (dk ** -0.5)
        k = k_ref[bb, :, ks].astype(F32)
        v = v_ref[bb, :, vs]
        q_dec = q * e_q[:, ks]
        k_st = k * e_ks[:, ks]
        q_b = q_dec.astype(BF16)
        k_in = (k * e_ki[:, ks]).astype(BF16)
        k0 = jnp.where(first_h, k_st, 0.0).astype(BF16)
        att = lax.dot_general(q_b, k_in, nt_dims, preferred_element_type=F32)
        att_x = lax.dot_general(q_b, k0, nt_dims, preferred_element_type=F32)
        att = jnp.where(causal, att, jnp.where(cross, att_x, 0.0)).astype(BF16)
        s0 = st_ref[bb, h]
        q_s = jnp.where(first_h, q_dec, q_dec * dec0[:, ks]).astype(BF16)
        o = (jnp.dot(att, v, preferred_element_type=F32)
             + jnp.dot(q_s, s0.astype(BF16), preferred_element_type=F32))
        k_u = jnp.where(first_h, k_st * dec1[:, ks], k_st).astype(BF16)
        dec_cols = jnp.tile(jnp.broadcast_to(dec01[:, ks], (LANE, dk)).T, (1, dv // LANE))
        st_ref[bb, h] = dec_cols * s0 + lax.dot_general(k_u, v, tn_dims, preferred_element_type=F32)
        gg = g_ref[bb, :, vs].astype(F32)
        o_ref[bb, :, vs] = ((_rms_scale(o) * on) * _silu(gg)).astype(o_ref.dtype)


def _gla_kernel(q_ref, k_ref, v_ref, g_ref, a_ref, wal_ref, bal_ref, on_ref, o_ref, st_ref, *, dk, dv):
    @pl.when(pl.program_id(1) == 0)
    def _():
        st_ref[...] = jnp.zeros_like(st_ref)

    for bb in range(q_ref.shape[0]):
        _gla_pair(q_ref, k_ref, v_ref, g_ref, a_ref, wal_ref, bal_ref, on_ref, o_ref, st_ref, bb, dk, dv)


def _gla(proj, a_lr, w_alpha_p, b_alpha, onorm, batch, seq):
    dk = w_alpha_p.shape[1] // GLA_HEADS
    dv = onorm.shape[0]
    qk = GLA_HEADS * dk
    vv = GLA_HEADS * dv
    tp = GLA_PAIR
    nb = GLA_BATCH_PER_STEP if batch % GLA_BATCH_PER_STEP == 0 else 1
    return pl.pallas_call(
        functools.partial(_gla_kernel, dk=dk, dv=dv),
        out_shape=jax.ShapeDtypeStruct((batch, seq, vv), BF16),
        grid=(batch // nb, seq // tp),
        in_specs=[
            pl.BlockSpec((nb, tp, qk), lambda b, i: (b, i, 0)),
            pl.BlockSpec((nb, tp, qk), lambda b, i: (b, i, 1)),
            pl.BlockSpec((nb, tp, vv), lambda b, i: (b, i, 2 * qk // vv)),
            pl.BlockSpec((nb, tp, vv), lambda b, i: (b, i, 2 * qk // vv + 1)),
            pl.BlockSpec((nb, tp, LANE), lambda b, i: (b, i, 0)),
            pl.BlockSpec((LANE, qk), lambda b, i: (0, 0)),
            pl.BlockSpec((1, qk), lambda b, i: (0, 0)),
            pl.BlockSpec((1, dv), lambda b, i: (0, 0)),
        ],
        out_specs=pl.BlockSpec((nb, tp, vv), lambda b, i: (b, i, 0)),
        scratch_shapes=[pltpu.VMEM((nb, GLA_HEADS, dk, dv), F32)],
        compiler_params=_params(("parallel", "arbitrary"), 56),
        name="gla",
    )(proj, proj, proj, proj, a_lr, w_alpha_p, b_alpha.reshape(1, qk), onorm.reshape(1, dv))


def _rope_pair(x, x_swapped, cos4, sin4):
    return x * cos4 + x_swapped * sin4


def _head_norm_store(o_ref, h, rows, nope, rope_sq, rope_g, gain_n, dim, post_scale):
    ss = jnp.sum(nope * nope + rope_sq, axis=-1, keepdims=True)
    r = lax.rsqrt(ss / dim + EPS)
    if post_scale != 1.0:
        r = r * post_scale
    o_ref[0, h, rows, 0:LANE] = ((nope * r) * gain_n).astype(o_ref.dtype)
    o_ref[0, h, rows, LANE:2 * LANE] = (rope_g * r).astype(o_ref.dtype)


MLA_SUB = 256


def _sub_tiles(tm):
    step = min(tm, MLA_SUB)
    return [slice(r * step, (r + 1) * step) for r in range(tm // step)]


def _mla_kv_kernel(x_ref, g_ref, sh_ref, sc_ref, wd_ref, ln_ref, wk_ref, wv_ref, kn_ref, kr_ref,
                   cos_ref, sin_ref, k_ref, v_ref, h_ref):
    lora = ln_ref.shape[1]
    gn = kn_ref[...]
    gr = kr_ref[...]
    _modulate_into(h_ref, x_ref, g_ref, sh_ref, sc_ref)
    for rows in _sub_tiles(x_ref.shape[0]):
        ckv = jnp.dot(h_ref[rows, :], wd_ref[...], preferred_element_type=F32)
        c_lat = (_rms_scale(ckv[:, :lora]) * ln_ref[...]).astype(BF16)
        k_pe = _rope_pair(ckv[:, lora:lora + LANE], ckv[:, lora + LANE:lora + 2 * LANE],
                          cos_ref[rows, :], sin_ref[rows, :])
        kn = jnp.dot(c_lat, wk_ref[...], preferred_element_type=F32)
        vv = jnp.dot(c_lat, wv_ref[...], preferred_element_type=F32)
        pe_sq = k_pe * k_pe
        pe_g = k_pe * gr
        for h in range(MLA_HEADS):
            cols = slice(h * LANE, (h + 1) * LANE)
            _head_norm_store(k_ref, h, rows, kn[:, cols], pe_sq, pe_g, gn, MLA_NOPE + MLA_ROPE, 1.0)
            v_ref[0, h, 0, :, rows] = vv[:, cols].T.astype(v_ref.dtype)


def _mla_q_kernel(x_ref, g_ref, sh_ref, sc_ref, wd_ref, ln_ref, wn_ref, wr_ref, ws_ref, qn_ref, qr_ref,
                  cos_ref, sin_ref, q_ref, h_ref):
    gn = qn_ref[...]
    gr = qr_ref[...]
    sm_scale = (MLA_NOPE + MLA_ROPE) ** -0.5 * LOG2_E
    _modulate_into(h_ref, x_ref, g_ref, sh_ref, sc_ref)
    for rows in _sub_tiles(x_ref.shape[0]):
        cq = jnp.dot(h_ref[rows, :], wd_ref[...], preferred_element_type=F32)
        cq = (_rms_scale(cq) * ln_ref[...]).astype(BF16)
        qn = jnp.dot(cq, wn_ref[...], preferred_element_type=F32)
        qr = jnp.dot(cq, wr_ref[...], preferred_element_type=F32)
        qs = jnp.dot(cq, ws_ref[...], preferred_element_type=F32)
        cos4 = cos_ref[rows, :]
        sin4 = sin_ref[rows, :]
        for h in range(MLA_HEADS):
            cols = slice(h * LANE, (h + 1) * LANE)
            rope = _rope_pair(qr[:, cols], qs[:, cols], cos4, sin4)
            _head_norm_store(q_ref, h, rows, qn[:, cols], rope * rope, rope * gr, gn, MLA_NOPE + MLA_ROPE, sm_scale)


def _const_spec(arr):
    nd = arr.ndim
    return pl.BlockSpec(arr.shape, lambda b, i: (0,) * nd)


def _mla_kv(x, gain, shift, scale, w_dkv_p, lat_norm, w_k, w_v, kg_n, kg_r, cos4, sin4, batch, seq, tm):
    t, d = x.shape
    nt = seq // tm
    row = lambda b, i: (b * nt + i, 0)
    vec = lambda b, i: (b, 0, 0)
    g2 = gain.reshape(1, d)
    weights = [w_dkv_p, lat_norm.reshape(1, -1), w_k, w_v, kg_n, kg_r]
    hk = pl.BlockSpec((1, MLA_HEADS, tm, 2 * LANE), lambda b, i: (b, 0, i, 0))
    hv = pl.BlockSpec((1, MLA_HEADS, 1, MLA_V, tm), lambda b, i: (b, 0, i, 0, 0))
    return pl.pallas_call(
        _mla_kv_kernel,
        out_shape=(jax.ShapeDtypeStruct((batch, MLA_HEADS, seq, 2 * LANE), BF16),
                   jax.ShapeDtypeStruct((batch, MLA_HEADS, nt, MLA_V, tm), BF16)),
        grid=(batch, nt),
        in_specs=[pl.BlockSpec((tm, d), row), _const_spec(g2),
                  pl.BlockSpec((1, 1, d), vec), pl.BlockSpec((1, 1, d), vec)]
                 + [_const_spec(w) for w in weights]
                 + [pl.BlockSpec((tm, LANE), row), pl.BlockSpec((tm, LANE), row)],
        out_specs=(hk, hv),
        scratch_shapes=[pltpu.VMEM((tm, d), BF16)],
        compiler_params=_params(("parallel", "parallel"), 48),
        name="mla_kv",
    )(x, g2, shift[:, None, :], scale[:, None, :], *weights, cos4, sin4)


def _mla_q(x, gain, shift, scale, w_dq, q_lat_norm, w_qn, w_qr, w_qs, qg_n, qg_r, cos4, sin4, batch, seq, tm):
    t, d = x.shape
    nt = seq // tm
    row = lambda b, i: (b * nt + i, 0)
    vec = lambda b, i: (b, 0, 0)
    g2 = gain.reshape(1, d)
    weights = [w_dq, q_lat_norm.reshape(1, -1), w_qn, w_qr, w_qs, qg_n, qg_r]
    hq = pl.BlockSpec((1, MLA_HEADS, tm, 2 * LANE), lambda b, i: (b, 0, i, 0))
    return pl.pallas_call(
        _mla_q_kernel,
        out_shape=jax.ShapeDtypeStruct((batch, MLA_HEADS, seq, 2 * LANE), BF16),
        grid=(batch, nt),
        in_specs=[pl.BlockSpec((tm, d), row), _const_spec(g2),
                  pl.BlockSpec((1, 1, d), vec), pl.BlockSpec((1, 1, d), vec)]
                 + [_const_spec(w) for w in weights]
                 + [pl.BlockSpec((tm, LANE), row), pl.BlockSpec((tm, LANE), row)],
        out_specs=hq,
        scratch_shapes=[pltpu.VMEM((tm, d), BF16)],
        compiler_params=_params(("parallel", "parallel"), 48),
        name="mla_q",
    )(x, g2, shift[:, None, :], scale[:, None, :], *weights, cos4, sin4)


ATTN_HEADS_PER_STEP = 4
ATTN_Q_BLOCKS_PER_STEP = 2
ATTN_MAX_SCORE_BOUND = 60.0


def _attn_kernel(q_ref, k_ref, vt_ref, o_ref, *, tq, running_max):
    first_pair = pl.program_id(2) * (ATTN_Q_BLOCKS_PER_STEP // 2)
    nh = q_ref.shape[1]
    tc = vt_ref.shape[-1]
    dv = vt_ref.shape[-2]
    nt_dims = (((1,), (1,)), ((), ()))

    def step(rows, blk0, nblk, carry, masked):
        tk = nblk * tq
        scores = []
        for h in range(nh):
            k = k_ref[0, h, pl.ds(pl.multiple_of(blk0 * tq, tq), tk), :]
            scores.append(lax.dot_general(k, q_ref[0, h, rows, :], nt_dims, preferred_element_type=F32))
        out = []
        for h in range(nh):
            st = scores[h]
            if masked:
                kv_i = lax.broadcasted_iota(jnp.int32, (tk, tq), 0)
                q_i = lax.broadcasted_iota(jnp.int32, (tk, tq), 1)
                st = jnp.where(kv_i - (nblk - 1) * tq <= q_i, st, -jnp.inf)
            if running_max:
                m, l, acc = carry[h]
                m_new = jnp.maximum(m, jnp.max(st, axis=0, keepdims=True))
                alpha = jnp.exp2(m - m_new)
                pt = jnp.exp2(st - m_new)
                l = alpha * l
                acc = alpha * acc
            else:
                l, acc = carry[h]
                pt = jnp.exp2(st)
            l = l + jnp.sum(pt, axis=0, keepdims=True)
            pt = pt.astype(BF16)
            slab0 = blk0 * (tq // tc)
            pv = jnp.dot(vt_ref[0, h, slab0], pt[0:tc], preferred_element_type=F32)
            for c in range(1, tk // tc):
                pv += jnp.dot(vt_ref[0, h, slab0 + c], pt[c * tc:(c + 1) * tc], preferred_element_type=F32)
            out.append((m_new, l, acc + pv) if running_max else (l, acc + pv))
        return tuple(out)

    stats = (jnp.zeros((1, tq), F32), jnp.zeros((dv, tq), F32))
    if running_max:
        stats = (jnp.full((1, tq), -jnp.inf, F32),) + stats
    init = tuple(stats for _ in range(nh))
    for sub in range(ATTN_Q_BLOCKS_PER_STEP):
        rows = slice(sub * tq, (sub + 1) * tq)
        pairs = first_pair + sub // 2
        carry = lax.fori_loop(0, pairs, lambda j, c: step(rows, 2 * j, 2, c, False), init)
        carry = step(rows, 2 * pairs, sub % 2 + 1, carry, True)
        for h in range(nh):
            l, acc = carry[h][-2:]
            o_ref[0, rows, h * dv:(h + 1) * dv] = (acc / l).T.astype(o_ref.dtype)


def _attention(q, k, vt, tq, running_max):
    b, h, s, dq = q.shape
    _, _, nslab, dvv, tc = vt.shape
    nh = ATTN_HEADS_PER_STEP
    return pl.pallas_call(
        functools.partial(_attn_kernel, tq=tq, running_max=running_max),
        out_shape=jax.ShapeDtypeStruct((b, s, h * dvv), BF16),
        grid=(b, h // nh, s // (ATTN_Q_BLOCKS_PER_STEP * tq)),
        in_specs=[
            pl.BlockSpec((1, nh, ATTN_Q_BLOCKS_PER_STEP * tq, dq), lambda bi, hi, qi: (bi, hi, qi, 0)),
            pl.BlockSpec((1, nh, s, dq), lambda bi, hi, qi: (bi, hi, 0, 0)),
            pl.BlockSpec((1, nh, nslab, dvv, tc), lambda bi, hi, qi: (bi, hi, 0, 0, 0)),
        ],
        out_specs=pl.BlockSpec((1, ATTN_Q_BLOCKS_PER_STEP * tq, nh * dvv), lambda bi, hi, qi: (bi, qi, hi)),
        compiler_params=_params(("parallel", "parallel", "arbitrary"), 48),
        name="mla_attention" if running_max else "mla_attention_bounded",
    )(q, k, vt)


def _rope_tables(positions):
    b, s = positions.shape
    half = MLA_ROPE // 2
    inv_freq = ROPE_THETA ** (-jnp.arange(half, dtype=F32) / half)
    ang = positions.astype(F32)[..., None] * inv_freq
    cos, sin = _trig(ang.reshape(-1, LANE))
    cos = cos.reshape(b * s, half)
    sin = sin.reshape(b * s, half)
    zero = jnp.zeros((b * s, LANE - 2 * half), F32)
    cos4 = jnp.concatenate([cos, cos, zero], axis=-1)
    sin4 = jnp.concatenate([-sin, sin, zero], axis=-1)
    return cos4, sin4


def _split_rope_cols(w_rope):
    half = MLA_ROPE // 2
    x1, x2 = w_rope[..., :half], w_rope[..., half:]
    zero = jnp.zeros(w_rope.shape[:-1] + (LANE - MLA_ROPE,), w_rope.dtype)
    return jnp.concatenate([x1, x2, zero], axis=-1), jnp.concatenate([x2, x1, zero], axis=-1)


def _pad_rope_gain(gain):
    g_n = gain[:MLA_NOPE].reshape(1, MLA_NOPE)
    g_r = jnp.concatenate([gain[MLA_NOPE:], jnp.zeros((LANE - MLA_ROPE,), gain.dtype)]).reshape(1, LANE)
    return g_n, g_r


def kernel(x, c, positions, ada_w, ada_b, norm_mix, norm_ffn, gla_w_in, gla_w_alpha, gla_b_alpha, gla_onorm,
           gla_w_out, mla_w_dq, mla_q_lat_norm, mla_w_uq, mla_q_norm, mla_w_out, kv_ada_w, kv_ada_b, kv_norm,
           kv_w_dkv, kv_lat_norm, kv_w_ukv, kv_k_norm, ffn_w_gu, ffn_w_down):
    batch, seq, d = x.shape
    t = batch * seq
    depth = ada_w.shape[0]
    n_gla = gla_w_in.shape[0]
    xf = x.reshape(t, d)

    tm_big = min(seq, 1024)
    tm_mid = min(seq, 512)
    t_attn = min(tm_mid, seq // ATTN_Q_BLOCKS_PER_STEP)

    c_pad = jnp.pad(c, ((0, ADA_ROWS - batch), (0, 0)))
    cos4, sin4 = _rope_tables(positions)
    w_gu_bf = ffn_w_gu.astype(BF16)
    w_down_bf = ffn_w_down.astype(BF16)

    k_sh = v_sh = None
    for layer in range(depth):
        mod = _ada_mod(c_pad, ada_w, ada_b, layer)[:batch]
        shift_m, scale_m, gate_m, shift_f, scale_f, gate_f = jnp.split(mod, 6, axis=-1)
        if layer < n_gla:
            qkvg = gla_w_in.shape[2] - GLA_GATE_RANK
            w_a = jnp.pad(gla_w_in[layer, :, qkvg:], ((0, 0), (0, LANE - GLA_GATE_RANK))).astype(BF16)
            proj, a_lr = _mod_matmul(xf, norm_mix[layer], shift_m, scale_m, gla_w_in.astype(BF16), layer, qkvg,
                                     w_a, seq, tm_big, 2048)
            w_alpha_p = jnp.pad(gla_w_alpha[layer], ((0, LANE - GLA_GATE_RANK), (0, 0))).astype(BF16)
            mix_in = _gla(proj.reshape(batch, seq, qkvg), a_lr.reshape(batch, seq, LANE), w_alpha_p,
                          gla_b_alpha[layer], gla_onorm[layer], batch, seq).reshape(t, -1)
            w_out = gla_w_out[layer].astype(BF16)
        else:
            j = layer - n_gla
            if j == 0:
                kv_mod = _ada_mod(c_pad, kv_ada_w[None], kv_ada_b[None], 0)[:batch]
                kv_shift, kv_scale = jnp.split(kv_mod, 2, axis=-1)
                pe_p, pe_s = _split_rope_cols(kv_w_dkv[:, KV_LORA:])
                w_dkv_p = jnp.concatenate([kv_w_dkv[:, :KV_LORA], pe_p, pe_s], axis=1).astype(BF16)
                w_ukv = kv_w_ukv.reshape(KV_LORA, MLA_HEADS, MLA_NOPE + MLA_V)
                w_k = w_ukv[:, :, :MLA_NOPE].reshape(KV_LORA, MLA_HEADS * MLA_NOPE).astype(BF16)
                w_v = w_ukv[:, :, MLA_NOPE:].reshape(KV_LORA, MLA_HEADS * MLA_V).astype(BF16)
                kg_n, kg_r = _pad_rope_gain(kv_k_norm)
                k_sh, v_sh = _mla_kv(xf, kv_norm, kv_shift, kv_scale, w_dkv_p, kv_lat_norm, w_k, w_v,
                                     kg_n, kg_r, cos4, sin4, batch, seq, t_attn)
            q_lora = mla_w_dq.shape[2]
            w_uq = mla_w_uq[j].reshape(q_lora, MLA_HEADS, MLA_NOPE + MLA_ROPE)
            w_qn = w_uq[:, :, :MLA_NOPE].reshape(q_lora, MLA_HEADS * MLA_NOPE).astype(BF16)
            r_p, r_s = _split_rope_cols(w_uq[:, :, MLA_NOPE:])
            w_qr = r_p.reshape(q_lora, MLA_HEADS * LANE).astype(BF16)
            w_qs = r_s.reshape(q_lora, MLA_HEADS * LANE).astype(BF16)
            qg_n, qg_r = _pad_rope_gain(mla_q_norm[j])
            q = _mla_q(xf, norm_mix[layer], shift_m, scale_m, mla_w_dq[j].astype(BF16), mla_q_lat_norm[j],
                       w_qn, w_qr, w_qs, qg_n, qg_r, cos4, sin4, batch, seq, tm_mid)
            score_bound = ((MLA_NOPE + MLA_ROPE) ** 0.5 * LOG2_E
                           * jnp.max(jnp.abs(mla_q_norm[j])) * jnp.max(jnp.abs(kv_k_norm)))
            mix_in = lax.cond(score_bound <= ATTN_MAX_SCORE_BOUND,
                              lambda: _attention(q, k_sh, v_sh, t_attn, False),
                              lambda: _attention(q, k_sh, v_sh, t_attn, True))
            mix_in = mix_in.reshape(t, MLA_HEADS * MLA_V)
            w_out = mla_w_out[j].astype(BF16)
        xf = _matmul_residual(mix_in, w_out, xf, gate_m, seq, tm_mid, d)
        xf = _ffn(xf, norm_ffn[layer], shift_f, scale_f, gate_f, w_gu_bf, w_down_bf, layer, seq, tm_big, 512)
    return xf.reshape(batch, seq, d)
```
